```python
import math, functools
import jax, jax.numpy as jnp
from jax import lax
import numpy as np

D_MODEL = 1024
BATCH = 4
SEQ = 4096
DEPTH = 4

GRID_W = 64
CTX_LEN = 256
EPS = 1e-6

HY_W = 256
HY_ORDER = 2
HY_BANDS = 8
HY_EMB = 1 + 2 * HY_BANDS
HY_FF = 64
HY_DECAY_MIN = 3.0
HY_DECAY_MAX = 15.0
S5_W = 256
S5_CH = 16
S5_GROUPS = S5_W // S5_CH
S5_N = 64
ML_HEADS = 4
ML_HD = 128
ML_W = ML_HEADS * ML_HD
ML_CHUNK = 64
ML_GATES = 4 * ML_HEADS
SHORT_K = 3
MIX_W = HY_W + S5_W + ML_W
MLP_W = 4 * D_MODEL
PROJ_W = 3 * HY_W + S5_W + 3 * ML_W + ML_GATES
PROJ_SPLITS = (3 * HY_W,
               3 * HY_W + S5_W,
               3 * HY_W + S5_W + ML_W,
               3 * HY_W + S5_W + 2 * ML_W,
               3 * HY_W + S5_W + 3 * ML_W)

kernel_name = 'hybrid_hyena_s5_mlstm_dit'


def rms_norm(x, gain):
    xf = x.astype(jnp.float32)
    y = xf * lax.rsqrt(jnp.mean(xf * xf, axis=-1, keepdims=True) + EPS)
    return (y * gain.astype(jnp.float32)).astype(x.dtype)


def depthwise_conv1d(u, w):
    return lax.conv_general_dilated(
        u, w[:, None, :].astype(u.dtype), window_strides=(1,), padding='SAME',
        dimension_numbers=('NWC', 'WIO', 'NWC'), feature_group_count=u.shape[-1])


def depthwise_conv2d_grid(u, w, rows):
    bsz, length, ch = u.shape
    img = u.reshape(bsz, rows, GRID_W, ch)
    out = lax.conv_general_dilated(
        img, w[:, :, None, :].astype(u.dtype), window_strides=(1, 1), padding='SAME',
        dimension_numbers=('NHWC', 'HWIO', 'NHWC'), feature_group_count=ch)
    return out.reshape(bsz, length, ch)


def hyena_filters(length, w1, b1, w2, b2, w3, freq, decay):
    f32 = jnp.float32
    t = jnp.arange(length, dtype=f32) / length
    bands = jnp.arange(1, HY_BANDS + 1, dtype=f32)
    ang = 2.0 * math.pi * t[:, None] * bands
    feat = jnp.concatenate([t[:, None], jnp.cos(ang), jnp.sin(ang)], axis=-1)
    fr = freq.astype(f32)
    hdn = jnp.sin(fr * (feat @ w1.astype(f32) + b1.astype(f32)))
    hdn = jnp.sin(fr * (hdn @ w2.astype(f32) + b2.astype(f32)))
    filt = (hdn @ w3.astype(f32)).reshape(length, HY_ORDER, 2, HY_W)
    filt = filt * jnp.exp(-t[:, None, None, None] * decay.astype(f32))
    return filt * lax.rsqrt(jnp.sum(filt * filt, axis=(0, 2), keepdims=True) + EPS)


def long_conv(u, h_fwd, h_bwd):
    length = u.shape[1]
    taps = jnp.concatenate([h_fwd, jnp.zeros_like(h_fwd[:1]), h_bwd[:0:-1]], axis=0)
    spec = jnp.fft.rfft(u, n=2 * length, axis=1) * jnp.fft.rfft(taps, axis=0)[None]
    return jnp.fft.irfft(spec, n=2 * length, axis=1)[:, :length]


def hyena_mix(p_hy, conv_w, filt, bias):
    u = depthwise_conv1d(p_hy, conv_w).astype(jnp.float32)
    z, x1, x2 = jnp.split(u, 3, axis=-1)
    for n, gate in enumerate((x1, x2)):
        z = gate * (long_conv(z, filt[:, n, 0], filt[:, n, 1]) + bias[n].astype(jnp.float32) * z)
    return z


def s5_discretise(a_re, a_im, log_dt, b_re, b_im, c_re, c_im):
    f32 = jnp.float32
    lam = lax.complex(a_re.astype(f32), a_im.astype(f32))
    a_bar = jnp.exp(lam * jnp.exp(log_dt.astype(f32))[:, None])
    b_bar = ((a_bar - 1.0) / lam)[..., None] * lax.complex(b_re.astype(f32), b_im.astype(f32))
    c_mat = lax.complex(c_re.astype(f32), c_im.astype(f32))
    return a_bar, b_bar, c_mat


def _linear_combine(e1, e2):
    a1, b1 = e1
    a2, b2 = e2
    return a1 * a2, a2 * b1 + b2


def s5_scan(u, a_bar, b_bar, c_mat, h0):
    bu = jnp.einsum('gnc,blgc->blgn', b_bar, u)
    bu = bu.at[:, 0].add(a_bar * h0)
    a = jnp.broadcast_to(a_bar, bu.shape)
    _, states = lax.associative_scan(_linear_combine, (a, bu), axis=1)
    y = jnp.einsum('gcn,blgn->blgc', c_mat, states).real
    return y, states[:, -1]


def s5_mix(p_s5, disc, d_skip, w_glu, b_glu, h0):
    bsz, length, _ = p_s5.shape
    u = p_s5.astype(jnp.float32)
    uc = u.reshape(bsz, length, S5_GROUPS, S5_CH).astype(jnp.complex64)
    y_f, h_f = s5_scan(uc, *disc[0], h0[0])
    y_b, h_b = s5_scan(uc[:, ::-1], *disc[1], h0[1])
    y = (y_f + y_b[:, ::-1]).reshape(bsz, length, S5_W) + d_skip.astype(jnp.float32) * u
    g = jax.nn.gelu(y)
    return g * jax.nn.sigmoid(g @ w_glu.astype(jnp.float32) + b_glu.astype(jnp.float32)), (h_f, h_b)


def mlstm_chunkwise(q, k, v, log_i, log_f, state):
    bsz, nh, length, dh = q.shape
    nc = length // ML_CHUNK
    q, k, v = (t.reshape(bsz, nh, nc, ML_CHUNK, dh) for t in (q, k, v))
    log_i = log_i.reshape(bsz, nh, nc, ML_CHUNK)
    b = jnp.cumsum(log_f.reshape(bsz, nh, nc, ML_CHUNK), axis=-1)
    g = b[..., -1]
    a = g[..., None] - b + log_i
    m_loc = a.max(axis=-1)
    w = jnp.exp(a - m_loc[..., None])
    c_loc = jnp.einsum('bhcsd,bhcse->bhcde', v * w[..., None], k)
    n_loc = jnp.einsum('bhcs,bhcse->bhce', w, k)

    def step(carry, inp):
        c_st, n_st, m_st = carry
        g_j, m_j, c_j, n_j = inp
        m_new = jnp.maximum(g_j + m_st, m_j)
        dec = jnp.exp(g_j + m_st - m_new)
        grow = jnp.exp(m_j - m_new)
        c_new = dec[..., None, None] * c_st + grow[..., None, None] * c_j
        n_new = dec[..., None] * n_st + grow[..., None] * n_j
        return (c_new, n_new, m_new), (c_st, n_st, m_st)

    front = lambda t: jnp.moveaxis(t, 2, 0)
    final, prev = lax.scan(step, state, (front(g), front(m_loc), front(c_loc), front(n_loc)))
    c_prev, n_prev, m_prev = (jnp.moveaxis(t, 0, 2) for t in prev)

    lower = jnp.tril(jnp.ones((ML_CHUNK, ML_CHUNK), dtype=bool))
    dmat = jnp.where(lower, b[..., :, None] - b[..., None, :] + log_i[..., None, :], -jnp.inf)
    inter = b + m_prev[..., None]
    m_t = jnp.maximum(dmat.max(axis=-1), inter)
    s = jnp.einsum('bhctd,bhcsd->bhcts', q, k) * jnp.exp(dmat - m_t[..., None])
    w_inter = jnp.exp(inter - m_t)
    num = (jnp.einsum('bhcts,bhcsd->bhctd', s, v)
           + w_inter[..., None] * jnp.einsum('bhcde,bhcte->bhctd', c_prev, q))
    nq = s.sum(axis=-1) + w_inter * jnp.einsum('bhce,bhcte->bhct', n_prev, q)
    h = num / jnp.maximum(jnp.abs(nq), jnp.exp(-m_t))[..., None]
    return h.reshape(bsz, nh, length, dh), final


def mlstm_mix(p_x, p_v, p_o, p_g, conv_fn, wq, wk, gate_bias, skip, norm_gain, state):
    f32 = jnp.float32
    bsz, length, _ = p_x.shape
    xc = jax.nn.silu(conv_fn(p_x)).astype(f32)
    xh = xc.reshape(bsz, length, ML_HEADS, ML_HD)
    q = jnp.einsum('blhd,hde->bhle', xh, wq.astype(f32))
    k = jnp.einsum('blhd,hde->bhle', xh, wk.astype(f32)) * (ML_HD ** -0.5)
    v = p_v.astype(f32).reshape(bsz, length, ML_HEADS, ML_HD).transpose(0, 2, 1, 3)
    gates = (p_g.astype(f32).reshape(bsz, length, 4, ML_HEADS)
             + gate_bias.astype(f32)).transpose(2, 0, 3, 1)
    h_f, st_f = mlstm_chunkwise(q, k, v, gates[0], jax.nn.log_sigmoid(gates[1]), state[0])
    rev = lambda t: jnp.flip(t, axis=2)
    h_b, st_b = mlstm_chunkwise(rev(q), rev(k), rev(v), rev(gates[2]),
                                rev(jax.nn.log_sigmoid(gates[3])), state[1])
    h = (h_f + rev(h_b)).transpose(0, 2, 1, 3)
    h = h * lax.rsqrt(jnp.mean(h * h, axis=-1, keepdims=True) + EPS) * norm_gain.astype(f32).reshape(ML_HEADS, ML_HD)
    out = jax.nn.sigmoid(p_o.astype(f32)) * (h.reshape(bsz, length, ML_W) + skip.astype(f32) * xc)
    return out, (st_f, st_b)


def mix_sequence(p, conv_ml, filt, s5_h0, ml_state, hy_conv, hy_bias, s5_disc, s5_d, s5_w_glu,
                 s5_b_glu, ml_wq, ml_wk, ml_gate_bias, ml_skip, ml_norm_gain):
    p_hy, p_s5, p_mx, p_mv, p_mo, p_mg = jnp.split(p, PROJ_SPLITS, axis=-1)
    y_hy = hyena_mix(p_hy, hy_conv, filt, hy_bias)
    y_s5, s5_state = s5_mix(p_s5, s5_disc, s5_d, s5_w_glu, s5_b_glu, s5_h0)
    y_ml, ml_out_state = mlstm_mix(p_mx, p_mv, p_mo, p_mg, conv_ml, ml_wq, ml_wk, ml_gate_bias,
                                   ml_skip, ml_norm_gain, ml_state)
    y = jnp.concatenate([y_hy, y_s5, y_ml], axis=-1).astype(p.dtype)
    return y, s5_state, ml_out_state


def squared_relu_mlp(h, w1, w2):
    return jnp.square(jax.nn.relu(h @ w1)) @ w2


def setup_inputs(seed: int = 0) -> dict:
    key = jax.random.key(seed)
    keys = iter(jax.random.split(key, 64))

    def normal(shape, scale):
        return jax.random.normal(next(keys), shape, jnp.float32) * scale

    def near_one(shape):
        return 1.0 + normal(shape, 0.02)

    d = D_MODEL
    sd = (DEPTH, 2)
    f_bias = jnp.linspace(3.0, 6.0, ML_HEADS, dtype=jnp.float32)
    ml_gate_bias = jnp.stack([normal((DEPTH, ML_HEADS), 0.1),
                              f_bias + normal((DEPTH, ML_HEADS), 0.1),
                              normal((DEPTH, ML_HEADS), 0.1),
                              f_bias + normal((DEPTH, ML_HEADS), 0.1)], axis=1)
    decay = jnp.linspace(HY_DECAY_MIN, HY_DECAY_MAX, HY_W, dtype=jnp.float32)
    state_idx = jnp.arange(S5_N, dtype=jnp.float32)
    return {
        'x': normal((BATCH, SEQ, d), 1.0),
        'c': normal((BATCH, d), 1.0),
        'ctx': normal((BATCH, CTX_LEN, d), 1.0),
        'c_ctx': normal((d,), 1.0),
        'w_mod': normal((DEPTH, d, 6 * d), 0.5 * d ** -0.5),
        'b_mod': normal((DEPTH, 6 * d), 0.02),
        'g_pre_mix': near_one((DEPTH, d)),
        'g_post_mix': near_one((DEPTH, d)),
        'g_pre_mlp': near_one((DEPTH, d)),
        'g_post_mlp': near_one((DEPTH, d)),
        'w_in': normal((DEPTH, d, PROJ_W), d ** -0.5),
        'w_out': normal((DEPTH, MIX_W, d), MIX_W ** -0.5),
        'hy_conv': normal((DEPTH, SHORT_K, 3 * HY_W), SHORT_K ** -0.5),
        'hy_w1': normal((DEPTH, HY_EMB, HY_FF), HY_EMB ** -0.5),
        'hy_b1': normal((DEPTH, HY_FF), 0.1),
        'hy_w2': normal((DEPTH, HY_FF, HY_FF), HY_FF ** -0.5),
        'hy_b2': normal((DEPTH, HY_FF), 0.1),
        'hy_w3': normal((DEPTH, HY_FF, HY_ORDER * 2 * HY_W), HY_FF ** -0.5),
        'hy_freq': 1.0 + normal((DEPTH, HY_FF), 0.1),
        'hy_decay': decay + normal((DEPTH, HY_ORDER, 2, HY_W), 0.1),
        'hy_bias': normal((DEPTH, HY_ORDER, HY_W), 0.1),
        's5_a_re': -0.5 * jnp.exp(normal(sd + (S5_GROUPS, S5_N), 0.05)),
        's5_a_im': math.pi * state_idx + normal(sd + (S5_GROUPS, S5_N), 0.05),
        's5_log_dt': jax.random.uniform(next(keys), sd + (S5_GROUPS,), jnp.float32,
                                        math.log(1e-3), math.log(1e-1)),
        's5_b_re': normal(sd + (S5_GROUPS, S5_N, S5_CH), (2 * S5_CH) ** -0.5),
        's5_b_im': normal(sd + (S5_GROUPS, S5_N, S5_CH), (2 * S5_CH) ** -0.5),
        's5_c_re': normal(sd + (S5_GROUPS, S5_CH, S5_N), S5_N ** -0.5),
        's5_c_im': normal(sd + (S5_GROUPS, S5_CH, S5_N), S5_N ** -0.5),
        's5_d': normal((DEPTH, S5_W), 0.5),
        's5_w_glu': normal((DEPTH, S5_W, S5_W), S5_W ** -0.5),
        's5_b_glu': normal((DEPTH, S5_W), 0.02),
        'ml_conv': normal((DEPTH, SHORT_K, SHORT_K, ML_W), 1.0 / SHORT_K),
        'ml_wq': normal((DEPTH, ML_HEADS, ML_HD, ML_HD), ML_HD ** -0.5),
        'ml_wk': normal((DEPTH, ML_HEADS, ML_HD, ML_HD), ML_HD ** -0.5),
        'ml_gate_bias': ml_gate_bias,
        'ml_skip': near_one((DEPTH, ML_W)),
        'ml_norm_gain': near_one((DEPTH, ML_W)),
        'w_mlp1': normal((DEPTH, d, MLP_W), d ** -0.5),
        'w_mlp2': normal((DEPTH, MLP_W, d), MLP_W ** -0.5),
    }


def reference(x, c, ctx, c_ctx, w_mod, b_mod, g_pre_mix, g_post_mix, g_pre_mlp, g_post_mlp,
              w_in, w_out, hy_conv, hy_w1, hy_b1, hy_w2, hy_b2, hy_w3, hy_freq, hy_decay, hy_bias,
              s5_a_re, s5_a_im, s5_log_dt, s5_b_re, s5_b_im, s5_c_re, s5_c_im, s5_d, s5_w_glu,
              s5_b_glu, ml_conv, ml_wq, ml_wk, ml_gate_bias, ml_skip, ml_norm_gain, w_mlp1, w_mlp2):
    bsz, seq_len, _ = x.shape
    ctx_len = ctx.shape[1]
    rows = seq_len // GRID_W
    s5_zero = jnp.zeros((bsz, S5_GROUPS, S5_N), jnp.complex64)
    ml_zero = (jnp.zeros((bsz, ML_HEADS, ML_HD, ML_HD), jnp.float32),
               jnp.zeros((bsz, ML_HEADS, ML_HD), jnp.float32),
               jnp.zeros((bsz, ML_HEADS), jnp.float32))
    for l in range(DEPTH):
        mod_x = jnp.split(jax.nn.silu(c) @ w_mod[l] + b_mod[l], 6, axis=-1)
        sh1, sc1, gt1, sh2, sc2, gt2 = (m[:, None, :] for m in mod_x)
        csh1, csc1, cgt1, csh2, csc2, cgt2 = jnp.split(jax.nn.silu(c_ctx) @ w_mod[l] + b_mod[l], 6, axis=-1)

        filter_args = (hy_w1[l], hy_b1[l], hy_w2[l], hy_b2[l], hy_w3[l], hy_freq[l], hy_decay[l])
        s5_disc = tuple(s5_discretise(s5_a_re[l, dr], s5_a_im[l, dr], s5_log_dt[l, dr], s5_b_re[l, dr],
                                      s5_b_im[l, dr], s5_c_re[l, dr], s5_c_im[l, dr]) for dr in range(2))
        mix = functools.partial(mix_sequence, hy_conv=hy_conv[l], hy_bias=hy_bias[l], s5_disc=s5_disc,
                                s5_d=s5_d[l], s5_w_glu=s5_w_glu[l], s5_b_glu=s5_b_glu[l],
                                ml_wq=ml_wq[l], ml_wk=ml_wk[l], ml_gate_bias=ml_gate_bias[l],
                                ml_skip=ml_skip[l], ml_norm_gain=ml_norm_gain[l])

        hc = rms_norm(ctx, g_pre_mix[l]) * (1.0 + csc1) + csh1
        y_c, s5_state, ml_state = mix(
            hc @ w_in[l], conv_ml=functools.partial(depthwise_conv1d, w=ml_conv[l, 1]),
            filt=hyena_filters(ctx_len, *filter_args), s5_h0=(s5_zero, s5_zero),
            ml_state=(ml_zero, ml_zero))

        hx = rms_norm(x, g_pre_mix[l]) * (1.0 + sc1) + sh1
        y_x, _, _ = mix(
            hx @ w_in[l], conv_ml=functools.partial(depthwise_conv2d_grid, w=ml_conv[l], rows=rows),
            filt=hyena_filters(seq_len, *filter_args), s5_h0=s5_state, ml_state=ml_state)
        x = x + gt1 * rms_norm(y_x @ w_out[l], g_post_mix[l])
        hx = rms_norm(x, g_pre_mlp[l]) * (1.0 + sc2) + sh2
        x = x + gt2 * rms_norm(squared_relu_mlp(hx, w_mlp1[l], w_mlp2[l]), g_post_mlp[l])

        if l < DEPTH - 1:
            ctx = ctx + cgt1 * rms_norm(y_c @ w_out[l], g_post_mix[l])
            hc = rms_norm(ctx, g_pre_mlp[l]) * (1.0 + csc2) + csh2
            ctx = ctx + cgt2 * rms_norm(squared_relu_mlp(hc, w_mlp1[l], w_mlp2[l]), g_post_mlp[l])
    return x
```

```python
import functools
import math

import numpy as np
import jax
import jax.numpy as jnp
from jax import lax
from jax.experimental import pallas as pl
from jax.experimental.pallas import tpu as pltpu

F32 = jnp.float32
BF16 = jnp.bfloat16
EPS = 1e-6
HIGHEST = lax.Precision.HIGHEST

V7X_VMEM_BYTES = 64 * 1024 * 1024
VMEM_LIMIT = V7X_VMEM_BYTES - 8 * 1024 * 1024
LANES = 128
SUBLANES = 8
FFT_N2 = 64

HY_W = 256
HY_BANDS = 8
HY_EMB = 1 + 2 * HY_BANDS
HY_FF = 64
S5_W = 256
S5_CH = 16
S5_GROUPS = S5_W // S5_CH
S5_N = 64
S5_T = 8
ML_HEADS = 4
ML_HD = 128
ML_W = ML_HEADS * ML_HD
ML_CHUNK = 64
ML_GATES = 4 * ML_HEADS
GRID_W = 64
CTX_FFT_LEN = 1024


def _cparams(sem):
    return pltpu.CompilerParams(dimension_semantics=sem, vmem_limit_bytes=VMEM_LIMIT)


def _dot(a, b):
    return jnp.dot(a, b, preferred_element_type=F32)


def _dot_nt(a, b):
    return lax.dot_general(a, b, (((1,), (1,)), ((), ())), preferred_element_type=F32)


def _rms(x, g):
    return x * lax.rsqrt(jnp.mean(x * x, axis=-1, keepdims=True) + EPS) * g


def _silu(x):
    return x * jax.nn.sigmoid(x)


def _mod_kernel(c_ref, w_ref, b_ref, o_ref):
    s = _silu(c_ref[...]).astype(BF16)
    o_ref[0] = _dot(s, w_ref[0].astype(BF16)) + b_ref[0]


def mod_vectors(cc, w_mod, b_mod):
    depth, d, n = w_mod.shape
    r = cc.shape[0]
    tn = 1536
    return pl.pallas_call(
        _mod_kernel,
        grid=(depth, n // tn),
        in_specs=[pl.BlockSpec((r, d), lambda l, j: (0, 0)),
                  pl.BlockSpec((1, d, tn), lambda l, j: (l, 0, j)),
                  pl.BlockSpec((1, 1, tn), lambda l, j: (l, 0, j))],
        out_specs=pl.BlockSpec((1, r, tn), lambda l, j: (l, 0, j)),
        out_shape=jax.ShapeDtypeStruct((depth, r, n), F32),
        compiler_params=_cparams(("arbitrary", "arbitrary")),
        name="mod_vectors",
    )(cc, w_mod, b_mod.reshape(depth, 1, n))


def _row_tile(L):
    return min(L, 512)


def _inproj_kernel(x_ref, g_ref, sc_ref, sh_ref, w_ref, *o_refs, splits):
    h = _rms(x_ref[0], g_ref[...]) * (1.0 + sc_ref[0]) + sh_ref[0]
    hb = h.astype(BF16)
    for o_ref, (a, b) in zip(o_refs, splits):
        o_ref[0] = _dot(hb, w_ref[:, a:b])


def in_projection(x, g, sc, sh, w, splits):
    bn, L, d = x.shape
    tm = _row_tile(L)
    vec = pl.BlockSpec((1, 1, d), lambda b, i: (b, 0, 0))
    return pl.pallas_call(
        functools.partial(_inproj_kernel, splits=splits),
        grid=(bn, L // tm),
        in_specs=[pl.BlockSpec((1, tm, d), lambda b, i: (b, i, 0)),
                  pl.BlockSpec((1, d), lambda b, i: (0, 0)), vec, vec,
                  pl.BlockSpec(w.shape, lambda b, i: (0, 0))],
        out_specs=[pl.BlockSpec((1, tm, b_ - a_), lambda b, i: (b, i, 0)) for a_, b_ in splits],
        out_shape=[jax.ShapeDtypeStruct((bn, L, b_ - a_), F32) for a_, b_ in splits],
        compiler_params=_cparams(("arbitrary", "arbitrary")),
        name="in_projection",
    )(x, g.reshape(1, d), sc, sh, w)


def _outproj_kernel(yh_ref, ys_ref, ym_ref, w_ref, g_ref, gt_ref, x_ref, o_ref):
    a, b = yh_ref.shape[-1], yh_ref.shape[-1] + ys_ref.shape[-1]
    acc = _dot(yh_ref[0].astype(BF16), w_ref[0:a])
    acc += _dot(ys_ref[0].astype(BF16), w_ref[a:b])
    acc += _dot(ym_ref[0].astype(BF16), w_ref[b:])
    o_ref[0] = x_ref[0] + gt_ref[0] * _rms(acc, g_ref[...])


def out_projection(y_hy, y_s5, y_ml, w, g, gate, x):
    bn, L, d = x.shape
    tm = _row_tile(L)
    tok = lambda wd: pl.BlockSpec((1, tm, wd), lambda b, i: (b, i, 0))
    return pl.pallas_call(
        _outproj_kernel,
        grid=(bn, L // tm),
        in_specs=[tok(y_hy.shape[-1]), tok(y_s5.shape[-1]), tok(y_ml.shape[-1]),
                  pl.BlockSpec(w.shape, lambda b, i: (0, 0)),
                  pl.BlockSpec((1, d), lambda b, i: (0, 0)),
                  pl.BlockSpec((1, 1, d), lambda b, i: (b, 0, 0)), tok(d)],
        out_specs=tok(d),
        out_shape=jax.ShapeDtypeStruct((bn, L, d), F32),
        compiler_params=_cparams(("arbitrary", "arbitrary")),
        name="out_projection",
    )(y_hy, y_s5, y_ml, w, g.reshape(1, d), gate, x)


def _mlp_kernel(x_ref, g1_ref, sc_ref, sh_ref, w1_ref, w2_ref, g2_ref, gt_ref, o_ref, h_ref, acc_ref):
    j = pl.program_id(2)

    @pl.when(j == 0)
    def _():
        h = _rms(x_ref[0], g1_ref[...]) * (1.0 + sc_ref[0]) + sh_ref[0]
        h_ref[...] = h.astype(BF16)
        acc_ref[...] = jnp.zeros_like(acc_ref)

    a = jnp.maximum(_dot(h_ref[...], w1_ref[...]), 0.0)
    acc_ref[...] += _dot((a * a).astype(BF16), w2_ref[...])

    @pl.when(j == pl.num_programs(2) - 1)
    def _():
        o_ref[0] = x_ref[0] + gt_ref[0] * _rms(acc_ref[...], g2_ref[...])


def mlp_block(x, g_pre, sc, sh, w1, w2, g_post, gate):
    bn, L, d = x.shape
    hid = w1.shape[1]
    tm = _row_tile(L)
    th = 1024
    tok = pl.BlockSpec((1, tm, d), lambda b, i, j: (b, i, 0))
    vec = pl.BlockSpec((1, 1, d), lambda b, i, j: (b, 0, 0))
    gain = pl.BlockSpec((1, d), lambda b, i, j: (0, 0))
    return pl.pallas_call(
        _mlp_kernel,
        grid=(bn, L // tm, hid // th),
        in_specs=[tok, gain, vec, vec,
                  pl.BlockSpec((d, th), lambda b, i, j: (0, j)),
                  pl.BlockSpec((th, d), lambda b, i, j: (j, 0)),
                  gain, vec],
        out_specs=tok,
        out_shape=jax.ShapeDtypeStruct((bn, L, d), F32),
        scratch_shapes=[pltpu.VMEM((tm, d), BF16), pltpu.VMEM((tm, d), F32)],
        compiler_params=_cparams(("arbitrary", "arbitrary", "arbitrary")),
        name="mlp_block",
    )(x, g_pre.reshape(1, d), sc, sh, w1, w2, g_post.reshape(1, d), gate)


def _dwconv_kernel(x_ref, w_ref, o_ref, pad_ref, *, width, taps, act, pad):
    L = x_ref.shape[1]
    ch = x_ref.shape[2]
    pad_ref[0:pad, :] = jnp.zeros((pad, ch), F32)
    pad_ref[pad + L:pad + L + pad, :] = jnp.zeros((pad, ch), F32)
    pad_ref[pad:pad + L, :] = x_ref[0]
    tr = min(L, 256)
    for r0 in range(0, L, tr):
        col = (lax.broadcasted_iota(jnp.int32, (tr, ch), 0) + r0) & (width - 1)
        acc = jnp.zeros((tr, ch), F32)
        for dr, dc in taps:
            start = pad + r0 + dr * width + dc
            v = pad_ref[start:start + tr, :]
            if dc == -1:
                v = jnp.where(col >= 1, v, 0.0)
            elif dc == 1:
                v = jnp.where(col <= width - 2, v, 0.0)
            acc = acc + w_ref[(dr + 1) * 3 + (dc + 1):(dr + 1) * 3 + (dc + 2), :] * v
        o_ref[0, r0:r0 + tr, :] = act(acc)


def depthwise_conv(x, w9, width, rows, act=None):
    bn, L, ch = x.shape
    assert width & (width - 1) == 0
    taps = tuple((dr, dc) for dr in rows for dc in (-1, 0, 1))
    pad = -(-(width + 1) // SUBLANES) * SUBLANES if len(rows) > 1 else SUBLANES
    act = act or (lambda a: a)
    cs = LANES
    return pl.pallas_call(
        functools.partial(_dwconv_kernel, width=width, taps=taps, act=act, pad=pad),
        grid=(bn, ch // cs),
        in_specs=[pl.BlockSpec((1, L, cs), lambda b, c: (b, 0, c)),
                  pl.BlockSpec((9, cs), lambda b, c: (0, c))],
        out_specs=pl.BlockSpec((1, L, cs), lambda b, c: (b, 0, c)),
        out_shape=jax.ShapeDtypeStruct((bn, L, ch), F32),
        scratch_shapes=[pltpu.VMEM((L + 2 * pad, cs), F32)],
        compiler_params=_cparams(("arbitrary", "arbitrary")),
        name="depthwise_conv",
    )(x, w9)


def _hyfilt_kernel(w1_ref, b1_ref, w2_ref, b2_ref, w3_ref, fr_ref, dec_ref, o_ref, *, L):
    tr = min(L, 512)
    wd = o_ref.shape[1]
    half = wd // 2
    fr = fr_ref[...]
    ssq = jnp.zeros((1, wd), F32)
    for r0 in range(0, L, tr):
        t = (lax.broadcasted_iota(jnp.int32, (tr, 32), 0) + r0).astype(F32) / L
        lane = lax.broadcasted_iota(jnp.int32, (tr, 32), 1)
        band = jnp.where(lane <= HY_BANDS, lane, lane - HY_BANDS).astype(F32)
        ang = 2.0 * math.pi * t * band
        feat = jnp.where(lane == 0, t, jnp.where(lane <= HY_BANDS, jnp.cos(ang),
                                                 jnp.where(lane <= 2 * HY_BANDS, jnp.sin(ang), 0.0)))
        hdn = jnp.sin(fr * (jnp.dot(feat, w1_ref[...], precision=HIGHEST, preferred_element_type=F32) + b1_ref[...]))
        hdn = jnp.sin(fr * (jnp.dot(hdn, w2_ref[...], precision=HIGHEST, preferred_element_type=F32) + b2_ref[...]))
        filt = jnp.dot(hdn, w3_ref[...], precision=HIGHEST, preferred_element_type=F32)
        filt = filt * jnp.exp(-t[:, 0:1] * dec_ref[...])
        ssq = ssq + jnp.sum(filt * filt, axis=0, keepdims=True)
        o_ref[r0:r0 + tr, :] = filt
    tot = ssq[:, :half] + ssq[:, half:]
    scale = lax.rsqrt(tot + EPS)
    scale = jnp.concatenate([scale, scale], axis=1)
    for r0 in range(0, L, tr):
        o_ref[r0:r0 + tr, :] = o_ref[r0:r0 + tr, :] * scale


def hyena_filters(L, w1, b1, w2, b2, w3, freq, decay):
    order = decay.shape[0]
    wd = 2 * HY_W
    w1p = jnp.zeros((32, HY_FF), F32).at[:HY_EMB].set(w1)
    full = lambda a: pl.BlockSpec(a.shape, lambda o: (0,) * a.ndim)
    args = (w1p, b1.reshape(1, HY_FF), w2, b2.reshape(1, HY_FF))
    return pl.pallas_call(
        functools.partial(_hyfilt_kernel, L=L),
        grid=(order,),
        in_specs=[full(a) for a in args] + [pl.BlockSpec((HY_FF, wd), lambda o: (0, o)),
                                            pl.BlockSpec((1, HY_FF), lambda o: (0, 0)),
                                            pl.BlockSpec((1, wd), lambda o: (0, o))],
        out_specs=pl.BlockSpec((L, wd), lambda o: (0, o)),
        out_shape=jax.ShapeDtypeStruct((L, order * wd), F32),
        compiler_params=_cparams(("arbitrary",)),
        name="hyena_filters",
    )(*args, w3, freq.reshape(1, HY_FF), decay.reshape(1, order * wd))


@functools.lru_cache(maxsize=None)
def _dft_consts(L):
    n = 2 * L
    n2c = FFT_N2
    n1c = n // n2c
    n1h = n1c // 2
    n1 = np.arange(n1h)[None, None, :]
    k1 = np.arange(n1c)[None, :, None]
    n2 = np.arange(n2c)[:, None, None]
    ang = -2.0 * np.pi * (k1 * n1 / n1c + n2 * k1 / n)
    fr, fi = np.cos(ang), np.sin(ang)
    fa = np.concatenate([np.concatenate([fr, -fi], axis=2),
                         np.concatenate([fi, fr], axis=2)], axis=1)
    frt = np.transpose(fr, (0, 2, 1)) / n
    fit = -np.transpose(fi, (0, 2, 1)) / n
    fai = np.concatenate([np.concatenate([frt, -fit], axis=2),
                          np.concatenate([fit, frt], axis=2)], axis=1)
    k2 = np.arange(n2c)[:, None]
    m2 = np.arange(n2c)[None, :]
    angb = -2.0 * np.pi * k2 * m2 / n2c
    gr, gi = np.cos(angb), np.sin(angb)
    gb = np.block([[gr, -gi], [gi, gr]])
    gbi = np.block([[gr, gi], [-gi, gr]])
    return (fa.astype(np.float32), gb.astype(np.float32), gbi.astype(np.float32), fai.astype(np.float32))


def _fft_stage_a(load_rhs, fa_ref, s_ref, n1c):
    def body(n2, carry):
        rhs = load_rhs(n2).astype(BF16)
        res = _dot(fa_ref[n2], rhs)
        s_ref[pl.ds(pl.multiple_of(n2 * 2 * n1c, 2 * n1c), 2 * n1c), :] = res
        return carry
    lax.fori_loop(0, FFT_N2, body, 0)


def _fft_load_k1(s_ref, k1, n1c):
    xr = s_ref[pl.ds(k1, FFT_N2, stride=2 * n1c), :]
    xi = s_ref[pl.ds(n1c + k1, FFT_N2, stride=2 * n1c), :]
    return jnp.concatenate([xr, xi], axis=0)


def _fftconv_kernel(z_ref, gate_ref, bias_ref, h_ref, fa_ref, gb_ref, gbi_ref, fai_ref, o_ref, s_ref, *, L):
    n1c = L // 32
    n1h = n1c // 2
    half = FFT_N2

    def load_rhs(n2):
        za = z_ref[0, 0, pl.ds(n2, n1h, stride=FFT_N2), :]
        zb = z_ref[0, 1, pl.ds(n2, n1h, stride=FFT_N2), :]
        return jnp.concatenate([za, zb], axis=0)
    _fft_stage_a(load_rhs, fa_ref, s_ref, n1c)

    def stage_b(k1, carry):
        x = _dot(gb_ref[...], _fft_load_k1(s_ref, k1, n1c).astype(BF16))
        xr, xi = x[:half], x[half:]
        h = h_ref[k1]
        hr, hi = h[:half], h[half:]
        y = jnp.concatenate([xr * hr - xi * hi, xr * hi + xi * hr], axis=0)
        bp = _dot(gbi_ref[...], y.astype(BF16))
        s_ref[pl.ds(k1, FFT_N2, stride=2 * n1c), :] = bp[:half]
        s_ref[pl.ds(n1c + k1, FFT_N2, stride=2 * n1c), :] = bp[half:]
        return carry
    lax.fori_loop(0, n1c, stage_b, 0)

    bias = bias_ref[...]

    def stage_a_inv(n2, carry):
        rhs = s_ref[pl.ds(pl.multiple_of(n2 * 2 * n1c, 2 * n1c), 2 * n1c), :].astype(BF16)
        res = _dot(fai_ref[n2], rhs)
        for p in range(2):
            rows = pl.ds(n2, n1h, stride=FFT_N2)
            zin = z_ref[0, p, rows, :]
            o_ref[0, p, rows, :] = gate_ref[0, p, rows, :] * (res[p * n1h:(p + 1) * n1h] + bias * zin)
        return carry
    lax.fori_loop(0, FFT_N2, stage_a_inv, 0)


def fft_gated_conv(z, zoff, gate, goff, bias, hspec):
    bsz, L, _ = z.shape
    ch = bias.shape[0]
    n1c = L // 32
    fa, gb, gbi, fai = (jnp.asarray(a, BF16) for a in _dft_consts(L))
    zp = z.reshape(bsz // 2, 2, L, z.shape[-1])
    gp = gate.reshape(bsz // 2, 2, L, gate.shape[-1])
    cs = LANES
    one = pl.Buffered(1)
    const3 = lambda a: pl.BlockSpec(a.shape, lambda c, p: (0, 0, 0), pipeline_mode=one)
    const2 = lambda a: pl.BlockSpec(a.shape, lambda c, p: (0, 0), pipeline_mode=one)
    out = pl.pallas_call(
        functools.partial(_fftconv_kernel, L=L),
        grid=(ch // cs, bsz // 2),
        in_specs=[pl.BlockSpec((1, 2, L, cs), lambda c, p: (p, 0, 0, c + zoff)),
                  pl.BlockSpec((1, 2, L, cs), lambda c, p: (p, 0, 0, c + goff)),
                  pl.BlockSpec((1, cs), lambda c, p: (0, c)),
                  pl.BlockSpec((n1c, 2 * FFT_N2, cs), lambda c, p: (0, 0, c), pipeline_mode=one),
                  const3(fa), const2(gb), const2(gbi), const3(fai)],
        out_specs=pl.BlockSpec((1, 2, L, cs), lambda c, p: (p, 0, 0, c)),
        out_shape=jax.ShapeDtypeStruct((bsz // 2, 2, L, ch), F32),
        scratch_shapes=[pltpu.VMEM((FFT_N2 * 2 * n1c, cs), F32)],
        compiler_params=_cparams(("arbitrary", "arbitrary")),
        name="fft_gated_conv",
    )(zp, gp, bias.reshape(1, ch), hspec, fa, gb, gbi, fai)
    return out.reshape(bsz, L, ch)


def _fftspec_kernel(hf_ref, hb_ref, fa_ref, gb_ref, o_ref, s_ref, *, L):
    n1c = L // 32
    n1h = n1c // 2
    half = FFT_N2
    for d, src in enumerate((hf_ref, hb_ref)):
        def load_rhs(n2, src=src, d=d):
            h = src[pl.ds(n2, n1h, stride=FFT_N2), :]
            if d == 1:
                row = lax.broadcasted_iota(jnp.int32, h.shape, 0)
                h = jnp.where((row == 0) & (n2 == 0), 0.0, h)
            return jnp.concatenate([h, jnp.zeros_like(h)], axis=0)
        _fft_stage_a(load_rhs, fa_ref, s_ref, n1c)

        def stage_b(k1, carry, d=d):
            x = _dot(gb_ref[...], _fft_load_k1(s_ref, k1, n1c).astype(BF16))
            if d == 0:
                o_ref[k1] = x
            else:
                o_ref[k1] = o_ref[k1] + jnp.concatenate([x[:half], -x[half:]], axis=0)
            return carry
        lax.fori_loop(0, n1c, stage_b, 0)


def fft_filter_spectrum(filt, foff, boff, L):
    if filt.shape[0] < L:
        filt = jnp.pad(filt, ((0, L - filt.shape[0]), (0, 0)))
    n1c = L // 32
    fa, gb, _, _ = (jnp.asarray(a, BF16) for a in _dft_consts(L))
    cs = LANES
    return pl.pallas_call(
        functools.partial(_fftspec_kernel, L=L),
        grid=(HY_W // cs,),
        in_specs=[pl.BlockSpec((L, cs), lambda c: (0, c + foff)), pl.BlockSpec((L, cs), lambda c: (0, c + boff)),
                  pl.BlockSpec(fa.shape, lambda c: (0, 0, 0)), pl.BlockSpec(gb.shape, lambda c: (0, 0))],
        out_specs=pl.BlockSpec((n1c, 2 * FFT_N2, cs), lambda c: (0, 0, c)),
        out_shape=jax.ShapeDtypeStruct((n1c, 2 * FFT_N2, HY_W), F32),
        scratch_shapes=[pltpu.VMEM((FFT_N2 * 2 * n1c, cs), F32)],
        compiler_params=_cparams(("arbitrary",)),
        name="fft_filter_spectrum",
    )(filt, filt, fa, gb)


def hyena_mix(p_hy, conv_w, filt, bias, fft_len):
    bsz, L, _ = p_hy.shape
    w9 = jnp.zeros((9, 3 * HY_W), F32).at[3:6].set(conv_w)
    u = depthwise_conv(p_hy, w9, width=L, rows=(0,))
    if fft_len > L:
        u = jnp.pad(u, ((0, 0), (0, fft_len - L), (0, 0)))
    nb = HY_W // LANES
    z = fft_gated_conv(u, 0, u, nb, bias[0], fft_filter_spectrum(filt, 0, nb, fft_len))
    z = fft_gated_conv(z, 0, u, 2 * nb, bias[1], fft_filter_spectrum(filt, 2 * nb, 3 * nb, fft_len))
    return z[:, :L]


def s5_matrices(a_re, a_im, log_dt, b_re, b_im, c_re, c_im):
    t = S5_T
    lam = lax.complex(a_re, a_im)
    a_bar = jnp.exp(lam * jnp.exp(log_dt)[..., None])
    b_bar = ((a_bar - 1.0) / lam)[..., None] * lax.complex(b_re, b_im)
    c_mat = lax.complex(c_re, c_im)
    pw = jnp.stack([a_bar ** j for j in range(t + 1)], axis=1)
    kern = jnp.einsum('xgcn,xjgn,xgnd->xjgcd', c_mat, pw[:, :t], b_bar).real
    s = np.arange(t)[:, None]
    tt = np.arange(t)[None, :]
    sel_f = np.stack([(tt - s == j) for j in range(t)]).astype(np.float32)
    sel_b = np.stack([(s - tt == j) for j in range(t)]).astype(np.float32)
    eye = jnp.eye(S5_GROUPS, dtype=F32)
    m6 = (jnp.einsum('jst,jgcd,gh->sgdthc', sel_f, kern[0], eye)
          + jnp.einsum('jst,jgcd,gh->sgdthc', sel_b, kern[1], eye))
    m_intra = m6.reshape(t * S5_W, t * S5_W)
    pf = pw[0, :t][::-1]
    pb = pw[1, :t]
    rf = pf[..., None] * b_bar[0][None]
    rb = pb[..., None] * b_bar[1][None]

    def in_mat(r):
        both = jnp.stack([r.real, r.imag], axis=0)
        return jnp.einsum('rsgnd,gh->sgdrhn', both, eye).reshape(t * S5_W, 2 * S5_GROUPS * S5_N)
    r_in = jnp.concatenate([in_mat(rf), in_mat(rb)], axis=1)
    cf = c_mat[0][None] * jnp.moveaxis(pw[0, 1:], -1, -1)[:, :, None, :]
    cb = c_mat[1][None] * pw[1, 1:][::-1][:, :, None, :]

    def out_mat(cm):
        both = jnp.stack([cm.real, -cm.imag], axis=0)
        return jnp.einsum('rtgcn,gh->rgnthc', both, eye).reshape(2 * S5_GROUPS * S5_N, t * S5_W)
    r_out = jnp.concatenate([out_mat(cf), out_mat(cb)], axis=0)
    a8 = pw[:, t].reshape(2, S5_GROUPS * S5_N)
    a8v = jnp.concatenate([a8[0].real, a8[0].imag, a8[1].real, a8[1].imag])[None, :]
    return m_intra.astype(BF16), r_in.astype(BF16), r_out.astype(BF16), a8v


def _matmul_kernel(x_ref, w_ref, o_ref):
    o_ref[0] = _dot(x_ref[0].astype(BF16), w_ref[...])


def rows_matmul(x, w, tn):
    bn, r, k = x.shape
    n = w.shape[1]
    tm = min(r, 512)
    return pl.pallas_call(
        _matmul_kernel,
        grid=(n // tn, bn, r // tm),
        in_specs=[pl.BlockSpec((1, tm, k), lambda j, b, i: (b, i, 0)),
                  pl.BlockSpec((k, tn), lambda j, b, i: (0, j))],
        out_specs=pl.BlockSpec((1, tm, tn), lambda j, b, i: (b, i, j)),
        out_shape=jax.ShapeDtypeStruct((bn, r, n), F32),
        compiler_params=_cparams(("arbitrary", "arbitrary", "arbitrary")),
        name="rows_matmul",
    )(x, w)


def _matmul2_kernel(x_ref, w_ref, y_ref, v_ref, o_ref):
    o_ref[0] = _dot(x_ref[0].astype(BF16), w_ref[...]) + _dot(y_ref[0].astype(BF16), v_ref[...])


def rows_matmul2(x, w, y, v, tn):
    bn, r, k = x.shape
    k2 = y.shape[-1]
    n = w.shape[1]
    tm = min(r, 512)
    return pl.pallas_call(
        _matmul2_kernel,
        grid=(n // tn, bn, r // tm),
        in_specs=[pl.BlockSpec((1, tm, k), lambda j, b, i: (b, i, 0)),
                  pl.BlockSpec((k, tn), lambda j, b, i: (0, j)),
                  pl.BlockSpec((1, tm, k2), lambda j, b, i: (b, i, 0)),
                  pl.BlockSpec((k2, tn), lambda j, b, i: (0, j))],
        out_specs=pl.BlockSpec((1, tm, tn), lambda j, b, i: (b, i, j)),
        out_shape=jax.ShapeDtypeStruct((bn, r, n), F32),
        compiler_params=_cparams(("arbitrary", "arbitrary", "arbitrary")),
        name="rows_matmul2",
    )(x, w, y, v)


def _s5scan_kernel(r_ref, a_ref, h0_ref, x_ref, fin_ref):
    nk = r_ref.shape[1]
    sw = S5_GROUPS * S5_N
    ar_f, ai_f = a_ref[:, 0:sw], a_ref[:, sw:2 * sw]
    ar_b, ai_b = a_ref[:, 2 * sw:3 * sw], a_ref[:, 3 * sw:4 * sw]
    init = tuple(h0_ref[0, :, j * sw:(j + 1) * sw] for j in range(4))

    def body(i, carry):
        fr, fi, br, bi = carry
        k = nk - 1 - i
        x_ref[0, pl.ds(i, 1), 0:sw] = fr
        x_ref[0, pl.ds(i, 1), sw:2 * sw] = fi
        x_ref[0, pl.ds(k, 1), 2 * sw:3 * sw] = br
        x_ref[0, pl.ds(k, 1), 3 * sw:4 * sw] = bi
        rfr = r_ref[0, pl.ds(i, 1), 0:sw]
        rfi = r_ref[0, pl.ds(i, 1), sw:2 * sw]
        rbr = r_ref[0, pl.ds(k, 1), 2 * sw:3 * sw]
        rbi = r_ref[0, pl.ds(k, 1), 3 * sw:4 * sw]
        return (ar_f * fr - ai_f * fi + rfr, ar_f * fi + ai_f * fr + rfi,
                ar_b * br - ai_b * bi + rbr, ar_b * bi + ai_b * br + rbi)
    fin = lax.fori_loop(0, nk, body, init)
    for j in range(4):
        fin_ref[0, :, j * sw:(j + 1) * sw] = fin[j]


def s5_row_scan(r, a8v, h0):
    bn, nk, w = r.shape
    return pl.pallas_call(
        _s5scan_kernel,
        grid=(bn,),
        in_specs=[pl.BlockSpec((1, nk, w), lambda b: (b, 0, 0)),
                  pl.BlockSpec((1, w), lambda b: (0, 0)),
                  pl.BlockSpec((1, 1, w), lambda b: (b, 0, 0))],
        out_specs=[pl.BlockSpec((1, nk, w), lambda b: (b, 0, 0)),
                   pl.BlockSpec((1, 1, w), lambda b: (b, 0, 0))],
        out_shape=[jax.ShapeDtypeStruct((bn, nk, w), F32), jax.ShapeDtypeStruct((bn, 1, w), F32)],
        compiler_params=_cparams(("arbitrary",)),
        name="s5_row_scan",
    )(r, a8v, h0)


def _s5glu_kernel(y_ref, u_ref, d_ref, w_ref, b_ref, o_ref):
    g = jax.nn.gelu(y_ref[0] + d_ref[...] * u_ref[0], approximate=True)
    o_ref[0] = g * jax.nn.sigmoid(_dot(g.astype(BF16), w_ref[...]) + b_ref[...])


def s5_glu(y, u, d_skip, w_glu, b_glu):
    bn, L, w = y.shape
    tm = _row_tile(L)
    tok = pl.BlockSpec((1, tm, w), lambda b, i: (b, i, 0))
    row = pl.BlockSpec((1, w), lambda b, i: (0, 0))
    return pl.pallas_call(
        _s5glu_kernel,
        grid=(bn, L // tm),
        in_specs=[tok, tok, row, pl.BlockSpec((w, w), lambda b, i: (0, 0)), row],
        out_specs=tok,
        out_shape=jax.ShapeDtypeStruct((bn, L, w), F32),
        compiler_params=_cparams(("arbitrary", "arbitrary")),
        name="s5_glu",
    )(y, u, d_skip.reshape(1, w), w_glu, b_glu.reshape(1, w))


def s5_mix(p_s5, mats, d_skip, w_glu, b_glu, h0):
    m_intra, r_in, r_out, a8v = mats
    bn, L, w = p_s5.shape
    u8 = p_s5.reshape(bn, L // S5_T, S5_T * w)
    r = rows_matmul(u8, r_in, tn=1024)
    xs, fin = s5_row_scan(r, a8v, h0)
    y8 = rows_matmul2(u8, m_intra, xs, r_out, tn=512)
    return s5_glu(y8.reshape(bn, L, w), p_s5, d_skip, w_glu, b_glu), fin


def _qk_kernel(x_ref, wq_ref, wk_ref, q_ref, k_ref):
    for h in range(ML_HEADS):
        sl = slice(h * ML_HD, (h + 1) * ML_HD)
        xb = x_ref[0, :, sl].astype(BF16)
        q_ref[0, :, sl] = _dot(xb, wq_ref[h]).astype(BF16)
        k_ref[0, :, sl] = (_dot(xb, wk_ref[h]) * (ML_HD ** -0.5)).astype(BF16)


def mlstm_qk(xc, wq, wk):
    bn, L, w = xc.shape
    tm = _row_tile(L)
    tok = pl.BlockSpec((1, tm, w), lambda b, i: (b, i, 0))
    wsp = pl.BlockSpec(wq.shape, lambda b, i: (0, 0, 0))
    return pl.pallas_call(
        _qk_kernel,
        grid=(bn, L // tm),
        in_specs=[tok, wsp, wsp],
        out_specs=[tok, tok],
        out_shape=[jax.ShapeDtypeStruct((bn, L, w), BF16)] * 2,
        compiler_params=_cparams(("arbitrary", "arbitrary")),
        name="mlstm_qk",
    )(xc, wq, wk)


def _mlstm_kernel(bias_ref, q_ref, k_ref, v_ref, g_ref, c0_ref, n0_ref, m0_ref,
                  h_ref, cf_ref, nf_ref, mf_ref, c_s, n_s, m_s):
    d = pl.program_id(2)
    hd = pl.program_id(1)
    nc = g_ref.shape[2]
    t = ML_CHUNK
    c_s[...] = c0_ref[0, 0, 0]
    n_s[...] = n0_ref[0, 0, 0]
    m_s[...] = m0_ref[0, 0, 0]
    row = lax.broadcasted_iota(jnp.int32, (t, t), 0)
    col = lax.broadcasted_iota(jnp.int32, (t, t), 1)
    fwd = d == 0
    sgn = 1 - 2 * d
    tri = (row - col) * sgn >= 0
    cum = ((col - row) * sgn >= 0).astype(F32)
    bias_i = bias_ref[d * 2 * ML_HEADS + hd]
    bias_f = bias_ref[d * 2 * ML_HEADS + ML_HEADS + hd]

    def body(j, carry):
        c = jnp.where(fwd, j, nc - 1 - j)
        r0 = pl.multiple_of(c * t, t)
        q = q_ref[0, pl.ds(r0, t), :]
        k = k_ref[0, pl.ds(r0, t), :]
        v = v_ref[0, pl.ds(r0, t), :]
        li = g_ref[0, d * 2 * ML_HEADS + hd, pl.ds(c, 1), :] + bias_i
        lf = jax.nn.log_sigmoid(g_ref[0, d * 2 * ML_HEADS + ML_HEADS + hd, pl.ds(c, 1), :] + bias_f)
        b_row = jnp.dot(jnp.broadcast_to(lf, (SUBLANES, t)), cum, precision=HIGHEST,
                        preferred_element_type=F32)[0:1]
        b_col = jnp.sum(jnp.where(tri, lf, 0.0), axis=-1, keepdims=True)
        g = jnp.sum(lf, axis=-1, keepdims=True)
        a_row = g - b_row + li
        m_loc = jnp.max(a_row, axis=-1, keepdims=True)
        w_row = jnp.exp(a_row - m_loc)
        vt = v.T
        c_loc = _dot((vt * w_row).astype(BF16), k)
        n_loc = _dot(jnp.broadcast_to(w_row, (SUBLANES, t)).astype(BF16), k)[0:1]
        c_prev, n_prev, m_prev = c_s[...], n_s[...], m_s[:, 0:1]
        dmat = jnp.where(tri, b_col - b_row + li, -jnp.inf)
        inter = b_col + m_prev
        m_t = jnp.maximum(jnp.max(dmat, axis=-1, keepdims=True), inter)
        s = _dot_nt(q, k) * jnp.exp(dmat - m_t)
        w_inter = jnp.exp(inter - m_t)
        num = _dot(s.astype(BF16), v.astype(BF16)) + w_inter * _dot_nt(q, c_prev.astype(BF16))
        nq = jnp.sum(s, axis=-1, keepdims=True) + w_inter * jnp.sum(q.astype(F32) * n_prev, axis=-1, keepdims=True)
        h_ref[0, 0, pl.ds(r0, t), :] = num / jnp.maximum(jnp.abs(nq), jnp.exp(-m_t))
        m_new = jnp.maximum(g + m_prev, m_loc)
        dec = jnp.exp(g + m_prev - m_new)
        grow = jnp.exp(m_loc - m_new)
        c_s[...] = dec * c_prev + grow * c_loc
        n_s[...] = dec * n_prev + grow * n_loc
        m_s[...] = jnp.broadcast_to(m_new, m_s.shape)
        return carry
    lax.fori_loop(0, nc, body, 0)
    cf_ref[0, 0, 0] = c_s[...]
    nf_ref[0, 0, 0] = n_s[...]
    mf_ref[0, 0, 0] = m_s[...]


def mlstm_scan(q, k, v, gates_t, gate_bias, state):
    bn, L, w = q.shape
    c0, n0, m0 = state
    hsp = pl.BlockSpec((1, L, ML_HD), lambda b, h, d: (b, 0, h))
    st = lambda a: pl.BlockSpec((1, 1, 1) + a.shape[3:], lambda b, h, d: (b, d, h, 0, 0))
    return pl.pallas_call(
        _mlstm_kernel,
        grid=(bn, ML_HEADS, 2),
        in_specs=[pl.BlockSpec(memory_space=pltpu.SMEM), hsp, hsp, hsp,
                  pl.BlockSpec((1,) + gates_t.shape[1:], lambda b, h, d: (b, 0, 0, 0)),
                  st(c0), st(n0), st(m0)],
        out_specs=[pl.BlockSpec((1, 1, L, ML_HD), lambda b, h, d: (d, b, 0, h)), st(c0), st(n0), st(m0)],
        out_shape=[jax.ShapeDtypeStruct((2, bn, L, w), F32),
                   jax.ShapeDtypeStruct(c0.shape, F32), jax.ShapeDtypeStruct(n0.shape, F32),
                   jax.ShapeDtypeStruct(m0.shape, F32)],
        scratch_shapes=[pltpu.VMEM((ML_HD, ML_HD), F32), pltpu.VMEM((1, ML_HD), F32), pltpu.VMEM((1, LANES), F32)],
        compiler_params=_cparams(("arbitrary", "arbitrary", "arbitrary")),
        name="mlstm_scan",
    )(gate_bias, q, k, v, gates_t, c0, n0, m0)


def _mlout_kernel(hf_ref, hb_ref, o_ref, xc_ref, gain_ref, skip_ref, y_ref):
    h = hf_ref[0, 0] + hb_ref[0, 0]
    parts = []
    for hd in range(ML_HEADS):
        sl = slice(hd * ML_HD, (hd + 1) * ML_HD)
        hh = h[:, sl]
        parts.append(hh * lax.rsqrt(jnp.mean(hh * hh, axis=-1, keepdims=True) + EPS) * gain_ref[:, sl])
    hn = jnp.concatenate(parts, axis=-1)
    y_ref[0] = jax.nn.sigmoid(o_ref[0]) * (hn + skip_ref[...] * xc_ref[0])


def mlstm_output(h2, p_o, xc, gain, skip):
    _, bn, L, w = h2.shape
    tm = _row_tile(L)
    tok = pl.BlockSpec((1, tm, w), lambda b, i: (b, i, 0))
    row = pl.BlockSpec((1, w), lambda b, i: (0, 0))
    return pl.pallas_call(
        _mlout_kernel,
        grid=(bn, L // tm),
        in_specs=[pl.BlockSpec((1, 1, tm, w), lambda b, i: (0, b, i, 0)),
                  pl.BlockSpec((1, 1, tm, w), lambda b, i: (1, b, i, 0)), tok, tok, row, row],
        out_specs=tok,
        out_shape=jax.ShapeDtypeStruct((bn, L, w), F32),
        compiler_params=_cparams(("arbitrary", "arbitrary")),
        name="mlstm_output",
    )(h2, h2, p_o, xc, gain.reshape(1, w), skip.reshape(1, w))


def mlstm_mix(p_x, p_v, p_o, p_g, conv_w9, width, rows, wq, wk, gate_bias, skip, gain, state):
    bn, L, _ = p_x.shape
    xc = depthwise_conv(p_x, conv_w9, width=width, rows=rows, act=_silu)
    q, k = mlstm_qk(xc, wq, wk)
    gates_t = jnp.swapaxes(p_g[..., :ML_GATES], 1, 2).reshape(bn, ML_GATES, L // ML_CHUNK, ML_CHUNK)
    h2, cf, nf, mf = mlstm_scan(q, k, p_v, gates_t, gate_bias.reshape(ML_GATES), state)
    return mlstm_output(h2, p_o, xc, gain, skip), (cf, nf, mf)


PROJ_SPLITS = ((0, 3 * HY_W), (3 * HY_W, 3 * HY_W + S5_W),
               (3 * HY_W + S5_W, 3 * HY_W + S5_W + ML_W),
               (3 * HY_W + S5_W + ML_W, 3 * HY_W + S5_W + 2 * ML_W),
               (3 * HY_W + S5_W + 2 * ML_W, 3 * HY_W + S5_W + 3 * ML_W),
               (3 * HY_W + S5_W + 3 * ML_W, 3 * HY_W + S5_W + 3 * ML_W + LANES))


def kernel(x, c, ctx, c_ctx, w_mod, b_mod, g_pre_mix, g_post_mix, g_pre_mlp, g_post_mlp, w_in, w_out, hy_conv, hy_w1, hy_b1, hy_w2, hy_b2, hy_w3, hy_freq, hy_decay, hy_bias, s5_a_re, s5_a_im, s5_log_dt, s5_b_re, s5_b_im, s5_c_re, s5_c_im, s5_d, s5_w_glu, s5_b_glu, ml_conv, ml_wq, ml_wk, ml_gate_bias, ml_skip, ml_norm_gain, w_mlp1, w_mlp2):
    bsz, seq_len, d = x.shape
    ctx_len = ctx.shape[1]
    depth = w_mod.shape[0]

    cc = jnp.zeros((SUBLANES, d), F32).at[:bsz].set(c).at[bsz].set(c_ctx)
    mods = mod_vectors(cc, w_mod, b_mod)

    proj_w = w_in.shape[-1]
    w_in_b = jnp.pad(w_in, ((0, 0), (0, 0), (0, PROJ_SPLITS[-1][1] - proj_w))).astype(BF16)
    w_out_b, w1_b, w2_b = w_out.astype(BF16), w_mlp1.astype(BF16), w_mlp2.astype(BF16)
    wq_b, wk_b, wglu_b = ml_wq.astype(BF16), ml_wk.astype(BF16), s5_w_glu.astype(BF16)

    s5_zero = jnp.zeros((bsz, 1, 4 * S5_GROUPS * S5_N), F32)
    ml_zero = (jnp.zeros((bsz, 2, ML_HEADS, ML_HD, ML_HD), F32),
               jnp.zeros((bsz, 2, ML_HEADS, 1, ML_HD), F32),
               jnp.zeros((bsz, 2, ML_HEADS, 1, LANES), F32))

    for l in range(depth):
        mx = [mods[l, :bsz, j * d:(j + 1) * d][:, None, :] for j in range(6)]
        mc = [jnp.broadcast_to(mods[l, bsz, j * d:(j + 1) * d][None, None, :], (bsz, 1, d)) for j in range(6)]
        filt_x = hyena_filters(seq_len, hy_w1[l], hy_b1[l], hy_w2[l], hy_b2[l], hy_w3[l], hy_freq[l], hy_decay[l])
        filt_c = hyena_filters(ctx_len, hy_w1[l], hy_b1[l], hy_w2[l], hy_b2[l], hy_w3[l], hy_freq[l], hy_decay[l])
        s5_mats = s5_matrices(s5_a_re[l], s5_a_im[l], s5_log_dt[l], s5_b_re[l], s5_b_im[l], s5_c_re[l], s5_c_im[l])
        conv2d_w = ml_conv[l].reshape(9, ML_W)

        def mix(h_in, mod, L, fft_len, filt, width, rows, s5_h0, ml_state):
            p_hy, p_s5, p_mx, p_mv, p_mo, p_mg = in_projection(h_in, g_pre_mix[l], mod[1], mod[0], w_in_b[l], PROJ_SPLITS)
            y_hy = hyena_mix(p_hy, hy_conv[l], filt, hy_bias[l], fft_len)
            y_s5, s5_fin = s5_mix(p_s5, s5_mats, s5_d[l], wglu_b[l], s5_b_glu[l], s5_h0)
            y_ml, ml_fin = mlstm_mix(p_mx, p_mv, p_mo, p_mg, conv2d_w, width, rows, wq_b[l], wk_b[l],
                                     ml_gate_bias[l], ml_skip[l], ml_norm_gain[l], ml_state)
            return (y_hy, y_s5, y_ml), s5_fin, ml_fin

        y_c, s5_state, ml_state = mix(ctx, mc, ctx_len, CTX_FFT_LEN, filt_c, ctx_len, (0,), s5_zero, ml_zero)
        y_x, _, _ = mix(x, mx, seq_len, seq_len, filt_x, GRID_W, (-1, 0, 1), s5_state, ml_state)
        x = out_projection(*y_x, w_out_b[l], g_post_mix[l], mx[2], x)
        x = mlp_block(x, g_pre_mlp[l], mx[4], mx[3], w1_b[l], w2_b[l], g_post_mlp[l], mx[5])
        if l < depth - 1:
            ctx = out_projection(*y_c, w_out_b[l], g_post_mix[l], mc[2], ctx)
            ctx = mlp_block(ctx, g_pre_mlp[l], mc[4], mc[3], w1_b[l], w2_b[l], g_post_mlp[l], mc[5])
    return x
```

```python
import functools
import math

import numpy as np
import jax
import jax.numpy as jnp
from jax import lax
from jax.experimental import pallas as pl
from jax.experimental.pallas import tpu as pltpu

F32 = jnp.float32
BF16 = jnp.bfloat16
EPS = 1e-6
HIGHEST = lax.Precision.HIGHEST

V7X_VMEM_BYTES = 64 * 1024 * 1024
VMEM_LIMIT = V7X_VMEM_BYTES - 8 * 1024 * 1024
LANES = 128
SUBLANES = 8
FFT_N2 = 64
FFT_UNROLL_A = 4
FFT_UNROLL_B = 8

HY_W = 256
HY_BANDS = 8
HY_EMB = 1 + 2 * HY_BANDS
HY_FF = 64
S5_W = 256
S5_CH = 16
S5_GROUPS = S5_W // S5_CH
S5_N = 64
S5_T = 8
ML_HEADS = 4
ML_HD = 128
ML_W = ML_HEADS * ML_HD
ML_CHUNK = 64
ML_GATES = 4 * ML_HEADS
GRID_W = 64
CTX_FFT_LEN = 1024


def _cparams(sem):
    return pltpu.CompilerParams(dimension_semantics=sem, vmem_limit_bytes=VMEM_LIMIT)


def _dot(a, b):
    return jnp.dot(a, b, preferred_element_type=F32)


def _dot_nt(a, b):
    return lax.dot_general(a, b, (((1,), (1,)), ((), ())), preferred_element_type=F32)


def _rms(x, g):
    return x * lax.rsqrt(jnp.mean(x * x, axis=-1, keepdims=True) + EPS) * g


def _silu(x):
    return x * jax.nn.sigmoid(x)


def _mod_kernel(c_ref, w_ref, b_ref, o_ref):
    s = _silu(c_ref[...]).astype(BF16)
    o_ref[0] = _dot(s, w_ref[0].astype(BF16)) + b_ref[0]


def mod_vectors(cc, w_mod, b_mod):
    depth, d, n = w_mod.shape
    r = cc.shape[0]
    tn = 1536
    return pl.pallas_call(
        _mod_kernel,
        grid=(depth, n // tn),
        in_specs=[pl.BlockSpec((r, d), lambda l, j: (0, 0)),
                  pl.BlockSpec((1, d, tn), lambda l, j: (l, 0, j)),
                  pl.BlockSpec((1, 1, tn), lambda l, j: (l, 0, j))],
        out_specs=pl.BlockSpec((1, r, tn), lambda l, j: (l, 0, j)),
        out_shape=jax.ShapeDtypeStruct((depth, r, n), F32),
        compiler_params=_cparams(("arbitrary", "arbitrary")),
        name="mod_vectors",
    )(cc, w_mod, b_mod.reshape(depth, 1, n))


def _row_tile(L):
    return min(L, 512)


def _inproj_kernel(x_ref, g_ref, sc_ref, sh_ref, w_ref, *o_refs, splits):
    h = _rms(x_ref[0], g_ref[...]) * (1.0 + sc_ref[0]) + sh_ref[0]
    hb = h.astype(BF16)
    for o_ref, (a, b) in zip(o_refs, splits):
        o_ref[0] = _dot(hb, w_ref[:, a:b]).astype(o_ref.dtype)


def in_projection(x, g, sc, sh, w, splits, dtypes):
    bn, L, d = x.shape
    tm = _row_tile(L)
    vec = pl.BlockSpec((1, 1, d), lambda b, i: (b, 0, 0))
    return pl.pallas_call(
        functools.partial(_inproj_kernel, splits=splits),
        grid=(bn, L // tm),
        in_specs=[pl.BlockSpec((1, tm, d), lambda b, i: (b, i, 0)),
                  pl.BlockSpec((1, d), lambda b, i: (0, 0)), vec, vec,
                  pl.BlockSpec(w.shape, lambda b, i: (0, 0))],
        out_specs=[pl.BlockSpec((1, tm, b_ - a_), lambda b, i: (b, i, 0)) for a_, b_ in splits],
        out_shape=[jax.ShapeDtypeStruct((bn, L, b_ - a_), dt) for (a_, b_), dt in zip(splits, dtypes)],
        compiler_params=_cparams(("arbitrary", "arbitrary")),
        name="in_projection",
    )(x, g.reshape(1, d), sc, sh, w)


def _outproj_kernel(yh_ref, ys_ref, ym_ref, w_ref, g_ref, gt_ref, x_ref, o_ref):
    a, b = yh_ref.shape[-1], yh_ref.shape[-1] + ys_ref.shape[-1]
    acc = _dot(yh_ref[0].astype(BF16), w_ref[0:a])
    acc += _dot(ys_ref[0].astype(BF16), w_ref[a:b])
    acc += _dot(ym_ref[0].astype(BF16), w_ref[b:])
    o_ref[0] = x_ref[0] + gt_ref[0] * _rms(acc, g_ref[...])


def out_projection(y_hy, y_s5, y_ml, w, g, gate, x):
    bn, L, d = x.shape
    tm = _row_tile(L)
    tok = lambda wd: pl.BlockSpec((1, tm, wd), lambda b, i: (b, i, 0))
    return pl.pallas_call(
        _outproj_kernel,
        grid=(bn, L // tm),
        in_specs=[tok(y_hy.shape[-1]), tok(y_s5.shape[-1]), tok(y_ml.shape[-1]),
                  pl.BlockSpec(w.shape, lambda b, i: (0, 0)),
                  pl.BlockSpec((1, d), lambda b, i: (0, 0)),
                  pl.BlockSpec((1, 1, d), lambda b, i: (b, 0, 0)), tok(d)],
        out_specs=tok(d),
        out_shape=jax.ShapeDtypeStruct((bn, L, d), F32),
        compiler_params=_cparams(("arbitrary", "arbitrary")),
        name="out_projection",
    )(y_hy, y_s5, y_ml, w, g.reshape(1, d), gate, x)


def _mlp_kernel(x_ref, g1_ref, sc_ref, sh_ref, w1_ref, w2_ref, g2_ref, gt_ref, o_ref, h_ref, acc_ref):
    j = pl.program_id(2)

    @pl.when(j == 0)
    def _():
        h = _rms(x_ref[0], g1_ref[...]) * (1.0 + sc_ref[0]) + sh_ref[0]
        h_ref[...] = h.astype(BF16)
        acc_ref[...] = jnp.zeros_like(acc_ref)

    a = jnp.maximum(_dot(h_ref[...], w1_ref[...]), 0.0)
    acc_ref[...] += _dot((a * a).astype(BF16), w2_ref[...])

    @pl.when(j == pl.num_programs(2) - 1)
    def _():
        o_ref[0] = x_ref[0] + gt_ref[0] * _rms(acc_ref[...], g2_ref[...])


def mlp_block(x, g_pre, sc, sh, w1, w2, g_post, gate):
    bn, L, d = x.shape
    hid = w1.shape[1]
    tm = _row_tile(L)
    th = 1024
    tok = pl.BlockSpec((1, tm, d), lambda b, i, j: (b, i, 0))
    vec = pl.BlockSpec((1, 1, d), lambda b, i, j: (b, 0, 0))
    gain = pl.BlockSpec((1, d), lambda b, i, j: (0, 0))
    return pl.pallas_call(
        _mlp_kernel,
        grid=(bn, L // tm, hid // th),
        in_specs=[tok, gain, vec, vec,
                  pl.BlockSpec((d, th), lambda b, i, j: (0, j)),
                  pl.BlockSpec((th, d), lambda b, i, j: (j, 0)),
                  gain, vec],
        out_specs=tok,
        out_shape=jax.ShapeDtypeStruct((bn, L, d), F32),
        scratch_shapes=[pltpu.VMEM((tm, d), BF16), pltpu.VMEM((tm, d), F32)],
        compiler_params=_cparams(("arbitrary", "arbitrary", "arbitrary")),
        name="mlp_block",
    )(x, g_pre.reshape(1, d), sc, sh, w1, w2, g_post.reshape(1, d), gate)


def _dwconv_kernel(x_ref, w_ref, o_ref, pad_ref, *, width, taps, act, pad):
    L = x_ref.shape[1]
    ch = x_ref.shape[2]
    pad_ref[0:pad, :] = jnp.zeros((pad, ch), F32)
    pad_ref[pad + L:pad + L + pad, :] = jnp.zeros((pad, ch), F32)
    pad_ref[pad:pad + L, :] = x_ref[0]
    tr = min(L, 256)
    for r0 in range(0, L, tr):
        col = (lax.broadcasted_iota(jnp.int32, (tr, ch), 0) + r0) & (width - 1)
        acc = jnp.zeros((tr, ch), F32)
        for dr, dc in taps:
            start = pad + r0 + dr * width + dc
            v = pad_ref[start:start + tr, :]
            if dc == -1:
                v = jnp.where(col >= 1, v, 0.0)
            elif dc == 1:
                v = jnp.where(col <= width - 2, v, 0.0)
            acc = acc + w_ref[(dr + 1) * 3 + (dc + 1):(dr + 1) * 3 + (dc + 2), :] * v
        o_ref[0, r0:r0 + tr, :] = act(acc)


def depthwise_conv(x, w9, width, rows, act=None):
    bn, L, ch = x.shape
    assert width & (width - 1) == 0
    taps = tuple((dr, dc) for dr in rows for dc in (-1, 0, 1))
    pad = -(-(width + 1) // SUBLANES) * SUBLANES if len(rows) > 1 else SUBLANES
    act = act or (lambda a: a)
    cs = LANES
    return pl.pallas_call(
        functools.partial(_dwconv_kernel, width=width, taps=taps, act=act, pad=pad),
        grid=(bn, ch // cs),
        in_specs=[pl.BlockSpec((1, L, cs), lambda b, c: (b, 0, c)),
                  pl.BlockSpec((9, cs), lambda b, c: (0, c))],
        out_specs=pl.BlockSpec((1, L, cs), lambda b, c: (b, 0, c)),
        out_shape=jax.ShapeDtypeStruct((bn, L, ch), F32),
        scratch_shapes=[pltpu.VMEM((L + 2 * pad, cs), F32)],
        compiler_params=_cparams(("arbitrary", "arbitrary")),
        name="depthwise_conv",
    )(x, w9)


def _hyfilt_kernel(w1_ref, b1_ref, w2_ref, b2_ref, w3_ref, fr_ref, dec_ref, o_ref, *, L):
    tr = min(L, 512)
    wd = o_ref.shape[1]
    half = wd // 2
    fr = fr_ref[...]
    ssq = jnp.zeros((1, wd), F32)
    for r0 in range(0, L, tr):
        t = (lax.broadcasted_iota(jnp.int32, (tr, 32), 0) + r0).astype(F32) / L
        lane = lax.broadcasted_iota(jnp.int32, (tr, 32), 1)
        band = jnp.where(lane <= HY_BANDS, lane, lane - HY_BANDS).astype(F32)
        ang = 2.0 * math.pi * t * band
        feat = jnp.where(lane == 0, t, jnp.where(lane <= HY_BANDS, jnp.cos(ang),
                                                 jnp.where(lane <= 2 * HY_BANDS, jnp.sin(ang), 0.0)))
        hdn = jnp.sin(fr * (jnp.dot(feat, w1_ref[...], precision=HIGHEST, preferred_element_type=F32) + b1_ref[...]))
        hdn = jnp.sin(fr * (jnp.dot(hdn, w2_ref[...], precision=HIGHEST, preferred_element_type=F32) + b2_ref[...]))
        filt = jnp.dot(hdn, w3_ref[...], precision=HIGHEST, preferred_element_type=F32)
        filt = filt * jnp.exp(-t[:, 0:1] * dec_ref[...])
        ssq = ssq + jnp.sum(filt * filt, axis=0, keepdims=True)
        o_ref[r0:r0 + tr, :] = filt
    tot = ssq[:, :half] + ssq[:, half:]
    scale = lax.rsqrt(tot + EPS)
    scale = jnp.concatenate([scale, scale], axis=1)
    for r0 in range(0, L, tr):
        o_ref[r0:r0 + tr, :] = o_ref[r0:r0 + tr, :] * scale


def hyena_filters(L, w1, b1, w2, b2, w3, freq, decay):
    order = decay.shape[0]
    wd = 2 * HY_W
    w1p = jnp.zeros((32, HY_FF), F32).at[:HY_EMB].set(w1)
    full = lambda a: pl.BlockSpec(a.shape, lambda o: (0,) * a.ndim)
    args = (w1p, b1.reshape(1, HY_FF), w2, b2.reshape(1, HY_FF))
    return pl.pallas_call(
        functools.partial(_hyfilt_kernel, L=L),
        grid=(order,),
        in_specs=[full(a) for a in args] + [pl.BlockSpec((HY_FF, wd), lambda o: (0, o)),
                                            pl.BlockSpec((1, HY_FF), lambda o: (0, 0)),
                                            pl.BlockSpec((1, wd), lambda o: (0, o))],
        out_specs=pl.BlockSpec((L, wd), lambda o: (0, o)),
        out_shape=jax.ShapeDtypeStruct((L, order * wd), F32),
        compiler_params=_cparams(("arbitrary",)),
        name="hyena_filters",
    )(*args, w3, freq.reshape(1, HY_FF), decay.reshape(1, order * wd))


@functools.lru_cache(maxsize=None)
def _dft_consts(L):
    n = 2 * L
    n2c = FFT_N2
    n1c = n // n2c
    n1h = n1c // 2
    n1 = np.arange(n1h)[None, None, :]
    k1 = np.arange(n1c)[None, :, None]
    n2 = np.arange(n2c)[:, None, None]
    ang = -2.0 * np.pi * (k1 * n1 / n1c + n2 * k1 / n)
    fr, fi = np.cos(ang), np.sin(ang)
    fa = np.concatenate([np.concatenate([fr, -fi], axis=2),
                         np.concatenate([fi, fr], axis=2)], axis=1)
    frt = np.transpose(fr, (0, 2, 1)) / n
    fit = -np.transpose(fi, (0, 2, 1)) / n
    fai = np.concatenate([np.concatenate([frt, -fit], axis=2),
                          np.concatenate([fit, frt], axis=2)], axis=1)
    k2 = np.arange(n2c)[:, None]
    m2 = np.arange(n2c)[None, :]
    angb = -2.0 * np.pi * k2 * m2 / n2c
    gr, gi = np.cos(angb), np.sin(angb)
    gb = np.block([[gr, -gi], [gi, gr]])
    gbi = np.block([[gr, gi], [-gi, gr]])
    return (fa.astype(np.float32), gb.astype(np.float32), gbi.astype(np.float32), fai.astype(np.float32))


def _fft_stage_a(load_rhs, fa_ref, s_ref, n1c):
    def body(n2, carry):
        rhs = load_rhs(n2).astype(BF16)
        res = _dot(fa_ref[n2], rhs)
        s_ref[pl.ds(pl.multiple_of(n2 * 2 * n1c, 2 * n1c), 2 * n1c), :] = res
        return carry
    lax.fori_loop(0, FFT_N2, body, 0, unroll=FFT_UNROLL_A)


def _fft_load_k1(s_ref, k1, n1c):
    xr = s_ref[pl.ds(k1, FFT_N2, stride=2 * n1c), :]
    xi = s_ref[pl.ds(n1c + k1, FFT_N2, stride=2 * n1c), :]
    return jnp.concatenate([xr, xi], axis=0)


def _fftconv_kernel(z_ref, gate_ref, bias_ref, h_ref, fa_ref, gb_ref, gbi_ref, fai_ref, o_ref, s_ref, *, L):
    n1c = L // 32
    n1h = n1c // 2
    half = FFT_N2

    def load_rhs(n2):
        za = z_ref[0, 0, pl.ds(n2, n1h, stride=FFT_N2), :]
        zb = z_ref[0, 1, pl.ds(n2, n1h, stride=FFT_N2), :]
        return jnp.concatenate([za, zb], axis=0)
    _fft_stage_a(load_rhs, fa_ref, s_ref, n1c)

    def stage_b(k1, carry):
        x = _dot(gb_ref[...], _fft_load_k1(s_ref, k1, n1c).astype(BF16))
        xr, xi = x[:half], x[half:]
        h = h_ref[k1]
        hr, hi = h[:half], h[half:]
        y = jnp.concatenate([xr * hr - xi * hi, xr * hi + xi * hr], axis=0)
        bp = _dot(gbi_ref[...], y.astype(BF16))
        s_ref[pl.ds(k1, FFT_N2, stride=2 * n1c), :] = bp[:half]
        s_ref[pl.ds(n1c + k1, FFT_N2, stride=2 * n1c), :] = bp[half:]
        return carry
    lax.fori_loop(0, n1c, stage_b, 0, unroll=FFT_UNROLL_B)

    bias = bias_ref[...]

    def stage_a_inv(n2, carry):
        rhs = s_ref[pl.ds(pl.multiple_of(n2 * 2 * n1c, 2 * n1c), 2 * n1c), :].astype(BF16)
        res = _dot(fai_ref[n2], rhs)
        for p in range(2):
            rows = pl.ds(n2, n1h, stride=FFT_N2)
            zin = z_ref[0, p, rows, :]
            o_ref[0, p, rows, :] = gate_ref[0, p, rows, :] * (res[p * n1h:(p + 1) * n1h] + bias * zin)
        return carry
    lax.fori_loop(0, FFT_N2, stage_a_inv, 0, unroll=FFT_UNROLL_A)


def fft_gated_conv(z, zoff, gate, goff, bias, hspec):
    bsz, L, _ = z.shape
    ch = bias.shape[0]
    n1c = L // 32
    fa, gb, gbi, fai = (jnp.asarray(a, BF16) for a in _dft_consts(L))
    zp = z.reshape(bsz // 2, 2, L, z.shape[-1])
    gp = gate.reshape(bsz // 2, 2, L, gate.shape[-1])
    cs = LANES
    one = pl.Buffered(1)
    const3 = lambda a: pl.BlockSpec(a.shape, lambda c, p: (0, 0, 0), pipeline_mode=one)
    const2 = lambda a: pl.BlockSpec(a.shape, lambda c, p: (0, 0), pipeline_mode=one)
    out = pl.pallas_call(
        functools.partial(_fftconv_kernel, L=L),
        grid=(ch // cs, bsz // 2),
        in_specs=[pl.BlockSpec((1, 2, L, cs), lambda c, p: (p, 0, 0, c + zoff)),
                  pl.BlockSpec((1, 2, L, cs), lambda c, p: (p, 0, 0, c + goff)),
                  pl.BlockSpec((1, cs), lambda c, p: (0, c)),
                  pl.BlockSpec((n1c, 2 * FFT_N2, cs), lambda c, p: (0, 0, c), pipeline_mode=one),
                  const3(fa), const2(gb), const2(gbi), const3(fai)],
        out_specs=pl.BlockSpec((1, 2, L, cs), lambda c, p: (p, 0, 0, c)),
        out_shape=jax.ShapeDtypeStruct((bsz // 2, 2, L, ch), F32),
        scratch_shapes=[pltpu.VMEM((FFT_N2 * 2 * n1c, cs), F32)],
        compiler_params=_cparams(("arbitrary", "arbitrary")),
        name="fft_gated_conv",
    )(zp, gp, bias.reshape(1, ch), hspec, fa, gb, gbi, fai)
    return out.reshape(bsz, L, ch)


def _fftspec_kernel(hf_ref, hb_ref, fa_ref, gb_ref, o_ref, s_ref, *, L):
    n1c = L // 32
    n1h = n1c // 2
    half = FFT_N2
    for d, src in enumerate((hf_ref, hb_ref)):
        def load_rhs(n2, src=src, d=d):
            h = src[pl.ds(n2, n1h, stride=FFT_N2), :]
            if d == 1:
                row = lax.broadcasted_iota(jnp.int32, h.shape, 0)
                h = jnp.where((row == 0) & (n2 == 0), 0.0, h)
            return jnp.concatenate([h, jnp.zeros_like(h)], axis=0)
        _fft_stage_a(load_rhs, fa_ref, s_ref, n1c)

        def stage_b(k1, carry, d=d):
            x = _dot(gb_ref[...], _fft_load_k1(s_ref, k1, n1c).astype(BF16))
            if d == 0:
                o_ref[k1] = x
            else:
                o_ref[k1] = o_ref[k1] + jnp.concatenate([x[:half], -x[half:]], axis=0)
            return carry
        lax.fori_loop(0, n1c, stage_b, 0, unroll=FFT_UNROLL_B)


def fft_filter_spectrum(filt, foff, boff, L):
    if filt.shape[0] < L:
        filt = jnp.pad(filt, ((0, L - filt.shape[0]), (0, 0)))
    n1c = L // 32
    fa, gb, _, _ = (jnp.asarray(a, BF16) for a in _dft_consts(L))
    cs = LANES
    return pl.pallas_call(
        functools.partial(_fftspec_kernel, L=L),
        grid=(HY_W // cs,),
        in_specs=[pl.BlockSpec((L, cs), lambda c: (0, c + foff)), pl.BlockSpec((L, cs), lambda c: (0, c + boff)),
                  pl.BlockSpec(fa.shape, lambda c: (0, 0, 0)), pl.BlockSpec(gb.shape, lambda c: (0, 0))],
        out_specs=pl.BlockSpec((n1c, 2 * FFT_N2, cs), lambda c: (0, 0, c)),
        out_shape=jax.ShapeDtypeStruct((n1c, 2 * FFT_N2, HY_W), F32),
        scratch_shapes=[pltpu.VMEM((FFT_N2 * 2 * n1c, cs), F32)],
        compiler_params=_cparams(("arbitrary",)),
        name="fft_filter_spectrum",
    )(filt, filt, fa, gb)


def hyena_mix(p_hy, conv_w, filt, bias, fft_len):
    bsz, L, _ = p_hy.shape
    w9 = jnp.zeros((9, 3 * HY_W), F32).at[3:6].set(conv_w)
    u = depthwise_conv(p_hy, w9, width=L, rows=(0,))
    if fft_len > L:
        u = jnp.pad(u, ((0, 0), (0, fft_len - L), (0, 0)))
    nb = HY_W // LANES
    z = fft_gated_conv(u, 0, u, nb, bias[0], fft_filter_spectrum(filt, 0, nb, fft_len))
    z = fft_gated_conv(z, 0, u, 2 * nb, bias[1], fft_filter_spectrum(filt, 2 * nb, 3 * nb, fft_len))
    return z[:, :L]


def s5_matrices(a_re, a_im, log_dt, b_re, b_im, c_re, c_im):
    t = S5_T
    lam = lax.complex(a_re, a_im)
    a_bar = jnp.exp(lam * jnp.exp(log_dt)[..., None])
    b_bar = ((a_bar - 1.0) / lam)[..., None] * lax.complex(b_re, b_im)
    c_mat = lax.complex(c_re, c_im)
    pw = jnp.stack([a_bar ** j for j in range(t + 1)], axis=1)
    kern = jnp.einsum('xgcn,xjgn,xgnd->xjgcd', c_mat, pw[:, :t], b_bar).real
    s = np.arange(t)[:, None]
    tt = np.arange(t)[None, :]
    sel_f = np.stack([(tt - s == j) for j in range(t)]).astype(np.float32)
    sel_b = np.stack([(s - tt == j) for j in range(t)]).astype(np.float32)
    eye = jnp.eye(S5_GROUPS, dtype=F32)
    m6 = (jnp.einsum('jst,jgcd,gh->sgdthc', sel_f, kern[0], eye)
          + jnp.einsum('jst,jgcd,gh->sgdthc', sel_b, kern[1], eye))
    m_intra = m6.reshape(t * S5_W, t * S5_W)
    pf = pw[0, :t][::-1]
    pb = pw[1, :t]
    rf = pf[..., None] * b_bar[0][None]
    rb = pb[..., None] * b_bar[1][None]

    def in_mat(r):
        both = jnp.stack([r.real, r.imag], axis=0)
        return jnp.einsum('rsgnd,gh->sgdrhn', both, eye).reshape(t * S5_W, 2 * S5_GROUPS * S5_N)
    r_in = jnp.concatenate([in_mat(rf), in_mat(rb)], axis=1)
    cf = c_mat[0][None] * jnp.moveaxis(pw[0, 1:], -1, -1)[:, :, None, :]
    cb = c_mat[1][None] * pw[1, 1:][::-1][:, :, None, :]

    def out_mat(cm):
        both = jnp.stack([cm.real, -cm.imag], axis=0)
        return jnp.einsum('rtgcn,gh->rgnthc', both, eye).reshape(2 * S5_GROUPS * S5_N, t * S5_W)
    r_out = jnp.concatenate([out_mat(cf), out_mat(cb)], axis=0)
    a8 = pw[:, t].reshape(2, S5_GROUPS * S5_N)
    a8v = jnp.concatenate([a8[0].real, a8[0].imag, a8[1].real, a8[1].imag])[None, :]
    return m_intra.astype(BF16), r_in.astype(BF16), r_out.astype(BF16), a8v


def _matmul_kernel(x_ref, w_ref, o_ref):
    o_ref[0] = _dot(x_ref[0].astype(BF16), w_ref[...])


def rows_matmul(x, w, tn):
    bn, r, k = x.shape
    n = w.shape[1]
    tm = min(r, 512)
    return pl.pallas_call(
        _matmul_kernel,
        grid=(n // tn, bn, r // tm),
        in_specs=[pl.BlockSpec((1, tm, k), lambda j, b, i: (b, i, 0)),
                  pl.BlockSpec((k, tn), lambda j, b, i: (0, j))],
        out_specs=pl.BlockSpec((1, tm, tn), lambda j, b, i: (b, i, j)),
        out_shape=jax.ShapeDtypeStruct((bn, r, n), F32),
        compiler_params=_cparams(("arbitrary", "arbitrary", "arbitrary")),
        name="rows_matmul",
    )(x, w)


def _matmul2_kernel(x_ref, w_ref, y_ref, v_ref, o_ref):
    o_ref[0] = _dot(x_ref[0].astype(BF16), w_ref[...]) + _dot(y_ref[0].astype(BF16), v_ref[...])


def rows_matmul2(x, w, y, v, tn):
    bn, r, k = x.shape
    k2 = y.shape[-1]
    n = w.shape[1]
    tm = min(r, 512)
    return pl.pallas_call(
        _matmul2_kernel,
        grid=(n // tn, bn, r // tm),
        in_specs=[pl.BlockSpec((1, tm, k), lambda j, b, i: (b, i, 0)),
                  pl.BlockSpec((k, tn), lambda j, b, i: (0, j)),
                  pl.BlockSpec((1, tm, k2), lambda j, b, i: (b, i, 0)),
                  pl.BlockSpec((k2, tn), lambda j, b, i: (0, j))],
        out_specs=pl.BlockSpec((1, tm, tn), lambda j, b, i: (b, i, j)),
        out_shape=jax.ShapeDtypeStruct((bn, r, n), F32),
        compiler_params=_cparams(("arbitrary", "arbitrary", "arbitrary")),
        name="rows_matmul2",
    )(x, w, y, v)


def _s5scan_kernel(r_ref, a_ref, h0_ref, x_ref, fin_ref):
    nk = r_ref.shape[1]
    sw = S5_GROUPS * S5_N
    ar_f, ai_f = a_ref[:, 0:sw], a_ref[:, sw:2 * sw]
    ar_b, ai_b = a_ref[:, 2 * sw:3 * sw], a_ref[:, 3 * sw:4 * sw]
    init = tuple(h0_ref[0, :, j * sw:(j + 1) * sw] for j in range(4))

    def body(i, carry):
        fr, fi, br, bi = carry
        k = nk - 1 - i
        x_ref[0, pl.ds(i, 1), 0:sw] = fr
        x_ref[0, pl.ds(i, 1), sw:2 * sw] = fi
        x_ref[0, pl.ds(k, 1), 2 * sw:3 * sw] = br
        x_ref[0, pl.ds(k, 1), 3 * sw:4 * sw] = bi
        rfr = r_ref[0, pl.ds(i, 1), 0:sw]
        rfi = r_ref[0, pl.ds(i, 1), sw:2 * sw]
        rbr = r_ref[0, pl.ds(k, 1), 2 * sw:3 * sw]
        rbi = r_ref[0, pl.ds(k, 1), 3 * sw:4 * sw]
        return (ar_f * fr - ai_f * fi + rfr, ar_f * fi + ai_f * fr + rfi,
                ar_b * br - ai_b * bi + rbr, ar_b * bi + ai_b * br + rbi)
    fin = lax.fori_loop(0, nk, body, init)
    for j in range(4):
        fin_ref[0, :, j * sw:(j + 1) * sw] = fin[j]


def s5_row_scan(r, a8v, h0):
    bn, nk, w = r.shape
    return pl.pallas_call(
        _s5scan_kernel,
        grid=(bn,),
        in_specs=[pl.BlockSpec((1, nk, w), lambda b: (b, 0, 0)),
                  pl.BlockSpec((1, w), lambda b: (0, 0)),
                  pl.BlockSpec((1, 1, w), lambda b: (b, 0, 0))],
        out_specs=[pl.BlockSpec((1, nk, w), lambda b: (b, 0, 0)),
                   pl.BlockSpec((1, 1, w), lambda b: (b, 0, 0))],
        out_shape=[jax.ShapeDtypeStruct((bn, nk, w), F32), jax.ShapeDtypeStruct((bn, 1, w), F32)],
        compiler_params=_cparams(("arbitrary",)),
        name="s5_row_scan",
    )(r, a8v, h0)


def _s5glu_kernel(y_ref, u_ref, d_ref, w_ref, b_ref, o_ref):
    g = jax.nn.gelu(y_ref[0] + d_ref[...] * u_ref[0], approximate=True)
    o_ref[0] = g * jax.nn.sigmoid(_dot(g.astype(BF16), w_ref[...]) + b_ref[...])


def s5_glu(y, u, d_skip, w_glu, b_glu):
    bn, L, w = y.shape
    tm = _row_tile(L)
    tok = pl.BlockSpec((1, tm, w), lambda b, i: (b, i, 0))
    row = pl.BlockSpec((1, w), lambda b, i: (0, 0))
    return pl.pallas_call(
        _s5glu_kernel,
        grid=(bn, L // tm),
        in_specs=[tok, tok, row, pl.BlockSpec((w, w), lambda b, i: (0, 0)), row],
        out_specs=tok,
        out_shape=jax.ShapeDtypeStruct((bn, L, w), F32),
        compiler_params=_cparams(("arbitrary", "arbitrary")),
        name="s5_glu",
    )(y, u, d_skip.reshape(1, w), w_glu, b_glu.reshape(1, w))


def s5_mix(p_s5, mats, d_skip, w_glu, b_glu, h0):
    m_intra, r_in, r_out, a8v = mats
    bn, L, w = p_s5.shape
    u8 = p_s5.reshape(bn, L // S5_T, S5_T * w)
    r = rows_matmul(u8, r_in, tn=1024)
    xs, fin = s5_row_scan(r, a8v, h0)
    y8 = rows_matmul2(u8, m_intra, xs, r_out, tn=512)
    return s5_glu(y8.reshape(bn, L, w), p_s5, d_skip, w_glu, b_glu), fin


def _qk_kernel(x_ref, wq_ref, wk_ref, q_ref, k_ref):
    for h in range(ML_HEADS):
        sl = slice(h * ML_HD, (h + 1) * ML_HD)
        xb = x_ref[0, :, sl].astype(BF16)
        q_ref[0, :, sl] = _dot(xb, wq_ref[h]).astype(BF16)
        k_ref[0, :, sl] = (_dot(xb, wk_ref[h]) * (ML_HD ** -0.5)).astype(BF16)


def mlstm_qk(xc, wq, wk):
    bn, L, w = xc.shape
    tm = _row_tile(L)
    tok = pl.BlockSpec((1, tm, w), lambda b, i: (b, i, 0))
    wsp = pl.BlockSpec(wq.shape, lambda b, i: (0, 0, 0))
    return pl.pallas_call(
        _qk_kernel,
        grid=(bn, L // tm),
        in_specs=[tok, wsp, wsp],
        out_specs=[tok, tok],
        out_shape=[jax.ShapeDtypeStruct((bn, L, w), BF16)] * 2,
        compiler_params=_cparams(("arbitrary", "arbitrary")),
        name="mlstm_qk",
    )(xc, wq, wk)


def _mlstm_kernel(bias_ref, q_ref, k_ref, v_ref, g_ref, c0_ref, n0_ref, m0_ref,
                  h_ref, cf_ref, nf_ref, mf_ref, c_s, n_s, m_s, lf_s, b_s):
    hd = pl.program_id(0)
    bn = q_ref.shape[0]
    nc = g_ref.shape[2]
    t = ML_CHUNK
    chains = [(b, d) for b in range(bn) for d in range(2)]
    for i, (b, d) in enumerate(chains):
        c_s[i] = c0_ref[b, d, 0]
        n_s[i] = n0_ref[b, d, 0]
        m_s[i] = m0_ref[b, d, 0]
    row = lax.broadcasted_iota(jnp.int32, (t, t), 0)
    col = lax.broadcasted_iota(jnp.int32, (t, t), 1)
    tri = (col <= row, col >= row)
    cum = ((row <= col).astype(F32), (row >= col).astype(F32))
    bias_i = [bias_ref[d * 2 * ML_HEADS + hd] for d in range(2)]
    bias_f = [bias_ref[d * 2 * ML_HEADS + ML_HEADS + hd] for d in range(2)]
    for i, (b, d) in enumerate(chains):
        lf_all = jax.nn.log_sigmoid(g_ref[b, d * 2 * ML_HEADS + ML_HEADS + hd] + bias_f[d])
        lf_s[i] = lf_all
        b_s[i] = jnp.dot(lf_all, cum[d], precision=HIGHEST, preferred_element_type=F32)

    def body(j, carry):
        ids = range(len(chains))
        cidx = [j if d == 0 else nc - 1 - j for _, d in chains]
        r0 = [pl.multiple_of(c * t, t) for c in cidx]
        q = [q_ref[b, pl.ds(r0[i], t), :] for i, (b, d) in enumerate(chains)]
        k = [k_ref[b, pl.ds(r0[i], t), :] for i, (b, d) in enumerate(chains)]
        v = [v_ref[b, pl.ds(r0[i], t), :] for i, (b, d) in enumerate(chains)]
        li = [g_ref[b, d * 2 * ML_HEADS + hd, pl.ds(cidx[i], 1), :] + bias_i[d] for i, (b, d) in enumerate(chains)]
        lf = [lf_s[i, pl.ds(cidx[i], 1), :] for i in ids]
        b_row = [b_s[i, pl.ds(cidx[i], 1), :] for i in ids]
        b_col = [jnp.sum(jnp.where(tri[d], lf[i], 0.0), axis=-1, keepdims=True) for i, (b, d) in enumerate(chains)]
        g = [jnp.sum(lf[i], axis=-1, keepdims=True) for i in ids]
        a_row = [g[i] - b_row[i] + li[i] for i in ids]
        m_loc = [jnp.max(a_row[i], axis=-1, keepdims=True) for i in ids]
        w_row = [jnp.exp(a_row[i] - m_loc[i]) for i in ids]
        c_prev = [c_s[i] for i in ids]
        n_prev = [n_s[i] for i in ids]
        m_prev = [m_s[i][:, 0:1] for i in ids]
        qkc = [_dot_nt(q[i], jnp.concatenate([k[i], c_prev[i].astype(BF16)], axis=0)) for i in ids]
        lhs = [jnp.concatenate([v[i].astype(F32).T * w_row[i], jnp.broadcast_to(w_row[i], (2 * SUBLANES, t))], axis=0)
               for i in ids]
        cn = [_dot(lhs[i].astype(BF16), k[i]) for i in ids]
        dmat = [jnp.where(tri[d], b_col[i] - b_row[i] + li[i], -jnp.inf) for i, (b, d) in enumerate(chains)]
        inter = [b_col[i] + m_prev[i] for i in ids]
        m_t = [jnp.maximum(jnp.max(dmat[i], axis=-1, keepdims=True), inter[i]) for i in ids]
        s = [qkc[i][:, :t] * jnp.exp(dmat[i] - m_t[i]) for i in ids]
        w_inter = [jnp.exp(inter[i] - m_t[i]) for i in ids]
        sv = [_dot(s[i].astype(BF16), v[i]) for i in ids]
        nq = [jnp.sum(s[i], axis=-1, keepdims=True)
              + w_inter[i] * jnp.sum(q[i].astype(F32) * n_prev[i], axis=-1, keepdims=True) for i in ids]
        for i, (b, d) in enumerate(chains):
            num = sv[i] + w_inter[i] * qkc[i][:, t:]
            h_ref[d, b, pl.ds(r0[i], t), :] = num / jnp.maximum(jnp.abs(nq[i]), jnp.exp(-m_t[i]))
        for i in ids:
            m_new = jnp.maximum(g[i] + m_prev[i], m_loc[i])
            dec = jnp.exp(g[i] + m_prev[i] - m_new)
            grow = jnp.exp(m_loc[i] - m_new)
            c_s[i] = dec * c_prev[i] + grow * cn[i][:ML_HD]
            n_s[i] = dec * n_prev[i] + grow * cn[i][ML_HD:ML_HD + 1]
            m_s[i] = jnp.broadcast_to(m_new, (1, LANES))
        return carry
    lax.fori_loop(0, nc, body, 0)
    for i, (b, d) in enumerate(chains):
        cf_ref[b, d, 0] = c_s[i]
        nf_ref[b, d, 0] = n_s[i]
        mf_ref[b, d, 0] = m_s[i]


def mlstm_scan(q, k, v, gates_t, gate_bias, state):
    bn, L, w = q.shape
    c0, n0, m0 = state
    hsp = pl.BlockSpec((bn, L, ML_HD), lambda h: (0, 0, h), pipeline_mode=pl.Buffered(1))
    st = lambda a: pl.BlockSpec((bn, 2, 1) + a.shape[3:], lambda h: (0, 0, h, 0, 0))
    return pl.pallas_call(
        _mlstm_kernel,
        grid=(ML_HEADS,),
        in_specs=[pl.BlockSpec(memory_space=pltpu.SMEM), hsp, hsp, hsp,
                  pl.BlockSpec(gates_t.shape, lambda h: (0, 0, 0, 0)),
                  st(c0), st(n0), st(m0)],
        out_specs=[pl.BlockSpec((2, bn, L, ML_HD), lambda h: (0, 0, 0, h)), st(c0), st(n0), st(m0)],
        out_shape=[jax.ShapeDtypeStruct((2, bn, L, w), F32),
                   jax.ShapeDtypeStruct(c0.shape, F32), jax.ShapeDtypeStruct(n0.shape, F32),
                   jax.ShapeDtypeStruct(m0.shape, F32)],
        scratch_shapes=[pltpu.VMEM((2 * bn, ML_HD, ML_HD), F32), pltpu.VMEM((2 * bn, 1, ML_HD), F32),
                        pltpu.VMEM((2 * bn, 1, LANES), F32),
                        pltpu.VMEM((2 * bn,) + gates_t.shape[2:], F32),
                        pltpu.VMEM((2 * bn,) + gates_t.shape[2:], F32)],
        compiler_params=_cparams(("arbitrary",)),
        name="mlstm_scan",
    )(gate_bias, q, k, v, gates_t, c0, n0, m0)


def _mlout_kernel(hf_ref, hb_ref, o_ref, xc_ref, gain_ref, skip_ref, y_ref):
    h = hf_ref[0, 0] + hb_ref[0, 0]
    parts = []
    for hd in range(ML_HEADS):
        sl = slice(hd * ML_HD, (hd + 1) * ML_HD)
        hh = h[:, sl]
        parts.append(hh * lax.rsqrt(jnp.mean(hh * hh, axis=-1, keepdims=True) + EPS) * gain_ref[:, sl])
    hn = jnp.concatenate(parts, axis=-1)
    y_ref[0] = jax.nn.sigmoid(o_ref[0]) * (hn + skip_ref[...] * xc_ref[0])


def mlstm_output(h2, p_o, xc, gain, skip):
    _, bn, L, w = h2.shape
    tm = _row_tile(L)
    tok = pl.BlockSpec((1, tm, w), lambda b, i: (b, i, 0))
    row = pl.BlockSpec((1, w), lambda b, i: (0, 0))
    return pl.pallas_call(
        _mlout_kernel,
        grid=(bn, L // tm),
        in_specs=[pl.BlockSpec((1, 1, tm, w), lambda b, i: (0, b, i, 0)),
                  pl.BlockSpec((1, 1, tm, w), lambda b, i: (1, b, i, 0)), tok, tok, row, row],
        out_specs=tok,
        out_shape=jax.ShapeDtypeStruct((bn, L, w), F32),
        compiler_params=_cparams(("arbitrary", "arbitrary")),
        name="mlstm_output",
    )(h2, h2, p_o, xc, gain.reshape(1, w), skip.reshape(1, w))


def mlstm_mix(p_x, p_v, p_o, p_g, conv_w9, width, rows, wq, wk, gate_bias, skip, gain, state):
    bn, L, _ = p_x.shape
    xc = depthwise_conv(p_x, conv_w9, width=width, rows=rows, act=_silu)
    q, k = mlstm_qk(xc, wq, wk)
    gates_t = jnp.swapaxes(p_g[..., :ML_GATES], 1, 2).reshape(bn, ML_GATES, L // ML_CHUNK, ML_CHUNK)
    h2, cf, nf, mf = mlstm_scan(q, k, p_v, gates_t, gate_bias.reshape(ML_GATES), state)
    return mlstm_output(h2, p_o, xc, gain, skip), (cf, nf, mf)


PROJ_SPLITS = ((0, 3 * HY_W), (3 * HY_W, 3 * HY_W + S5_W),
               (3 * HY_W + S5_W, 3 * HY_W + S5_W + ML_W),
               (3 * HY_W + S5_W + ML_W, 3 * HY_W + S5_W + 2 * ML_W),
               (3 * HY_W + S5_W + 2 * ML_W, 3 * HY_W + S5_W + 3 * ML_W),
               (3 * HY_W + S5_W + 3 * ML_W, 3 * HY_W + S5_W + 3 * ML_W + LANES))
PROJ_DTYPES = (F32, F32, F32, BF16, F32, F32)


def kernel(x, c, ctx, c_ctx, w_mod, b_mod, g_pre_mix, g_post_mix, g_pre_mlp, g_post_mlp, w_in, w_out, hy_conv, hy_w1, hy_b1, hy_w2, hy_b2, hy_w3, hy_freq, hy_decay, hy_bias, s5_a_re, s5_a_im, s5_log_dt, s5_b_re, s5_b_im, s5_c_re, s5_c_im, s5_d, s5_w_glu, s5_b_glu, ml_conv, ml_wq, ml_wk, ml_gate_bias, ml_skip, ml_norm_gain, w_mlp1, w_mlp2):
    bsz, seq_len, d = x.shape
    ctx_len = ctx.shape[1]
    depth = w_mod.shape[0]

    cc = jnp.zeros((SUBLANES, d), F32).at[:bsz].set(c).at[bsz].set(c_ctx)
    mods = mod_vectors(cc, w_mod, b_mod)

    proj_w = w_in.shape[-1]
    w_in_b = jnp.pad(w_in, ((0, 0), (0, 0), (0, PROJ_SPLITS[-1][1] - proj_w))).astype(BF16)
    w_out_b, w1_b, w2_b = w_out.astype(BF16), w_mlp1.astype(BF16), w_mlp2.astype(BF16)
    wq_b, wk_b, wglu_b = ml_wq.astype(BF16), ml_wk.astype(BF16), s5_w_glu.astype(BF16)

    s5_zero = jnp.zeros((bsz, 1, 4 * S5_GROUPS * S5_N), F32)
    ml_zero = (jnp.zeros((bsz, 2, ML_HEADS, ML_HD, ML_HD), F32),
               jnp.zeros((bsz, 2, ML_HEADS, 1, ML_HD), F32),
               jnp.zeros((bsz, 2, ML_HEADS, 1, LANES), F32))

    for l in range(depth):
        mx = [mods[l, :bsz, j * d:(j + 1) * d][:, None, :] for j in range(6)]
        mc = [jnp.broadcast_to(mods[l, bsz, j * d:(j + 1) * d][None, None, :], (bsz, 1, d)) for j in range(6)]
        filt_x = hyena_filters(seq_len, hy_w1[l], hy_b1[l], hy_w2[l], hy_b2[l], hy_w3[l], hy_freq[l], hy_decay[l])
        filt_c = hyena_filters(ctx_len, hy_w1[l], hy_b1[l], hy_w2[l], hy_b2[l], hy_w3[l], hy_freq[l], hy_decay[l])
        s5_mats = s5_matrices(s5_a_re[l], s5_a_im[l], s5_log_dt[l], s5_b_re[l], s5_b_im[l], s5_c_re[l], s5_c_im[l])
        conv2d_w = ml_conv[l].reshape(9, ML_W)

        def mix(h_in, mod, L, fft_len, filt, width, rows, s5_h0, ml_state):
            p_hy, p_s5, p_mx, p_mv, p_mo, p_mg = in_projection(h_in, g_pre_mix[l], mod[1], mod[0], w_in_b[l],
                                                               PROJ_SPLITS, PROJ_DTYPES)
            y_hy = hyena_mix(p_hy, hy_conv[l], filt, hy_bias[l], fft_len)
            y_s5, s5_fin = s5_mix(p_s5, s5_mats, s5_d[l], wglu_b[l], s5_b_glu[l], s5_h0)
            y_ml, ml_fin = mlstm_mix(p_mx, p_mv, p_mo, p_mg, conv2d_w, width, rows, wq_b[l], wk_b[l],
                                     ml_gate_bias[l], ml_skip[l], ml_norm_gain[l], ml_state)
            return (y_hy, y_s5, y_ml), s5_fin, ml_fin

        y_c, s5_state, ml_state = mix(ctx, mc, ctx_len, CTX_FFT_LEN, filt_c, ctx_len, (0,), s5_zero, ml_zero)
        y_x, _, _ = mix(x, mx, seq_len, seq_len, filt_x, GRID_W, (-1, 0, 1), s5_state, ml_state)
        x = out_projection(*y_x, w_out_b[l], g_post_mix[l], mx[2], x)
        x = mlp_block(x, g_pre_mlp[l], mx[4], mx[3], w1_b[l], w2_b[l], g_post_mlp[l], mx[5])
        if l < depth - 1:
            ctx = out_projection(*y_c, w_out_b[l], g_post_mix[l], mc[2], ctx)
            ctx = mlp_block(ctx, g_pre_mlp[l], mc[4], mc[3], w1_b[l], w2_b[l], g_post_mlp[l], mc[5])
    return x
```

```python
import functools
import math

import numpy as np
import jax
import jax.numpy as jnp
from jax import lax
from jax.experimental import pallas as pl
from jax.experimental.pallas import tpu as pltpu

F32 = jnp.float32
BF16 = jnp.bfloat16
EPS = 1e-6
HIGHEST = lax.Precision.HIGHEST

V7X_VMEM_BYTES = 64 * 1024 * 1024
VMEM_LIMIT = V7X_VMEM_BYTES - 8 * 1024 * 1024
LANES = 128
SUBLANES = 8
FFT_N2 = 64
FFT_UNROLL_A = 4
FFT_UNROLL_B = 8

HY_W = 256
HY_BANDS = 8
HY_EMB = 1 + 2 * HY_BANDS
HY_FF = 64
S5_W = 256
S5_CH = 16
S5_GROUPS = S5_W // S5_CH
S5_N = 64
S5_T = 8
ML_HEADS = 4
ML_HD = 128
ML_W = ML_HEADS * ML_HD
ML_CHUNK = 64
ML_GATES = 4 * ML_HEADS
GRID_W = 64
CTX_FFT_LEN = 512


def _cparams(sem):
    return pltpu.CompilerParams(dimension_semantics=sem, vmem_limit_bytes=VMEM_LIMIT)


def _dot(a, b):
    return jnp.dot(a, b, preferred_element_type=F32)


def _dot_nt(a, b):
    return lax.dot_general(a, b, (((1,), (1,)), ((), ())), preferred_element_type=F32)


def _rms(x, g):
    return x * lax.rsqrt(jnp.mean(x * x, axis=-1, keepdims=True) + EPS) * g


def _silu(x):
    return x * jax.nn.sigmoid(x)


def _mod_kernel(c_ref, w_ref, b_ref, o_ref):
    s = _silu(c_ref[...]).astype(BF16)
    o_ref[0] = _dot(s, w_ref[0].astype(BF16)) + b_ref[0]


def mod_vectors(cc, w_mod, b_mod):
    depth, d, n = w_mod.shape
    r = cc.shape[0]
    tn = 1536
    return pl.pallas_call(
        _mod_kernel,
        grid=(depth, n // tn),
        in_specs=[pl.BlockSpec((r, d), lambda l, j: (0, 0)),
                  pl.BlockSpec((1, d, tn), lambda l, j: (l, 0, j)),
                  pl.BlockSpec((1, 1, tn), lambda l, j: (l, 0, j))],
        out_specs=pl.BlockSpec((1, r, tn), lambda l, j: (l, 0, j)),
        out_shape=jax.ShapeDtypeStruct((depth, r, n), F32),
        compiler_params=_cparams(("arbitrary", "arbitrary")),
        name="mod_vectors",
    )(cc, w_mod, b_mod.reshape(depth, 1, n))


def _row_tile(L):
    return min(L, 512)


def _inproj_kernel(x_ref, g_ref, sc_ref, sh_ref, w_ref, *o_refs, splits):
    h = _rms(x_ref[0], g_ref[...]) * (1.0 + sc_ref[0]) + sh_ref[0]
    hb = h.astype(BF16)
    for o_ref, (a, b) in zip(o_refs, splits):
        o_ref[0] = _dot(hb, w_ref[:, a:b]).astype(o_ref.dtype)


def in_projection(x, g, sc, sh, w, splits, dtypes):
    bn, L, d = x.shape
    tm = _row_tile(L)
    vec = pl.BlockSpec((1, 1, d), lambda b, i: (b, 0, 0))
    return pl.pallas_call(
        functools.partial(_inproj_kernel, splits=splits),
        grid=(bn, L // tm),
        in_specs=[pl.BlockSpec((1, tm, d), lambda b, i: (b, i, 0)),
                  pl.BlockSpec((1, d), lambda b, i: (0, 0)), vec, vec,
                  pl.BlockSpec(w.shape, lambda b, i: (0, 0))],
        out_specs=[pl.BlockSpec((1, tm, b_ - a_), lambda b, i: (b, i, 0)) for a_, b_ in splits],
        out_shape=[jax.ShapeDtypeStruct((bn, L, b_ - a_), dt) for (a_, b_), dt in zip(splits, dtypes)],
        compiler_params=_cparams(("arbitrary", "arbitrary")),
        name="in_projection",
    )(x, g.reshape(1, d), sc, sh, w)


def _outproj_kernel(yh_ref, ys0_ref, ys1_ref, ym_ref, w_ref, g_ref, gt_ref, x_ref, o_ref):
    a = yh_ref.shape[-1]
    acc = _dot(yh_ref[0].astype(BF16), w_ref[0:a])
    acc += _dot(ys0_ref[0, 0].astype(BF16), w_ref[a:a + LANES])
    acc += _dot(ys1_ref[0, 0].astype(BF16), w_ref[a + LANES:a + 2 * LANES])
    acc += _dot(ym_ref[0].astype(BF16), w_ref[a + 2 * LANES:])
    o_ref[0] = x_ref[0] + gt_ref[0] * _rms(acc, g_ref[...])


def out_projection(y_hy, y_s5, y_ml, w, g, gate, x):
    bn, L, d = x.shape
    tm = _row_tile(L)
    tok = lambda wd: pl.BlockSpec((1, tm, wd), lambda b, i: (b, i, 0))
    half = lambda h: pl.BlockSpec((1, 1, tm, LANES), lambda b, i: (h, b, i, 0))
    return pl.pallas_call(
        _outproj_kernel,
        grid=(bn, L // tm),
        in_specs=[tok(y_hy.shape[-1]), half(0), half(1), tok(y_ml.shape[-1]),
                  pl.BlockSpec(w.shape, lambda b, i: (0, 0)),
                  pl.BlockSpec((1, d), lambda b, i: (0, 0)),
                  pl.BlockSpec((1, 1, d), lambda b, i: (b, 0, 0)), tok(d)],
        out_specs=tok(d),
        out_shape=jax.ShapeDtypeStruct((bn, L, d), F32),
        compiler_params=_cparams(("arbitrary", "arbitrary")),
        name="out_projection",
    )(y_hy, y_s5, y_s5, y_ml, w, g.reshape(1, d), gate, x)


def _mlp_kernel(x_ref, g1_ref, sc_ref, sh_ref, w1_ref, w2_ref, g2_ref, gt_ref, o_ref, h_ref, acc_ref):
    j = pl.program_id(2)

    @pl.when(j == 0)
    def _():
        h = _rms(x_ref[0], g1_ref[...]) * (1.0 + sc_ref[0]) + sh_ref[0]
        h_ref[...] = h.astype(BF16)
        acc_ref[...] = jnp.zeros_like(acc_ref)

    a = jnp.maximum(_dot(h_ref[...], w1_ref[...]), 0.0)
    acc_ref[...] += _dot((a * a).astype(BF16), w2_ref[...])

    @pl.when(j == pl.num_programs(2) - 1)
    def _():
        o_ref[0] = x_ref[0] + gt_ref[0] * _rms(acc_ref[...], g2_ref[...])


def mlp_block(x, g_pre, sc, sh, w1, w2, g_post, gate):
    bn, L, d = x.shape
    hid = w1.shape[1]
    tm = _row_tile(L)
    th = 1024
    tok = pl.BlockSpec((1, tm, d), lambda b, i, j: (b, i, 0))
    vec = pl.BlockSpec((1, 1, d), lambda b, i, j: (b, 0, 0))
    gain = pl.BlockSpec((1, d), lambda b, i, j: (0, 0))
    return pl.pallas_call(
        _mlp_kernel,
        grid=(bn, L // tm, hid // th),
        in_specs=[tok, gain, vec, vec,
                  pl.BlockSpec((d, th), lambda b, i, j: (0, j)),
                  pl.BlockSpec((th, d), lambda b, i, j: (j, 0)),
                  gain, vec],
        out_specs=tok,
        out_shape=jax.ShapeDtypeStruct((bn, L, d), F32),
        scratch_shapes=[pltpu.VMEM((tm, d), BF16), pltpu.VMEM((tm, d), F32)],
        compiler_params=_cparams(("arbitrary", "arbitrary", "arbitrary")),
        name="mlp_block",
    )(x, g_pre.reshape(1, d), sc, sh, w1, w2, g_post.reshape(1, d), gate)


def _dwconv_kernel(x_ref, w_ref, o_ref, pad_ref, *, width, taps, act, pad):
    L = x_ref.shape[1]
    ch = x_ref.shape[2]
    pad_ref[0:pad, :] = jnp.zeros((pad, ch), F32)
    pad_ref[pad + L:pad + L + pad, :] = jnp.zeros((pad, ch), F32)
    pad_ref[pad:pad + L, :] = x_ref[0]
    tr = min(L, 256)
    for r0 in range(0, L, tr):
        col = (lax.broadcasted_iota(jnp.int32, (tr, ch), 0) + r0) & (width - 1)
        acc = jnp.zeros((tr, ch), F32)
        for dr, dc in taps:
            start = pad + r0 + dr * width + dc
            v = pad_ref[start:start + tr, :]
            if dc == -1:
                v = jnp.where(col >= 1, v, 0.0)
            elif dc == 1:
                v = jnp.where(col <= width - 2, v, 0.0)
            acc = acc + w_ref[(dr + 1) * 3 + (dc + 1):(dr + 1) * 3 + (dc + 2), :] * v
        o_ref[0, r0:r0 + tr, :] = act(acc)


def depthwise_conv(x, w9, width, rows, act=None):
    bn, L, ch = x.shape
    assert width & (width - 1) == 0
    taps = tuple((dr, dc) for dr in rows for dc in (-1, 0, 1))
    pad = -(-(width + 1) // SUBLANES) * SUBLANES if len(rows) > 1 else SUBLANES
    act = act or (lambda a: a)
    cs = LANES
    return pl.pallas_call(
        functools.partial(_dwconv_kernel, width=width, taps=taps, act=act, pad=pad),
        grid=(bn, ch // cs),
        in_specs=[pl.BlockSpec((1, L, cs), lambda b, c: (b, 0, c)),
                  pl.BlockSpec((9, cs), lambda b, c: (0, c))],
        out_specs=pl.BlockSpec((1, L, cs), lambda b, c: (b, 0, c)),
        out_shape=jax.ShapeDtypeStruct((bn, L, ch), F32),
        scratch_shapes=[pltpu.VMEM((L + 2 * pad, cs), F32)],
        compiler_params=_cparams(("arbitrary", "arbitrary")),
        name="depthwise_conv",
    )(x, w9)


def _hyfilt_kernel(w1_ref, b1_ref, w2_ref, b2_ref, w3_ref, fr_ref, dec_ref, o_ref, *, L):
    tr = min(L, 512)
    wd = o_ref.shape[1]
    half = wd // 2
    fr = fr_ref[...]
    ssq = jnp.zeros((1, wd), F32)
    for r0 in range(0, L, tr):
        t = (lax.broadcasted_iota(jnp.int32, (tr, 32), 0) + r0).astype(F32) / L
        lane = lax.broadcasted_iota(jnp.int32, (tr, 32), 1)
        band = jnp.where(lane <= HY_BANDS, lane, lane - HY_BANDS).astype(F32)
        ang = 2.0 * math.pi * t * band
        feat = jnp.where(lane == 0, t, jnp.where(lane <= HY_BANDS, jnp.cos(ang),
                                                 jnp.where(lane <= 2 * HY_BANDS, jnp.sin(ang), 0.0)))
        hdn = jnp.sin(fr * (jnp.dot(feat, w1_ref[...], precision=HIGHEST, preferred_element_type=F32) + b1_ref[...]))
        hdn = jnp.sin(fr * (jnp.dot(hdn, w2_ref[...], precision=HIGHEST, preferred_element_type=F32) + b2_ref[...]))
        filt = _dot(hdn.astype(BF16), w3_ref[...].astype(BF16))
        filt = filt * jnp.exp(-t[:, 0:1] * dec_ref[...])
        ssq = ssq + jnp.sum(filt * filt, axis=0, keepdims=True)
        o_ref[r0:r0 + tr, :] = filt
    tot = ssq[:, :half] + ssq[:, half:]
    scale = lax.rsqrt(tot + EPS)
    scale = jnp.concatenate([scale, scale], axis=1)
    for r0 in range(0, L, tr):
        o_ref[r0:r0 + tr, :] = o_ref[r0:r0 + tr, :] * scale


def hyena_filters(L, w1, b1, w2, b2, w3, freq, decay):
    order = decay.shape[0]
    wd = 2 * HY_W
    w1p = jnp.zeros((32, HY_FF), F32).at[:HY_EMB].set(w1)
    full = lambda a: pl.BlockSpec(a.shape, lambda o: (0,) * a.ndim)
    args = (w1p, b1.reshape(1, HY_FF), w2, b2.reshape(1, HY_FF))
    return pl.pallas_call(
        functools.partial(_hyfilt_kernel, L=L),
        grid=(order,),
        in_specs=[full(a) for a in args] + [pl.BlockSpec((HY_FF, wd), lambda o: (0, o)),
                                            pl.BlockSpec((1, HY_FF), lambda o: (0, 0)),
                                            pl.BlockSpec((1, wd), lambda o: (0, o))],
        out_specs=pl.BlockSpec((L, wd), lambda o: (0, o)),
        out_shape=jax.ShapeDtypeStruct((L, order * wd), F32),
        compiler_params=_cparams(("arbitrary",)),
        name="hyena_filters",
    )(*args, w3, freq.reshape(1, HY_FF), decay.reshape(1, order * wd))


@functools.lru_cache(maxsize=None)
def _dft_consts(L):
    n = 2 * L
    n2c = FFT_N2
    n1c = n // n2c
    n1h = n1c // 2
    n1 = np.arange(n1h)[None, None, :]
    k1 = np.arange(n1c)[None, :, None]
    n2 = np.arange(n2c)[:, None, None]
    ang = -2.0 * np.pi * (k1 * n1 / n1c + n2 * k1 / n)
    fr, fi = np.cos(ang), np.sin(ang)
    fa = np.concatenate([np.concatenate([fr, -fi], axis=2),
                         np.concatenate([fi, fr], axis=2)], axis=1)
    frt = np.transpose(fr, (0, 2, 1)) / n
    fit = -np.transpose(fi, (0, 2, 1)) / n
    fai = np.concatenate([np.concatenate([frt, -fit], axis=2),
                          np.concatenate([fit, frt], axis=2)], axis=1)
    k2 = np.arange(n2c)[:, None]
    m2 = np.arange(n2c)[None, :]
    angb = -2.0 * np.pi * k2 * m2 / n2c
    gr, gi = np.cos(angb), np.sin(angb)
    gb = np.block([[gr, -gi], [gi, gr]])
    gbi = np.block([[gr, gi], [-gi, gr]])
    return (fa.astype(np.float32), gb.astype(np.float32), gbi.astype(np.float32), fai.astype(np.float32))


def _fft_stage_a(load_rhs, fa_ref, s_ref, n1c):
    def body(n2, carry):
        rhs = load_rhs(n2).astype(BF16)
        res = _dot(fa_ref[n2], rhs)
        s_ref[pl.ds(pl.multiple_of(n2 * 2 * n1c, 2 * n1c), 2 * n1c), :] = res
        return carry
    lax.fori_loop(0, FFT_N2, body, 0, unroll=FFT_UNROLL_A)


def _fft_load_k1(s_ref, k1, n1c):
    xr = s_ref[pl.ds(k1, FFT_N2, stride=2 * n1c), :]
    xi = s_ref[pl.ds(n1c + k1, FFT_N2, stride=2 * n1c), :]
    return jnp.concatenate([xr, xi], axis=0)


def _fftconv_kernel(z_ref, gate_ref, bias_ref, h_ref, fa_ref, gb_ref, gbi_ref, fai_ref, o_ref, s_ref, *, L):
    n1c = L // 32
    n1h = n1c // 2
    half = FFT_N2

    def load_rhs(n2):
        za = z_ref[0, 0, pl.ds(n2, n1h, stride=FFT_N2), :]
        zb = z_ref[0, 1, pl.ds(n2, n1h, stride=FFT_N2), :]
        return jnp.concatenate([za, zb], axis=0)
    _fft_stage_a(load_rhs, fa_ref, s_ref, n1c)

    def stage_b(k1, carry):
        x = _dot(gb_ref[...], _fft_load_k1(s_ref, k1, n1c).astype(BF16))
        xr, xi = x[:half], x[half:]
        h = h_ref[k1]
        hr, hi = h[:half], h[half:]
        y = jnp.concatenate([xr * hr - xi * hi, xr * hi + xi * hr], axis=0)
        bp = _dot(gbi_ref[...], y.astype(BF16))
        s_ref[pl.ds(k1, FFT_N2, stride=2 * n1c), :] = bp[:half]
        s_ref[pl.ds(n1c + k1, FFT_N2, stride=2 * n1c), :] = bp[half:]
        return carry
    lax.fori_loop(0, n1c, stage_b, 0, unroll=FFT_UNROLL_B)

    bias = bias_ref[...]

    def stage_a_inv(n2, carry):
        rhs = s_ref[pl.ds(pl.multiple_of(n2 * 2 * n1c, 2 * n1c), 2 * n1c), :].astype(BF16)
        res = _dot(fai_ref[n2], rhs)
        for p in range(2):
            rows = pl.ds(n2, n1h, stride=FFT_N2)
            zin = z_ref[0, p, rows, :]
            o_ref[0, p, rows, :] = gate_ref[0, p, rows, :] * (res[p * n1h:(p + 1) * n1h] + bias * zin)
        return carry
    lax.fori_loop(0, FFT_N2, stage_a_inv, 0, unroll=FFT_UNROLL_A)


def fft_gated_conv(z, zoff, gate, goff, bias, hspec):
    bsz, L, _ = z.shape
    ch = bias.shape[0]
    n1c = L // 32
    fa, gb, gbi, fai = (jnp.asarray(a, BF16) for a in _dft_consts(L))
    zp = z.reshape(bsz // 2, 2, L, z.shape[-1])
    gp = gate.reshape(bsz // 2, 2, L, gate.shape[-1])
    cs = LANES
    one = pl.Buffered(1)
    const3 = lambda a: pl.BlockSpec(a.shape, lambda c, p: (0, 0, 0), pipeline_mode=one)
    const2 = lambda a: pl.BlockSpec(a.shape, lambda c, p: (0, 0), pipeline_mode=one)
    out = pl.pallas_call(
        functools.partial(_fftconv_kernel, L=L),
        grid=(ch // cs, bsz // 2),
        in_specs=[pl.BlockSpec((1, 2, L, cs), lambda c, p: (p, 0, 0, c + zoff)),
                  pl.BlockSpec((1, 2, L, cs), lambda c, p: (p, 0, 0, c + goff)),
                  pl.BlockSpec((1, cs), lambda c, p: (0, c)),
                  pl.BlockSpec((n1c, 2 * FFT_N2, cs), lambda c, p: (0, 0, c), pipeline_mode=one),
                  const3(fa), const2(gb), const2(gbi), const3(fai)],
        out_specs=pl.BlockSpec((1, 2, L, cs), lambda c, p: (p, 0, 0, c)),
        out_shape=jax.ShapeDtypeStruct((bsz // 2, 2, L, ch), F32),
        scratch_shapes=[pltpu.VMEM((FFT_N2 * 2 * n1c, cs), F32)],
        compiler_params=_cparams(("arbitrary", "arbitrary")),
        name="fft_gated_conv",
    )(zp, gp, bias.reshape(1, ch), hspec, fa, gb, gbi, fai)
    return out.reshape(bsz, L, ch)


def _fftspec_kernel(hf_ref, hb_ref, fa_ref, gb_ref, o_ref, s_ref, *, L):
    n1c = L // 32
    n1h = n1c // 2
    half = FFT_N2
    for d, src in enumerate((hf_ref, hb_ref)):
        def load_rhs(n2, src=src, d=d):
            h = src[pl.ds(n2, n1h, stride=FFT_N2), :]
            if d == 1:
                row = lax.broadcasted_iota(jnp.int32, h.shape, 0)
                h = jnp.where((row == 0) & (n2 == 0), 0.0, h)
            return jnp.concatenate([h, jnp.zeros_like(h)], axis=0)
        _fft_stage_a(load_rhs, fa_ref, s_ref, n1c)

        def stage_b(k1, carry, d=d):
            x = _dot(gb_ref[...], _fft_load_k1(s_ref, k1, n1c).astype(BF16))
            if d == 0:
                o_ref[k1] = x
            else:
                o_ref[k1] = o_ref[k1] + jnp.concatenate([x[:half], -x[half:]], axis=0)
            return carry
        lax.fori_loop(0, n1c, stage_b, 0, unroll=FFT_UNROLL_B)


def fft_filter_spectrum(filt, foff, boff, L):
    if filt.shape[0] < L:
        filt = jnp.pad(filt, ((0, L - filt.shape[0]), (0, 0)))
    n1c = L // 32
    fa, gb, _, _ = (jnp.asarray(a, BF16) for a in _dft_consts(L))
    cs = LANES
    return pl.pallas_call(
        functools.partial(_fftspec_kernel, L=L),
        grid=(HY_W // cs,),
        in_specs=[pl.BlockSpec((L, cs), lambda c: (0, c + foff)), pl.BlockSpec((L, cs), lambda c: (0, c + boff)),
                  pl.BlockSpec(fa.shape, lambda c: (0, 0, 0)), pl.BlockSpec(gb.shape, lambda c: (0, 0))],
        out_specs=pl.BlockSpec((n1c, 2 * FFT_N2, cs), lambda c: (0, 0, c)),
        out_shape=jax.ShapeDtypeStruct((n1c, 2 * FFT_N2, HY_W), F32),
        scratch_shapes=[pltpu.VMEM((FFT_N2 * 2 * n1c, cs), F32)],
        compiler_params=_cparams(("arbitrary",)),
        name="fft_filter_spectrum",
    )(filt, filt, fa, gb)


def hyena_mix(p_hy, conv_w, filt, bias, fft_len):
    bsz, L, _ = p_hy.shape
    w9 = jnp.zeros((9, 3 * HY_W), F32).at[3:6].set(conv_w)
    u = depthwise_conv(p_hy, w9, width=L, rows=(0,))
    if fft_len > L:
        u = jnp.pad(u, ((0, 0), (0, fft_len - L), (0, 0)))
    nb = HY_W // LANES
    z = fft_gated_conv(u, 0, u, nb, bias[0], fft_filter_spectrum(filt, 0, nb, fft_len))
    z = fft_gated_conv(z, 0, u, 2 * nb, bias[1], fft_filter_spectrum(filt, 2 * nb, 3 * nb, fft_len))
    return z[:, :L]


def s5_matrices(a_re, a_im, log_dt, b_re, b_im, c_re, c_im):
    t = S5_T
    lam = lax.complex(a_re, a_im)
    a_bar = jnp.exp(lam * jnp.exp(log_dt)[..., None])
    b_bar = ((a_bar - 1.0) / lam)[..., None] * lax.complex(b_re, b_im)
    c_mat = lax.complex(c_re, c_im)
    pw = jnp.stack([a_bar ** j for j in range(t + 1)], axis=1)
    kern = jnp.einsum('xgcn,xjgn,xgnd->xjgcd', c_mat, pw[:, :t], b_bar).real
    s = np.arange(t)[:, None]
    tt = np.arange(t)[None, :]
    sel_f = np.stack([(tt - s == j) for j in range(t)]).astype(np.float32)
    sel_b = np.stack([(s - tt == j) for j in range(t)]).astype(np.float32)
    eye = jnp.eye(S5_GROUPS, dtype=F32)
    m6 = (jnp.einsum('jst,jgcd,gh->sgdthc', sel_f, kern[0], eye)
          + jnp.einsum('jst,jgcd,gh->sgdthc', sel_b, kern[1], eye))
    m_intra = m6.reshape(t * S5_W, t * S5_W)
    pf = pw[0, :t][::-1]
    pb = pw[1, :t]
    rf = pf[..., None] * b_bar[0][None]
    rb = pb[..., None] * b_bar[1][None]

    def in_mat(r):
        both = jnp.stack([r.real, r.imag], axis=0)
        return jnp.einsum('rsgnd,gh->sgdrhn', both, eye).reshape(t * S5_W, 2 * S5_GROUPS * S5_N)
    r_in = jnp.concatenate([in_mat(rf), in_mat(rb)], axis=1)
    cf = c_mat[0][None] * jnp.moveaxis(pw[0, 1:], -1, -1)[:, :, None, :]
    cb = c_mat[1][None] * pw[1, 1:][::-1][:, :, None, :]

    def out_mat(cm):
        both = jnp.stack([cm.real, -cm.imag], axis=0)
        return jnp.einsum('rtgcn,gh->rgnthc', both, eye).reshape(2 * S5_GROUPS * S5_N, t * S5_W)
    r_out = jnp.concatenate([out_mat(cf), out_mat(cb)], axis=0)
    a8 = pw[:, t].reshape(2, S5_GROUPS * S5_N)
    a8v = jnp.concatenate([a8[0].real, a8[0].imag, a8[1].real, a8[1].imag])[None, :]
    return m_intra.astype(BF16), r_in.astype(BF16), r_out.astype(BF16), a8v


def _row_tokens(x0_ref, x1_ref, t, rows):
    sl = pl.ds(t, rows, stride=S5_T)
    return jnp.concatenate([x0_ref[0, sl, :], x1_ref[0, sl, :]], axis=1)


def _s5in_kernel(x0_ref, x1_ref, w_ref, o_ref):
    rows = o_ref.shape[1]
    wd = 2 * LANES
    acc = _dot(_row_tokens(x0_ref, x1_ref, 0, rows).astype(BF16), w_ref[0:wd, :])
    for t in range(1, S5_T):
        acc += _dot(_row_tokens(x0_ref, x1_ref, t, rows).astype(BF16), w_ref[t * wd:(t + 1) * wd, :])
    o_ref[0] = acc


def s5_state_inputs(x, w, tn):
    bn, L, wd = x.shape
    assert wd == 2 * LANES
    r = L // S5_T
    n = w.shape[1]
    tm = min(r, 256)
    half = lambda h: pl.BlockSpec((1, tm * S5_T, LANES), lambda j, b, i: (b, i, h))
    return pl.pallas_call(
        _s5in_kernel,
        grid=(n // tn, bn, r // tm),
        in_specs=[half(0), half(1), pl.BlockSpec((S5_T * wd, tn), lambda j, b, i: (0, j))],
        out_specs=pl.BlockSpec((1, tm, tn), lambda j, b, i: (b, i, j)),
        out_shape=jax.ShapeDtypeStruct((bn, r, n), F32),
        compiler_params=_cparams(("arbitrary", "arbitrary", "arbitrary")),
        name="s5_state_inputs",
    )(x, x, w)


def _s5out_kernel(x0_ref, x1_ref, m_ref, s_ref, r_ref, d_ref, wg_ref, bg_ref, o_ref):
    rows = s_ref.shape[1]
    wd = 2 * LANES
    xt = [_row_tokens(x0_ref, x1_ref, t, rows) for t in range(S5_T)]
    acc = _dot(s_ref[0].astype(BF16), r_ref[...])
    for t in range(S5_T):
        acc += _dot(xt[t].astype(BF16), m_ref[t * wd:(t + 1) * wd, :])
    for t in range(S5_T):
        g = jax.nn.gelu(acc[:, t * wd:(t + 1) * wd] + d_ref[...] * xt[t], approximate=True)
        y = g * jax.nn.sigmoid(_dot(g.astype(BF16), wg_ref[...]) + bg_ref[...])
        o_ref[0, 0, pl.ds(t, rows, stride=S5_T), :] = y[:, :LANES]
        o_ref[1, 0, pl.ds(t, rows, stride=S5_T), :] = y[:, LANES:]


def s5_readout_glu(x, m_intra, xs, r_out, d_skip, w_glu, b_glu):
    bn, L, wd = x.shape
    assert wd == 2 * LANES
    r = L // S5_T
    tm = min(r, 256)
    one = pl.Buffered(1)
    half = lambda h: pl.BlockSpec((1, tm * S5_T, LANES), lambda b, i: (b, i, h))
    row = pl.BlockSpec((1, wd), lambda b, i: (0, 0))
    return pl.pallas_call(
        _s5out_kernel,
        grid=(bn, r // tm),
        in_specs=[half(0), half(1), pl.BlockSpec(m_intra.shape, lambda b, i: (0, 0), pipeline_mode=one),
                  pl.BlockSpec((1, tm, xs.shape[-1]), lambda b, i: (b, i, 0)),
                  pl.BlockSpec(r_out.shape, lambda b, i: (0, 0), pipeline_mode=one),
                  row, pl.BlockSpec((wd, wd), lambda b, i: (0, 0)), row],
        out_specs=pl.BlockSpec((2, 1, tm * S5_T, LANES), lambda b, i: (0, b, i, 0)),
        out_shape=jax.ShapeDtypeStruct((2, bn, L, LANES), F32),
        compiler_params=_cparams(("arbitrary", "arbitrary")),
        name="s5_readout_glu",
    )(x, x, m_intra, xs, r_out, d_skip.reshape(1, wd), w_glu, b_glu.reshape(1, wd))


def _s5scan_kernel(r_ref, a_ref, h0_ref, x_ref, fin_ref):
    nk = r_ref.shape[1]
    sw = S5_GROUPS * S5_N
    ar_f, ai_f = a_ref[:, 0:sw], a_ref[:, sw:2 * sw]
    ar_b, ai_b = a_ref[:, 2 * sw:3 * sw], a_ref[:, 3 * sw:4 * sw]
    init = tuple(h0_ref[0, :, j * sw:(j + 1) * sw] for j in range(4))

    def body(i, carry):
        fr, fi, br, bi = carry
        k = nk - 1 - i
        x_ref[0, pl.ds(i, 1), 0:sw] = fr
        x_ref[0, pl.ds(i, 1), sw:2 * sw] = fi
        x_ref[0, pl.ds(k, 1), 2 * sw:3 * sw] = br
        x_ref[0, pl.ds(k, 1), 3 * sw:4 * sw] = bi
        rfr = r_ref[0, pl.ds(i, 1), 0:sw]
        rfi = r_ref[0, pl.ds(i, 1), sw:2 * sw]
        rbr = r_ref[0, pl.ds(k, 1), 2 * sw:3 * sw]
        rbi = r_ref[0, pl.ds(k, 1), 3 * sw:4 * sw]
        return (ar_f * fr - ai_f * fi + rfr, ar_f * fi + ai_f * fr + rfi,
                ar_b * br - ai_b * bi + rbr, ar_b * bi + ai_b * br + rbi)
    fin = lax.fori_loop(0, nk, body, init)
    for j in range(4):
        fin_ref[0, :, j * sw:(j + 1) * sw] = fin[j]


def s5_row_scan(r, a8v, h0):
    bn, nk, w = r.shape
    return pl.pallas_call(
        _s5scan_kernel,
        grid=(bn,),
        in_specs=[pl.BlockSpec((1, nk, w), lambda b: (b, 0, 0)),
                  pl.BlockSpec((1, w), lambda b: (0, 0)),
                  pl.BlockSpec((1, 1, w), lambda b: (b, 0, 0))],
        out_specs=[pl.BlockSpec((1, nk, w), lambda b: (b, 0, 0)),
                   pl.BlockSpec((1, 1, w), lambda b: (b, 0, 0))],
        out_shape=[jax.ShapeDtypeStruct((bn, nk, w), F32), jax.ShapeDtypeStruct((bn, 1, w), F32)],
        compiler_params=_cparams(("arbitrary",)),
        name="s5_row_scan",
    )(r, a8v, h0)


def s5_mix(p_s5, mats, d_skip, w_glu, b_glu, h0):
    m_intra, r_in, r_out, a8v = mats
    r = s5_state_inputs(p_s5, r_in, tn=1024)
    xs, fin = s5_row_scan(r, a8v, h0)
    return s5_readout_glu(p_s5, m_intra, xs, r_out, d_skip, w_glu, b_glu), fin


def _qk_kernel(x_ref, wq_ref, wk_ref, q_ref, k_ref):
    for h in range(ML_HEADS):
        sl = slice(h * ML_HD, (h + 1) * ML_HD)
        xb = x_ref[0, :, sl].astype(BF16)
        q_ref[0, :, sl] = _dot(xb, wq_ref[h]).astype(BF16)
        k_ref[0, :, sl] = (_dot(xb, wk_ref[h]) * (ML_HD ** -0.5)).astype(BF16)


def mlstm_qk(xc, wq, wk):
    bn, L, w = xc.shape
    tm = _row_tile(L)
    tok = pl.BlockSpec((1, tm, w), lambda b, i: (b, i, 0))
    wsp = pl.BlockSpec(wq.shape, lambda b, i: (0, 0, 0))
    return pl.pallas_call(
        _qk_kernel,
        grid=(bn, L // tm),
        in_specs=[tok, wsp, wsp],
        out_specs=[tok, tok],
        out_shape=[jax.ShapeDtypeStruct((bn, L, w), BF16)] * 2,
        compiler_params=_cparams(("arbitrary", "arbitrary")),
        name="mlstm_qk",
    )(xc, wq, wk)


def _mlstm_kernel(bias_ref, q_ref, k_ref, v_ref, g_ref, c0_ref, n0_ref, m0_ref,
                  h_ref, cf_ref, nf_ref, mf_ref, c_s, n_s, m_s, lf_s, b_s):
    hd = pl.program_id(0)
    bn = q_ref.shape[0]
    nc = g_ref.shape[2]
    t = ML_CHUNK
    chains = [(b, d) for b in range(bn) for d in range(2)]
    for i, (b, d) in enumerate(chains):
        c_s[i] = c0_ref[b, d, 0]
        n_s[i] = n0_ref[b, d, 0]
        m_s[i] = m0_ref[b, d, 0]
    row = lax.broadcasted_iota(jnp.int32, (t, t), 0)
    col = lax.broadcasted_iota(jnp.int32, (t, t), 1)
    tri = (col <= row, col >= row)
    cum = ((row <= col).astype(F32), (row >= col).astype(F32))
    bias_i = [bias_ref[d * 2 * ML_HEADS + hd] for d in range(2)]
    bias_f = [bias_ref[d * 2 * ML_HEADS + ML_HEADS + hd] for d in range(2)]
    for i, (b, d) in enumerate(chains):
        lf_all = jax.nn.log_sigmoid(g_ref[b, d * 2 * ML_HEADS + ML_HEADS + hd] + bias_f[d])
        lf_s[i] = lf_all
        b_s[i] = jnp.dot(lf_all, cum[d], precision=HIGHEST, preferred_element_type=F32)

    def body(j, carry):
        ids = range(len(chains))
        cidx = [j if d == 0 else nc - 1 - j for _, d in chains]
        r0 = [pl.multiple_of(c * t, t) for c in cidx]
        q = [q_ref[b, pl.ds(r0[i], t), :] for i, (b, d) in enumerate(chains)]
        k = [k_ref[b, pl.ds(r0[i], t), :] for i, (b, d) in enumerate(chains)]
        v = [v_ref[b, pl.ds(r0[i], t), :] for i, (b, d) in enumerate(chains)]
        li = [g_ref[b, d * 2 * ML_HEADS + hd, pl.ds(cidx[i], 1), :] + bias_i[d] for i, (b, d) in enumerate(chains)]
        lf = [lf_s[i, pl.ds(cidx[i], 1), :] for i in ids]
        b_row = [b_s[i, pl.ds(cidx[i], 1), :] for i in ids]
        b_col = [jnp.sum(jnp.where(tri[d], lf[i], 0.0), axis=-1, keepdims=True) for i, (b, d) in enumerate(chains)]
        g = [jnp.sum(lf[i], axis=-1, keepdims=True) for i in ids]
        a_row = [g[i] - b_row[i] + li[i] for i in ids]
        m_loc = [jnp.max(a_row[i], axis=-1, keepdims=True) for i in ids]
        w_row = [jnp.exp(a_row[i] - m_loc[i]) for i in ids]
        c_prev = [c_s[i] for i in ids]
        n_prev = [n_s[i] for i in ids]
        m_prev = [m_s[i][:, 0:1] for i in ids]
        qkc = [_dot_nt(q[i], jnp.concatenate([k[i], c_prev[i].astype(BF16)], axis=0)) for i in ids]
        lhs = [jnp.concatenate([v[i].astype(F32).T * w_row[i], jnp.broadcast_to(w_row[i], (2 * SUBLANES, t))], axis=0)
               for i in ids]
        cn = [_dot(lhs[i].astype(BF16), k[i]) for i in ids]
        dmat = [jnp.where(tri[d], b_col[i] - b_row[i] + li[i], -jnp.inf) for i, (b, d) in enumerate(chains)]
        inter = [b_col[i] + m_prev[i] for i in ids]
        m_t = [jnp.maximum(jnp.max(dmat[i], axis=-1, keepdims=True), inter[i]) for i in ids]
        s = [qkc[i][:, :t] * jnp.exp(dmat[i] - m_t[i]) for i in ids]
        w_inter = [jnp.exp(inter[i] - m_t[i]) for i in ids]
        sv = [_dot(s[i].astype(BF16), v[i]) for i in ids]
        nq = [jnp.sum(s[i], axis=-1, keepdims=True)
              + w_inter[i] * jnp.sum(q[i].astype(F32) * n_prev[i], axis=-1, keepdims=True) for i in ids]
        for i, (b, d) in enumerate(chains):
            num = sv[i] + w_inter[i] * qkc[i][:, t:]
            h_ref[d, b, pl.ds(r0[i], t), :] = num / jnp.maximum(jnp.abs(nq[i]), jnp.exp(-m_t[i]))
        for i in ids:
            m_new = jnp.maximum(g[i] + m_prev[i], m_loc[i])
            dec = jnp.exp(g[i] + m_prev[i] - m_new)
            grow = jnp.exp(m_loc[i] - m_new)
            c_s[i] = dec * c_prev[i] + grow * cn[i][:ML_HD]
            n_s[i] = dec * n_prev[i] + grow * cn[i][ML_HD:ML_HD + 1]
            m_s[i] = jnp.broadcast_to(m_new, (1, LANES))
        return carry
    lax.fori_loop(0, nc, body, 0, unroll=2)
    for i, (b, d) in enumerate(chains):
        cf_ref[b, d, 0] = c_s[i]
        nf_ref[b, d, 0] = n_s[i]
        mf_ref[b, d, 0] = m_s[i]


def mlstm_scan(q, k, v, gates_t, gate_bias, state):
    bn, L, w = q.shape
    c0, n0, m0 = state
    hsp = pl.BlockSpec((bn, L, ML_HD), lambda h: (0, 0, h), pipeline_mode=pl.Buffered(1))
    st = lambda a: pl.BlockSpec((bn, 2, 1) + a.shape[3:], lambda h: (0, 0, h, 0, 0))
    return pl.pallas_call(
        _mlstm_kernel,
        grid=(ML_HEADS,),
        in_specs=[pl.BlockSpec(memory_space=pltpu.SMEM), hsp, hsp, hsp,
                  pl.BlockSpec(gates_t.shape, lambda h: (0, 0, 0, 0)),
                  st(c0), st(n0), st(m0)],
        out_specs=[pl.BlockSpec((2, bn, L, ML_HD), lambda h: (0, 0, 0, h)), st(c0), st(n0), st(m0)],
        out_shape=[jax.ShapeDtypeStruct((2, bn, L, w), F32),
                   jax.ShapeDtypeStruct(c0.shape, F32), jax.ShapeDtypeStruct(n0.shape, F32),
                   jax.ShapeDtypeStruct(m0.shape, F32)],
        scratch_shapes=[pltpu.VMEM((2 * bn, ML_HD, ML_HD), F32), pltpu.VMEM((2 * bn, 1, ML_HD), F32),
                        pltpu.VMEM((2 * bn, 1, LANES), F32),
                        pltpu.VMEM((2 * bn,) + gates_t.shape[2:], F32),
                        pltpu.VMEM((2 * bn,) + gates_t.shape[2:], F32)],
        compiler_params=_cparams(("arbitrary",)),
        name="mlstm_scan",
    )(gate_bias, q, k, v, gates_t, c0, n0, m0)


def _mlout_kernel(hf_ref, hb_ref, o_ref, xc_ref, gain_ref, skip_ref, y_ref):
    h = hf_ref[0, 0] + hb_ref[0, 0]
    parts = []
    for hd in range(ML_HEADS):
        sl = slice(hd * ML_HD, (hd + 1) * ML_HD)
        hh = h[:, sl]
        parts.append(hh * lax.rsqrt(jnp.mean(hh * hh, axis=-1, keepdims=True) + EPS) * gain_ref[:, sl])
    hn = jnp.concatenate(parts, axis=-1)
    y_ref[0] = jax.nn.sigmoid(o_ref[0]) * (hn + skip_ref[...] * xc_ref[0])


def mlstm_output(h2, p_o, xc, gain, skip):
    _, bn, L, w = h2.shape
    tm = _row_tile(L)
    tok = pl.BlockSpec((1, tm, w), lambda b, i: (b, i, 0))
    row = pl.BlockSpec((1, w), lambda b, i: (0, 0))
    return pl.pallas_call(
        _mlout_kernel,
        grid=(bn, L // tm),
        in_specs=[pl.BlockSpec((1, 1, tm, w), lambda b, i: (0, b, i, 0)),
                  pl.BlockSpec((1, 1, tm, w), lambda b, i: (1, b, i, 0)), tok, tok, row, row],
        out_specs=tok,
        out_shape=jax.ShapeDtypeStruct((bn, L, w), F32),
        compiler_params=_cparams(("arbitrary", "arbitrary")),
        name="mlstm_output",
    )(h2, h2, p_o, xc, gain.reshape(1, w), skip.reshape(1, w))


def mlstm_mix(p_x, p_v, p_o, p_g, conv_w9, width, rows, wq, wk, gate_bias, skip, gain, state):
    bn, L, _ = p_x.shape
    xc = depthwise_conv(p_x, conv_w9, width=width, rows=rows, act=_silu)
    q, k = mlstm_qk(xc, wq, wk)
    gates_t = jnp.swapaxes(p_g[..., :ML_GATES], 1, 2).reshape(bn, ML_GATES, L // ML_CHUNK, ML_CHUNK)
    h2, cf, nf, mf = mlstm_scan(q, k, p_v, gates_t, gate_bias.reshape(ML_GATES), state)
    return mlstm_output(h2, p_o, xc, gain, skip), (cf, nf, mf)


PROJ_SPLITS = ((0, 3 * HY_W), (3 * HY_W, 3 * HY_W + S5_W),
               (3 * HY_W + S5_W, 3 * HY_W + S5_W + ML_W),
               (3 * HY_W + S5_W + ML_W, 3 * HY_W + S5_W + 2 * ML_W),
               (3 * HY_W + S5_W + 2 * ML_W, 3 * HY_W + S5_W + 3 * ML_W),
               (3 * HY_W + S5_W + 3 * ML_W, 3 * HY_W + S5_W + 3 * ML_W + LANES))
PROJ_DTYPES = (F32, F32, F32, BF16, F32, F32)


def kernel(x, c, ctx, c_ctx, w_mod, b_mod, g_pre_mix, g_post_mix, g_pre_mlp, g_post_mlp, w_in, w_out, hy_conv, hy_w1, hy_b1, hy_w2, hy_b2, hy_w3, hy_freq, hy_decay, hy_bias, s5_a_re, s5_a_im, s5_log_dt, s5_b_re, s5_b_im, s5_c_re, s5_c_im, s5_d, s5_w_glu, s5_b_glu, ml_conv, ml_wq, ml_wk, ml_gate_bias, ml_skip, ml_norm_gain, w_mlp1, w_mlp2):
    bsz, seq_len, d = x.shape
    ctx_len = ctx.shape[1]
    depth = w_mod.shape[0]

    cc = jnp.zeros((SUBLANES, d), F32).at[:bsz].set(c).at[bsz].set(c_ctx)
    mods = mod_vectors(cc, w_mod, b_mod)

    proj_w = w_in.shape[-1]
    w_in_b = jnp.pad(w_in, ((0, 0), (0, 0), (0, PROJ_SPLITS[-1][1] - proj_w))).astype(BF16)
    w_out_b, w1_b, w2_b = w_out.astype(BF16), w_mlp1.astype(BF16), w_mlp2.astype(BF16)
    wq_b, wk_b, wglu_b = ml_wq.astype(BF16), ml_wk.astype(BF16), s5_w_glu.astype(BF16)

    s5_zero = jnp.zeros((bsz, 1, 4 * S5_GROUPS * S5_N), F32)
    ml_zero = (jnp.zeros((bsz, 2, ML_HEADS, ML_HD, ML_HD), F32),
               jnp.zeros((bsz, 2, ML_HEADS, 1, ML_HD), F32),
               jnp.zeros((bsz, 2, ML_HEADS, 1, LANES), F32))

    for l in range(depth):
        mx = [mods[l, :bsz, j * d:(j + 1) * d][:, None, :] for j in range(6)]
        mc = [jnp.broadcast_to(mods[l, bsz, j * d:(j + 1) * d][None, None, :], (bsz, 1, d)) for j in range(6)]
        filt_x = hyena_filters(seq_len, hy_w1[l], hy_b1[l], hy_w2[l], hy_b2[l], hy_w3[l], hy_freq[l], hy_decay[l])
        filt_c = hyena_filters(ctx_len, hy_w1[l], hy_b1[l], hy_w2[l], hy_b2[l], hy_w3[l], hy_freq[l], hy_decay[l])
        s5_mats = s5_matrices(s5_a_re[l], s5_a_im[l], s5_log_dt[l], s5_b_re[l], s5_b_im[l], s5_c_re[l], s5_c_im[l])
        conv2d_w = ml_conv[l].reshape(9, ML_W)

        def mix(h_in, mod, L, fft_len, filt, width, rows, s5_h0, ml_state):
            p_hy, p_s5, p_mx, p_mv, p_mo, p_mg = in_projection(h_in, g_pre_mix[l], mod[1], mod[0], w_in_b[l],
                                                               PROJ_SPLITS, PROJ_DTYPES)
            y_hy = hyena_mix(p_hy, hy_conv[l], filt, hy_bias[l], fft_len)
            y_s5, s5_fin = s5_mix(p_s5, s5_mats, s5_d[l], wglu_b[l], s5_b_glu[l], s5_h0)
            y_ml, ml_fin = mlstm_mix(p_mx, p_mv, p_mo, p_mg, conv2d_w, width, rows, wq_b[l], wk_b[l],
                                     ml_gate_bias[l], ml_skip[l], ml_norm_gain[l], ml_state)
            return (y_hy, y_s5, y_ml), s5_fin, ml_fin

        y_c, s5_state, ml_state = mix(ctx, mc, ctx_len, CTX_FFT_LEN, filt_c, ctx_len, (0,), s5_zero, ml_zero)
        y_x, _, _ = mix(x, mx, seq_len, seq_len, filt_x, GRID_W, (-1, 0, 1), s5_state, ml_state)
        x = out_projection(*y_x, w_out_b[l], g_post_mix[l], mx[2], x)
        x = mlp_block(x, g_pre_mlp[l], mx[4], mx[3], w1_b[l], w2_b[l], g_post_mlp[l], mx[5])
        if l < depth - 1:
            ctx = out_projection(*y_c, w_out_b[l], g_post_mix[l], mc[2], ctx)
            ctx = mlp_block(ctx, g_pre_mlp[l], mc[4], mc[3], w1_b[l], w2_b[l], g_post_mlp[l], mc[5])
    return x
```

```python
import functools
import math

import numpy as np
import jax
import jax.numpy as jnp
from jax import lax
from jax.experimental import pallas as pl
from jax.experimental.pallas import tpu as pltpu

F32 = jnp.float32
BF16 = jnp.bfloat16
EPS = 1e-6
HIGHEST = lax.Precision.HIGHEST

V7X_VMEM_BYTES = 64 * 1024 * 1024
VMEM_LIMIT = V7X_VMEM_BYTES - 8 * 1024 * 1024
LANES = 128
SUBLANES = 8
FFT_N2 = 64
FFT_UNROLL_A = 4
FFT_UNROLL_B = 8

HY_W = 256
HY_BANDS = 8
HY_EMB = 1 + 2 * HY_BANDS
HY_FF = 64
S5_W = 256
S5_CH = 16
S5_GROUPS = S5_W // S5_CH
S5_N = 64
S5_T = 8
ML_HEADS = 4
ML_HD = 128
ML_W = ML_HEADS * ML_HD
ML_CHUNK = 64
ML_GATES = 4 * ML_HEADS
GRID_W = 64
CTX_FFT_LEN = 512


def _cparams(sem):
    return pltpu.CompilerParams(dimension_semantics=sem, vmem_limit_bytes=VMEM_LIMIT)


def _dot(a, b):
    return jnp.dot(a, b, preferred_element_type=F32)


def _dot_nt(a, b):
    return lax.dot_general(a, b, (((1,), (1,)), ((), ())), preferred_element_type=F32)


def _rms(x, g):
    return x * lax.rsqrt(jnp.mean(x * x, axis=-1, keepdims=True) + EPS) * g


def _silu(x):
    return x * jax.nn.sigmoid(x)


def _mod_kernel(c_ref, w_ref, b_ref, o_ref):
    s = _silu(c_ref[...]).astype(BF16)
    o_ref[0] = _dot(s, w_ref[0].astype(BF16)) + b_ref[0]


def mod_vectors(cc, w_mod, b_mod):
    depth, d, n = w_mod.shape
    r = cc.shape[0]
    tn = 1536
    return pl.pallas_call(
        _mod_kernel,
        grid=(depth, n // tn),
        in_specs=[pl.BlockSpec((r, d), lambda l, j: (0, 0)),
                  pl.BlockSpec((1, d, tn), lambda l, j: (l, 0, j)),
                  pl.BlockSpec((1, 1, tn), lambda l, j: (l, 0, j))],
        out_specs=pl.BlockSpec((1, r, tn), lambda l, j: (l, 0, j)),
        out_shape=jax.ShapeDtypeStruct((depth, r, n), F32),
        compiler_params=_cparams(("arbitrary", "arbitrary")),
        name="mod_vectors",
    )(cc, w_mod, b_mod.reshape(depth, 1, n))


def _row_tile(L):
    return min(L, 512)


def _inproj_kernel(x_ref, g_ref, sc_ref, sh_ref, w_ref, *o_refs, splits):
    h = _rms(x_ref[0], g_ref[...]) * (1.0 + sc_ref[0]) + sh_ref[0]
    hb = h.astype(BF16)
    for o_ref, (a, b) in zip(o_refs, splits):
        o_ref[0] = _dot(hb, w_ref[:, a:b]).astype(o_ref.dtype)


def in_projection(x, g, sc, sh, w, splits, dtypes):
    bn, L, d = x.shape
    tm = _row_tile(L)
    vec = pl.BlockSpec((1, 1, d), lambda b, i: (b, 0, 0))
    return pl.pallas_call(
        functools.partial(_inproj_kernel, splits=splits),
        grid=(bn, L // tm),
        in_specs=[pl.BlockSpec((1, tm, d), lambda b, i: (b, i, 0)),
                  pl.BlockSpec((1, d), lambda b, i: (0, 0)), vec, vec,
                  pl.BlockSpec(w.shape, lambda b, i: (0, 0))],
        out_specs=[pl.BlockSpec((1, tm, b_ - a_), lambda b, i: (b, i, 0)) for a_, b_ in splits],
        out_shape=[jax.ShapeDtypeStruct((bn, L, b_ - a_), dt) for (a_, b_), dt in zip(splits, dtypes)],
        compiler_params=_cparams(("arbitrary", "arbitrary")),
        name="in_projection",
    )(x, g.reshape(1, d), sc, sh, w)


def _outproj_kernel(yh_ref, ys0_ref, ys1_ref, ym_ref, w_ref, g_ref, gt_ref, x_ref, o_ref):
    a = yh_ref.shape[-1]
    acc = _dot(yh_ref[0].astype(BF16), w_ref[0:a])
    acc += _dot(ys0_ref[0, 0].astype(BF16), w_ref[a:a + LANES])
    acc += _dot(ys1_ref[0, 0].astype(BF16), w_ref[a + LANES:a + 2 * LANES])
    acc += _dot(ym_ref[0].astype(BF16), w_ref[a + 2 * LANES:])
    o_ref[0] = x_ref[0] + gt_ref[0] * _rms(acc, g_ref[...])


def out_projection(y_hy, y_s5, y_ml, w, g, gate, x):
    bn, L, d = x.shape
    tm = _row_tile(L)
    tok = lambda wd: pl.BlockSpec((1, tm, wd), lambda b, i: (b, i, 0))
    half = lambda h: pl.BlockSpec((1, 1, tm, LANES), lambda b, i: (h, b, i, 0))
    return pl.pallas_call(
        _outproj_kernel,
        grid=(bn, L // tm),
        in_specs=[tok(y_hy.shape[-1]), half(0), half(1), tok(y_ml.shape[-1]),
                  pl.BlockSpec(w.shape, lambda b, i: (0, 0)),
                  pl.BlockSpec((1, d), lambda b, i: (0, 0)),
                  pl.BlockSpec((1, 1, d), lambda b, i: (b, 0, 0)), tok(d)],
        out_specs=tok(d),
        out_shape=jax.ShapeDtypeStruct((bn, L, d), F32),
        compiler_params=_cparams(("arbitrary", "arbitrary")),
        name="out_projection",
    )(y_hy, y_s5, y_s5, y_ml, w, g.reshape(1, d), gate, x)


def _mlp_kernel(x_ref, g1_ref, sc_ref, sh_ref, w1_ref, w2_ref, g2_ref, gt_ref, o_ref, h_ref, acc_ref):
    j = pl.program_id(2)

    @pl.when(j == 0)
    def _():
        h = _rms(x_ref[0], g1_ref[...]) * (1.0 + sc_ref[0]) + sh_ref[0]
        h_ref[...] = h.astype(BF16)
        acc_ref[...] = jnp.zeros_like(acc_ref)

    a = jnp.maximum(_dot(h_ref[...], w1_ref[...]), 0.0)
    acc_ref[...] += _dot((a * a).astype(BF16), w2_ref[...])

    @pl.when(j == pl.num_programs(2) - 1)
    def _():
        o_ref[0] = x_ref[0] + gt_ref[0] * _rms(acc_ref[...], g2_ref[...])


def mlp_block(x, g_pre, sc, sh, w1, w2, g_post, gate):
    bn, L, d = x.shape
    hid = w1.shape[1]
    tm = _row_tile(L)
    th = 1024
    tok = pl.BlockSpec((1, tm, d), lambda b, i, j: (b, i, 0))
    vec = pl.BlockSpec((1, 1, d), lambda b, i, j: (b, 0, 0))
    gain = pl.BlockSpec((1, d), lambda b, i, j: (0, 0))
    return pl.pallas_call(
        _mlp_kernel,
        grid=(bn, L // tm, hid // th),
        in_specs=[tok, gain, vec, vec,
                  pl.BlockSpec((d, th), lambda b, i, j: (0, j)),
                  pl.BlockSpec((th, d), lambda b, i, j: (j, 0)),
                  gain, vec],
        out_specs=tok,
        out_shape=jax.ShapeDtypeStruct((bn, L, d), F32),
        scratch_shapes=[pltpu.VMEM((tm, d), BF16), pltpu.VMEM((tm, d), F32)],
        compiler_params=_cparams(("arbitrary", "arbitrary", "arbitrary")),
        name="mlp_block",
    )(x, g_pre.reshape(1, d), sc, sh, w1, w2, g_post.reshape(1, d), gate)


def _dwconv_kernel(x_ref, w_ref, o_ref, pad_ref, *, width, taps, act, pad):
    L = x_ref.shape[1]
    ch = x_ref.shape[2]
    pad_ref[0:pad, :] = jnp.zeros((pad, ch), F32)
    pad_ref[pad + L:pad + L + pad, :] = jnp.zeros((pad, ch), F32)
    pad_ref[pad:pad + L, :] = x_ref[0]
    tr = min(L, 256)
    for r0 in range(0, L, tr):
        col = (lax.broadcasted_iota(jnp.int32, (tr, ch), 0) + r0) & (width - 1)
        acc = jnp.zeros((tr, ch), F32)
        for dr, dc in taps:
            start = pad + r0 + dr * width + dc
            v = pad_ref[start:start + tr, :]
            if dc == -1:
                v = jnp.where(col >= 1, v, 0.0)
            elif dc == 1:
                v = jnp.where(col <= width - 2, v, 0.0)
            acc = acc + w_ref[(dr + 1) * 3 + (dc + 1):(dr + 1) * 3 + (dc + 2), :] * v
        o_ref[0, r0:r0 + tr, :] = act(acc)


def depthwise_conv(x, w9, width, rows, act=None):
    bn, L, ch = x.shape
    assert width & (width - 1) == 0
    taps = tuple((dr, dc) for dr in rows for dc in (-1, 0, 1))
    pad = -(-(width + 1) // SUBLANES) * SUBLANES if len(rows) > 1 else SUBLANES
    act = act or (lambda a: a)
    cs = LANES
    return pl.pallas_call(
        functools.partial(_dwconv_kernel, width=width, taps=taps, act=act, pad=pad),
        grid=(bn, ch // cs),
        in_specs=[pl.BlockSpec((1, L, cs), lambda b, c: (b, 0, c)),
                  pl.BlockSpec((9, cs), lambda b, c: (0, c))],
        out_specs=pl.BlockSpec((1, L, cs), lambda b, c: (b, 0, c)),
        out_shape=jax.ShapeDtypeStruct((bn, L, ch), F32),
        scratch_shapes=[pltpu.VMEM((L + 2 * pad, cs), F32)],
        compiler_params=_cparams(("arbitrary", "arbitrary")),
        name="depthwise_conv",
    )(x, w9)


def _hyfilt_kernel(w1_ref, b1_ref, w2_ref, b2_ref, w3_ref, fr_ref, dec_ref, o_ref, *, L):
    tr = min(L, 512)
    wd = o_ref.shape[1]
    half = wd // 2
    fr = fr_ref[...]
    ssq = jnp.zeros((1, wd), F32)
    for r0 in range(0, L, tr):
        t = (lax.broadcasted_iota(jnp.int32, (tr, 32), 0) + r0).astype(F32) / L
        lane = lax.broadcasted_iota(jnp.int32, (tr, 32), 1)
        band = jnp.where(lane <= HY_BANDS, lane, lane - HY_BANDS).astype(F32)
        ang = 2.0 * math.pi * t * band
        feat = jnp.where(lane == 0, t, jnp.where(lane <= HY_BANDS, jnp.cos(ang),
                                                 jnp.where(lane <= 2 * HY_BANDS, jnp.sin(ang), 0.0)))
        hdn = jnp.sin(fr * (jnp.dot(feat, w1_ref[...], precision=HIGHEST, preferred_element_type=F32) + b1_ref[...]))
        hdn = jnp.sin(fr * (jnp.dot(hdn, w2_ref[...], precision=HIGHEST, preferred_element_type=F32) + b2_ref[...]))
        filt = _dot(hdn.astype(BF16), w3_ref[...].astype(BF16))
        filt = filt * jnp.exp(-t[:, 0:1] * dec_ref[...])
        ssq = ssq + jnp.sum(filt * filt, axis=0, keepdims=True)
        o_ref[r0:r0 + tr, :] = filt
    tot = ssq[:, :half] + ssq[:, half:]
    scale = lax.rsqrt(tot + EPS)
    scale = jnp.concatenate([scale, scale], axis=1)
    for r0 in range(0, L, tr):
        o_ref[r0:r0 + tr, :] = o_ref[r0:r0 + tr, :] * scale


def hyena_filters(L, w1, b1, w2, b2, w3, freq, decay):
    order = decay.shape[0]
    wd = 2 * HY_W
    w1p = jnp.zeros((32, HY_FF), F32).at[:HY_EMB].set(w1)
    full = lambda a: pl.BlockSpec(a.shape, lambda o: (0,) * a.ndim)
    args = (w1p, b1.reshape(1, HY_FF), w2, b2.reshape(1, HY_FF))
    return pl.pallas_call(
        functools.partial(_hyfilt_kernel, L=L),
        grid=(order,),
        in_specs=[full(a) for a in args] + [pl.BlockSpec((HY_FF, wd), lambda o: (0, o)),
                                            pl.BlockSpec((1, HY_FF), lambda o: (0, 0)),
                                            pl.BlockSpec((1, wd), lambda o: (0, o))],
        out_specs=pl.BlockSpec((L, wd), lambda o: (0, o)),
        out_shape=jax.ShapeDtypeStruct((L, order * wd), F32),
        compiler_params=_cparams(("arbitrary",)),
        name="hyena_filters",
    )(*args, w3, freq.reshape(1, HY_FF), decay.reshape(1, order * wd))


@functools.lru_cache(maxsize=None)
def _dft_consts(L):
    n = 2 * L
    n2c = FFT_N2
    n1c = n // n2c
    n1h = n1c // 2
    n1 = np.arange(n1h)[None, None, :]
    k1 = np.arange(n1c)[None, :, None]
    n2 = np.arange(n2c)[:, None, None]
    ang = -2.0 * np.pi * (k1 * n1 / n1c + n2 * k1 / n)
    fr, fi = np.cos(ang), np.sin(ang)
    fa = np.concatenate([np.concatenate([fr, -fi], axis=2),
                         np.concatenate([fi, fr], axis=2)], axis=1)
    frt = np.transpose(fr, (0, 2, 1)) / n
    fit = -np.transpose(fi, (0, 2, 1)) / n
    fai = np.concatenate([np.concatenate([frt, -fit], axis=2),
                          np.concatenate([fit, frt], axis=2)], axis=1)
    k2 = np.arange(n2c)[:, None]
    m2 = np.arange(n2c)[None, :]
    angb = -2.0 * np.pi * k2 * m2 / n2c
    gr, gi = np.cos(angb), np.sin(angb)
    gb = np.block([[gr, -gi], [gi, gr]])
    gbi = np.block([[gr, gi], [-gi, gr]])
    return (fa.astype(np.float32), gb.astype(np.float32), gbi.astype(np.float32), fai.astype(np.float32))


def _fft_stage_a(load_rhs, fa_ref, s_ref, n1c):
    def body(n2, carry):
        rhs = load_rhs(n2).astype(BF16)
        res = _dot(fa_ref[n2], rhs)
        s_ref[pl.ds(pl.multiple_of(n2 * 2 * n1c, 2 * n1c), 2 * n1c), :] = res
        return carry
    lax.fori_loop(0, FFT_N2, body, 0, unroll=FFT_UNROLL_A)


def _fft_load_k1(s_ref, k1, n1c):
    xr = s_ref[pl.ds(k1, FFT_N2, stride=2 * n1c), :]
    xi = s_ref[pl.ds(n1c + k1, FFT_N2, stride=2 * n1c), :]
    return jnp.concatenate([xr, xi], axis=0)


def _fftconv_kernel(z_ref, gate_ref, bias_ref, h_ref, fa_ref, gb_ref, gbi_ref, fai_ref, o_ref, s_ref, *, L):
    n1c = L // 32
    n1h = n1c // 2
    half = FFT_N2

    def load_rhs(n2):
        za = z_ref[0, 0, pl.ds(n2, n1h, stride=FFT_N2), :]
        zb = z_ref[0, 1, pl.ds(n2, n1h, stride=FFT_N2), :]
        return jnp.concatenate([za, zb], axis=0)
    _fft_stage_a(load_rhs, fa_ref, s_ref, n1c)

    def stage_b(k1, carry):
        x = _dot(gb_ref[...], _fft_load_k1(s_ref, k1, n1c).astype(BF16))
        xr, xi = x[:half], x[half:]
        h = h_ref[k1]
        hr, hi = h[:half], h[half:]
        y = jnp.concatenate([xr * hr - xi * hi, xr * hi + xi * hr], axis=0)
        bp = _dot(gbi_ref[...], y.astype(BF16))
        s_ref[pl.ds(k1, FFT_N2, stride=2 * n1c), :] = bp[:half]
        s_ref[pl.ds(n1c + k1, FFT_N2, stride=2 * n1c), :] = bp[half:]
        return carry
    lax.fori_loop(0, n1c, stage_b, 0, unroll=FFT_UNROLL_B)

    bias = bias_ref[...]

    def stage_a_inv(n2, carry):
        rhs = s_ref[pl.ds(pl.multiple_of(n2 * 2 * n1c, 2 * n1c), 2 * n1c), :].astype(BF16)
        res = _dot(fai_ref[n2], rhs)
        for p in range(2):
            rows = pl.ds(n2, n1h, stride=FFT_N2)
            zin = z_ref[0, p, rows, :]
            o_ref[0, p, rows, :] = gate_ref[0, p, rows, :] * (res[p * n1h:(p + 1) * n1h] + bias * zin)
        return carry
    lax.fori_loop(0, FFT_N2, stage_a_inv, 0, unroll=FFT_UNROLL_A)


def fft_gated_conv(z, zoff, gate, goff, bias, hspec):
    bsz, L, _ = z.shape
    ch = bias.shape[0]
    n1c = L // 32
    fa, gb, gbi, fai = (jnp.asarray(a, BF16) for a in _dft_consts(L))
    zp = z.reshape(bsz // 2, 2, L, z.shape[-1])
    gp = gate.reshape(bsz // 2, 2, L, gate.shape[-1])
    cs = LANES
    one = pl.Buffered(1)
    const3 = lambda a: pl.BlockSpec(a.shape, lambda c, p: (0, 0, 0), pipeline_mode=one)
    const2 = lambda a: pl.BlockSpec(a.shape, lambda c, p: (0, 0), pipeline_mode=one)
    out = pl.pallas_call(
        functools.partial(_fftconv_kernel, L=L),
        grid=(ch // cs, bsz // 2),
        in_specs=[pl.BlockSpec((1, 2, L, cs), lambda c, p: (p, 0, 0, c + zoff)),
                  pl.BlockSpec((1, 2, L, cs), lambda c, p: (p, 0, 0, c + goff)),
                  pl.BlockSpec((1, cs), lambda c, p: (0, c)),
                  pl.BlockSpec((n1c, 2 * FFT_N2, cs), lambda c, p: (0, 0, c), pipeline_mode=one),
                  const3(fa), const2(gb), const2(gbi), const3(fai)],
        out_specs=pl.BlockSpec((1, 2, L, cs), lambda c, p: (p, 0, 0, c)),
        out_shape=jax.ShapeDtypeStruct((bsz // 2, 2, L, ch), F32),
        scratch_shapes=[pltpu.VMEM((FFT_N2 * 2 * n1c, cs), F32)],
        compiler_params=_cparams(("arbitrary", "arbitrary")),
        name="fft_gated_conv",
    )(zp, gp, bias.reshape(1, ch), hspec, fa, gb, gbi, fai)
    return out.reshape(bsz, L, ch)


def _fftspec_kernel(hf_ref, hb_ref, fa_ref, gb_ref, o_ref, s_ref, *, L):
    n1c = L // 32
    n1h = n1c // 2
    half = FFT_N2
    for d, src in enumerate((hf_ref, hb_ref)):
        def load_rhs(n2, src=src, d=d):
            h = src[pl.ds(n2, n1h, stride=FFT_N2), :]
            if d == 1:
                row = lax.broadcasted_iota(jnp.int32, h.shape, 0)
                h = jnp.where((row == 0) & (n2 == 0), 0.0, h)
            return jnp.concatenate([h, jnp.zeros_like(h)], axis=0)
        _fft_stage_a(load_rhs, fa_ref, s_ref, n1c)

        def stage_b(k1, carry, d=d):
            x = _dot(gb_ref[...], _fft_load_k1(s_ref, k1, n1c).astype(BF16))
            if d == 0:
                o_ref[k1] = x
            else:
                o_ref[k1] = o_ref[k1] + jnp.concatenate([x[:half], -x[half:]], axis=0)
            return carry
        lax.fori_loop(0, n1c, stage_b, 0, unroll=FFT_UNROLL_B)


def fft_filter_spectrum(filt, foff, boff, L):
    if filt.shape[0] < L:
        filt = jnp.pad(filt, ((0, L - filt.shape[0]), (0, 0)))
    n1c = L // 32
    fa, gb, _, _ = (jnp.asarray(a, BF16) for a in _dft_consts(L))
    cs = LANES
    return pl.pallas_call(
        functools.partial(_fftspec_kernel, L=L),
        grid=(HY_W // cs,),
        in_specs=[pl.BlockSpec((L, cs), lambda c: (0, c + foff)), pl.BlockSpec((L, cs), lambda c: (0, c + boff)),
                  pl.BlockSpec(fa.shape, lambda c: (0, 0, 0)), pl.BlockSpec(gb.shape, lambda c: (0, 0))],
        out_specs=pl.BlockSpec((n1c, 2 * FFT_N2, cs), lambda c: (0, 0, c)),
        out_shape=jax.ShapeDtypeStruct((n1c, 2 * FFT_N2, HY_W), F32),
        scratch_shapes=[pltpu.VMEM((FFT_N2 * 2 * n1c, cs), F32)],
        compiler_params=_cparams(("arbitrary",)),
        name="fft_filter_spectrum",
    )(filt, filt, fa, gb)


def hyena_mix(p_hy, conv_w, filt, bias, fft_len):
    bsz, L, _ = p_hy.shape
    w9 = jnp.zeros((9, 3 * HY_W), F32).at[3:6].set(conv_w)
    u = depthwise_conv(p_hy, w9, width=L, rows=(0,))
    if fft_len > L:
        u = jnp.pad(u, ((0, 0), (0, fft_len - L), (0, 0)))
    nb = HY_W // LANES
    z = fft_gated_conv(u, 0, u, nb, bias[0], fft_filter_spectrum(filt, 0, nb, fft_len))
    z = fft_gated_conv(z, 0, u, 2 * nb, bias[1], fft_filter_spectrum(filt, 2 * nb, 3 * nb, fft_len))
    return z[:, :L]


def s5_matrices(a_re, a_im, log_dt, b_re, b_im, c_re, c_im):
    t = S5_T
    sw = S5_GROUPS * S5_N
    dt = jnp.exp(log_dt)[..., None]
    mag = jnp.exp(a_re * dt)
    ar, ai = mag * jnp.cos(a_im * dt), mag * jnp.sin(a_im * dt)
    den = a_re * a_re + a_im * a_im
    qr = ((ar - 1.0) * a_re + ai * a_im) / den
    qi = (ai * a_re - (ar - 1.0) * a_im) / den
    bbr = qr[..., None] * b_re - qi[..., None] * b_im
    bbi = qr[..., None] * b_im + qi[..., None] * b_re
    pr, pi = [jnp.ones_like(ar)], [jnp.zeros_like(ai)]
    for _ in range(t):
        pr, pi = pr + [pr[-1] * ar - pi[-1] * ai], pi + [pr[-1] * ai + pi[-1] * ar]
    pr = [p.reshape(2, sw) for p in pr]
    pi = [p.reshape(2, sw) for p in pi]
    gd_hn = (np.arange(S5_W)[:, None] // S5_CH == np.arange(sw)[None, :] // S5_N).astype(np.float32)
    gn_hc = gd_hn.T
    expand_b = lambda b: jnp.tile(jnp.swapaxes(b, -1, -2).reshape(2, S5_W, S5_N), (1, 1, S5_GROUPS)) * gd_hn
    expand_c = lambda c: jnp.tile(jnp.swapaxes(c, -1, -2).reshape(2, sw, S5_CH), (1, 1, S5_GROUPS)) * gn_hc
    b_r, b_i = expand_b(bbr), expand_b(bbi)
    c_r, c_i = expand_c(c_re), expand_c(c_im)
    col = lambda p, x: p[x][None, :]
    rowv = lambda p, x: p[x][:, None]
    r_in = jnp.concatenate([
        jnp.concatenate([b_r[0] * col(pr[t - 1 - s], 0) - b_i[0] * col(pi[t - 1 - s], 0),
                         b_r[0] * col(pi[t - 1 - s], 0) + b_i[0] * col(pr[t - 1 - s], 0),
                         b_r[1] * col(pr[s], 1) - b_i[1] * col(pi[s], 1),
                         b_r[1] * col(pi[s], 1) + b_i[1] * col(pr[s], 1)], axis=1)
        for s in range(t)], axis=0)

    def scaled_c(x, j):
        return (c_r[x] * rowv(pr[j], x) - c_i[x] * rowv(pi[j], x), c_r[x] * rowv(pi[j], x) + c_i[x] * rowv(pr[j], x))
    blocks = []
    for tt in range(t):
        fr, fi = scaled_c(0, tt + 1)
        br, bi = scaled_c(1, t - tt)
        blocks.append(jnp.concatenate([fr, -fi, br, -bi], axis=0))
    r_out = jnp.concatenate(blocks, axis=1)
    lags = (list(range(t)), list(range(t - 1, -1, -1)))
    taps = []
    for x in range(2):
        sc = [scaled_c(x, j) for j in lags[x]]
        taps.append(s5_taps(b_r[x], b_i[x], jnp.concatenate([p[0] for p in sc], axis=1),
                            jnp.concatenate([p[1] for p in sc], axis=1)))
    w = S5_W
    m_intra = jnp.concatenate([
        jnp.pad(taps[0][:, :(t - s) * w], ((0, 0), (s * w, 0)))
        + jnp.pad(taps[1][:, (t - 1 - s) * w:], ((0, 0), (0, (t - 1 - s) * w)))
        for s in range(t)], axis=0)
    a8v = jnp.concatenate([pr[t][0], pi[t][0], pr[t][1], pi[t][1]])[None, :]
    return m_intra.astype(BF16), r_in.astype(BF16), r_out.astype(BF16), a8v


def _s5taps_kernel(br_ref, bi_ref, cr_ref, ci_ref, o_ref):
    o_ref[...] = (_dot(br_ref[...].astype(BF16), cr_ref[...].astype(BF16))
                  - _dot(bi_ref[...].astype(BF16), ci_ref[...].astype(BF16)))


def s5_taps(b_r, b_i, c_r, c_i):
    m, n = b_r.shape[0], c_r.shape[1]
    tn = 512
    lhs = pl.BlockSpec(b_r.shape, lambda j: (0, 0))
    rhs = pl.BlockSpec((c_r.shape[0], tn), lambda j: (0, j))
    return pl.pallas_call(
        _s5taps_kernel,
        grid=(n // tn,),
        in_specs=[lhs, lhs, rhs, rhs],
        out_specs=pl.BlockSpec((m, tn), lambda j: (0, j)),
        out_shape=jax.ShapeDtypeStruct((m, n), F32),
        compiler_params=_cparams(("arbitrary",)),
        name="s5_taps",
    )(b_r, b_i, c_r, c_i)


def _row_tokens(x0_ref, x1_ref, t, rows):
    sl = pl.ds(t, rows, stride=S5_T)
    return jnp.concatenate([x0_ref[0, sl, :], x1_ref[0, sl, :]], axis=1)


def _s5in_kernel(x0_ref, x1_ref, w_ref, o_ref):
    rows = o_ref.shape[1]
    wd = 2 * LANES
    acc = _dot(_row_tokens(x0_ref, x1_ref, 0, rows).astype(BF16), w_ref[0:wd, :])
    for t in range(1, S5_T):
        acc += _dot(_row_tokens(x0_ref, x1_ref, t, rows).astype(BF16), w_ref[t * wd:(t + 1) * wd, :])
    o_ref[0] = acc


def s5_state_inputs(x, w, tn):
    bn, L, wd = x.shape
    assert wd == 2 * LANES
    r = L // S5_T
    n = w.shape[1]
    tm = min(r, 256)
    half = lambda h: pl.BlockSpec((1, tm * S5_T, LANES), lambda j, b, i: (b, i, h))
    return pl.pallas_call(
        _s5in_kernel,
        grid=(n // tn, bn, r // tm),
        in_specs=[half(0), half(1), pl.BlockSpec((S5_T * wd, tn), lambda j, b, i: (0, j))],
        out_specs=pl.BlockSpec((1, tm, tn), lambda j, b, i: (b, i, j)),
        out_shape=jax.ShapeDtypeStruct((bn, r, n), F32),
        compiler_params=_cparams(("arbitrary", "arbitrary", "arbitrary")),
        name="s5_state_inputs",
    )(x, x, w)


def _s5out_kernel(x0_ref, x1_ref, m_ref, s_ref, r_ref, d_ref, wg_ref, bg_ref, o_ref):
    rows = s_ref.shape[1]
    wd = 2 * LANES
    xt = [_row_tokens(x0_ref, x1_ref, t, rows) for t in range(S5_T)]
    acc = _dot(s_ref[0].astype(BF16), r_ref[...])
    for t in range(S5_T):
        acc += _dot(xt[t].astype(BF16), m_ref[t * wd:(t + 1) * wd, :])
    for t in range(S5_T):
        g = jax.nn.gelu(acc[:, t * wd:(t + 1) * wd] + d_ref[...] * xt[t], approximate=True)
        y = g * jax.nn.sigmoid(_dot(g.astype(BF16), wg_ref[...]) + bg_ref[...])
        o_ref[0, 0, pl.ds(t, rows, stride=S5_T), :] = y[:, :LANES]
        o_ref[1, 0, pl.ds(t, rows, stride=S5_T), :] = y[:, LANES:]


def s5_readout_glu(x, m_intra, xs, r_out, d_skip, w_glu, b_glu):
    bn, L, wd = x.shape
    assert wd == 2 * LANES
    r = L // S5_T
    tm = min(r, 256)
    one = pl.Buffered(1)
    half = lambda h: pl.BlockSpec((1, tm * S5_T, LANES), lambda b, i: (b, i, h))
    row = pl.BlockSpec((1, wd), lambda b, i: (0, 0))
    return pl.pallas_call(
        _s5out_kernel,
        grid=(bn, r // tm),
        in_specs=[half(0), half(1), pl.BlockSpec(m_intra.shape, lambda b, i: (0, 0), pipeline_mode=one),
                  pl.BlockSpec((1, tm, xs.shape[-1]), lambda b, i: (b, i, 0)),
                  pl.BlockSpec(r_out.shape, lambda b, i: (0, 0), pipeline_mode=one),
                  row, pl.BlockSpec((wd, wd), lambda b, i: (0, 0)), row],
        out_specs=pl.BlockSpec((2, 1, tm * S5_T, LANES), lambda b, i: (0, b, i, 0)),
        out_shape=jax.ShapeDtypeStruct((2, bn, L, LANES), F32),
        compiler_params=_cparams(("arbitrary", "arbitrary")),
        name="s5_readout_glu",
    )(x, x, m_intra, xs, r_out, d_skip.reshape(1, wd), w_glu, b_glu.reshape(1, wd))


def _s5scan_kernel(r_ref, a_ref, h0_ref, x_ref, fin_ref):
    nk = r_ref.shape[1]
    sw = S5_GROUPS * S5_N
    ar_f, ai_f = a_ref[:, 0:sw], a_ref[:, sw:2 * sw]
    ar_b, ai_b = a_ref[:, 2 * sw:3 * sw], a_ref[:, 3 * sw:4 * sw]
    init = tuple(h0_ref[0, :, j * sw:(j + 1) * sw] for j in range(4))

    def body(i, carry):
        fr, fi, br, bi = carry
        k = nk - 1 - i
        x_ref[0, pl.ds(i, 1), 0:sw] = fr
        x_ref[0, pl.ds(i, 1), sw:2 * sw] = fi
        x_ref[0, pl.ds(k, 1), 2 * sw:3 * sw] = br
        x_ref[0, pl.ds(k, 1), 3 * sw:4 * sw] = bi
        rfr = r_ref[0, pl.ds(i, 1), 0:sw]
        rfi = r_ref[0, pl.ds(i, 1), sw:2 * sw]
        rbr = r_ref[0, pl.ds(k, 1), 2 * sw:3 * sw]
        rbi = r_ref[0, pl.ds(k, 1), 3 * sw:4 * sw]
        return (ar_f * fr - ai_f * fi + rfr, ar_f * fi + ai_f * fr + rfi,
                ar_b * br - ai_b * bi + rbr, ar_b * bi + ai_b * br + rbi)
    fin = lax.fori_loop(0, nk, body, init)
    for j in range(4):
        fin_ref[0, :, j * sw:(j + 1) * sw] = fin[j]


def s5_row_scan(r, a8v, h0):
    bn, nk, w = r.shape
    return pl.pallas_call(
        _s5scan_kernel,
        grid=(bn,),
        in_specs=[pl.BlockSpec((1, nk, w), lambda b: (b, 0, 0)),
                  pl.BlockSpec((1, w), lambda b: (0, 0)),
                  pl.BlockSpec((1, 1, w), lambda b: (b, 0, 0))],
        out_specs=[pl.BlockSpec((1, nk, w), lambda b: (b, 0, 0)),
                   pl.BlockSpec((1, 1, w), lambda b: (b, 0, 0))],
        out_shape=[jax.ShapeDtypeStruct((bn, nk, w), F32), jax.ShapeDtypeStruct((bn, 1, w), F32)],
        compiler_params=_cparams(("arbitrary",)),
        name="s5_row_scan",
    )(r, a8v, h0)


def s5_mix(p_s5, mats, d_skip, w_glu, b_glu, h0):
    m_intra, r_in, r_out, a8v = mats
    r = s5_state_inputs(p_s5, r_in, tn=1024)
    xs, fin = s5_row_scan(r, a8v, h0)
    return s5_readout_glu(p_s5, m_intra, xs, r_out, d_skip, w_glu, b_glu), fin


def _qk_kernel(x_ref, wq_ref, wk_ref, q_ref, k_ref):
    for h in range(ML_HEADS):
        sl = slice(h * ML_HD, (h + 1) * ML_HD)
        xb = x_ref[0, :, sl].astype(BF16)
        q_ref[0, :, sl] = _dot(xb, wq_ref[h]).astype(BF16)
        k_ref[0, :, sl] = (_dot(xb, wk_ref[h]) * (ML_HD ** -0.5)).astype(BF16)


def mlstm_qk(xc, wq, wk):
    bn, L, w = xc.shape
    tm = _row_tile(L)
    tok = pl.BlockSpec((1, tm, w), lambda b, i: (b, i, 0))
    wsp = pl.BlockSpec(wq.shape, lambda b, i: (0, 0, 0))
    return pl.pallas_call(
        _qk_kernel,
        grid=(bn, L // tm),
        in_specs=[tok, wsp, wsp],
        out_specs=[tok, tok],
        out_shape=[jax.ShapeDtypeStruct((bn, L, w), BF16)] * 2,
        compiler_params=_cparams(("arbitrary", "arbitrary")),
        name="mlstm_qk",
    )(xc, wq, wk)


def _mlstm_kernel(bias_ref, q_ref, k_ref, v_ref, g_ref, c0_ref, n0_ref, m0_ref,
                  h_ref, cf_ref, nf_ref, mf_ref, st_s, m_s, lf_s, b_s):
    hd = pl.program_id(0)
    bn = q_ref.shape[0]
    nc = g_ref.shape[2]
    t = ML_CHUNK
    chains = [(b, d) for b in range(bn) for d in range(2)]
    for i, (b, d) in enumerate(chains):
        st_s[i, :, 0:ML_HD] = c0_ref[b, d, 0].T
        st_s[i, :, ML_HD:] = jnp.broadcast_to(n0_ref[b, d, 0], (ML_HD, ML_HD)).T
        m_s[i] = m0_ref[b, d, 0]
    ones_blk = jnp.ones((t, ML_HD), BF16)
    row = lax.broadcasted_iota(jnp.int32, (t, t), 0)
    col = lax.broadcasted_iota(jnp.int32, (t, t), 1)
    tri = (col <= row, col >= row)
    cum = ((row <= col).astype(F32), (row >= col).astype(F32))
    bias_i = [bias_ref[d * 2 * ML_HEADS + hd] for d in range(2)]
    bias_f = [bias_ref[d * 2 * ML_HEADS + ML_HEADS + hd] for d in range(2)]
    for i, (b, d) in enumerate(chains):
        lf_all = jax.nn.log_sigmoid(g_ref[b, d * 2 * ML_HEADS + ML_HEADS + hd] + bias_f[d])
        lf_s[i] = lf_all
        b_s[i] = jnp.dot(lf_all, cum[d], precision=HIGHEST, preferred_element_type=F32)

    def body(j, carry):
        ids = range(len(chains))
        cidx = [j if d == 0 else nc - 1 - j for _, d in chains]
        r0 = [pl.multiple_of(c * t, t) for c in cidx]
        q = [q_ref[b, pl.ds(r0[i], t), :] for i, (b, d) in enumerate(chains)]
        k = [k_ref[b, pl.ds(r0[i], t), :] for i, (b, d) in enumerate(chains)]
        v = [v_ref[b, pl.ds(r0[i], t), :] for i, (b, d) in enumerate(chains)]
        li = [g_ref[b, d * 2 * ML_HEADS + hd, pl.ds(cidx[i], 1), :] + bias_i[d] for i, (b, d) in enumerate(chains)]
        lf = [lf_s[i, pl.ds(cidx[i], 1), :] for i in ids]
        b_row = [b_s[i, pl.ds(cidx[i], 1), :] for i in ids]
        b_col = [jnp.sum(jnp.where(tri[d], lf[i], 0.0), axis=-1, keepdims=True) for i, (b, d) in enumerate(chains)]
        g = [jnp.sum(lf[i], axis=-1, keepdims=True) for i in ids]
        a_row = [g[i] - b_row[i] + li[i] for i in ids]
        m_loc = [jnp.max(a_row[i], axis=-1, keepdims=True) for i in ids]
        w_row = [jnp.exp(a_row[i] - m_loc[i]) for i in ids]
        st_prev = [st_s[i] for i in ids]
        m_prev = [m_s[i][:, 0:1] for i in ids]
        kt = [k[i].astype(F32).T for i in ids]
        qr = [_dot(q[i], jnp.concatenate([st_prev[i].astype(BF16), kt[i].astype(BF16)], axis=1)) for i in ids]
        upd = [_dot((kt[i] * w_row[i]).astype(BF16), jnp.concatenate([v[i], ones_blk], axis=1)) for i in ids]
        dmat = [jnp.where(tri[d], b_col[i] - b_row[i] + li[i], -jnp.inf) for i, (b, d) in enumerate(chains)]
        inter = [b_col[i] + m_prev[i] for i in ids]
        m_t = [jnp.maximum(jnp.max(dmat[i], axis=-1, keepdims=True), inter[i]) for i in ids]
        s = [qr[i][:, 2 * ML_HD:] * jnp.exp(dmat[i] - m_t[i]) for i in ids]
        w_inter = [jnp.exp(inter[i] - m_t[i]) for i in ids]
        sv = [_dot(s[i].astype(BF16), v[i]) for i in ids]
        nq = [jnp.sum(s[i], axis=-1, keepdims=True) + w_inter[i] * qr[i][:, ML_HD:ML_HD + 1] for i in ids]
        for i, (b, d) in enumerate(chains):
            num = sv[i] + w_inter[i] * qr[i][:, :ML_HD]
            h_ref[d, b, pl.ds(r0[i], t), :] = num / jnp.maximum(jnp.abs(nq[i]), jnp.exp(-m_t[i]))
        for i in ids:
            m_new = jnp.maximum(g[i] + m_prev[i], m_loc[i])
            dec = jnp.exp(g[i] + m_prev[i] - m_new)
            grow = jnp.exp(m_loc[i] - m_new)
            st_s[i] = dec * st_prev[i] + grow * upd[i]
            m_s[i] = jnp.broadcast_to(m_new, (1, LANES))
        return carry
    lax.fori_loop(0, nc, body, 0, unroll=2)
    for i, (b, d) in enumerate(chains):
        cf_ref[b, d, 0] = st_s[i, :, 0:ML_HD].T
        nf_ref[b, d, 0] = st_s[i, :, ML_HD:].T[0:1]
        mf_ref[b, d, 0] = m_s[i]


def mlstm_scan(q, k, v, gates_t, gate_bias, state):
    bn, L, w = q.shape
    c0, n0, m0 = state
    hsp = pl.BlockSpec((bn, L, ML_HD), lambda h: (0, 0, h), pipeline_mode=pl.Buffered(1))
    st = lambda a: pl.BlockSpec((bn, 2, 1) + a.shape[3:], lambda h: (0, 0, h, 0, 0))
    return pl.pallas_call(
        _mlstm_kernel,
        grid=(ML_HEADS,),
        in_specs=[pl.BlockSpec(memory_space=pltpu.SMEM), hsp, hsp, hsp,
                  pl.BlockSpec(gates_t.shape, lambda h: (0, 0, 0, 0)),
                  st(c0), st(n0), st(m0)],
        out_specs=[pl.BlockSpec((2, bn, L, ML_HD), lambda h: (0, 0, 0, h)), st(c0), st(n0), st(m0)],
        out_shape=[jax.ShapeDtypeStruct((2, bn, L, w), F32),
                   jax.ShapeDtypeStruct(c0.shape, F32), jax.ShapeDtypeStruct(n0.shape, F32),
                   jax.ShapeDtypeStruct(m0.shape, F32)],
        scratch_shapes=[pltpu.VMEM((2 * bn, ML_HD, 2 * ML_HD), F32),
                        pltpu.VMEM((2 * bn, 1, LANES), F32),
                        pltpu.VMEM((2 * bn,) + gates_t.shape[2:], F32),
                        pltpu.VMEM((2 * bn,) + gates_t.shape[2:], F32)],
        compiler_params=_cparams(("arbitrary",)),
        name="mlstm_scan",
    )(gate_bias, q, k, v, gates_t, c0, n0, m0)


def _mlout_kernel(hf_ref, hb_ref, o_ref, xc_ref, gain_ref, skip_ref, y_ref):
    h = hf_ref[0, 0] + hb_ref[0, 0]
    parts = []
    for hd in range(ML_HEADS):
        sl = slice(hd * ML_HD, (hd + 1) * ML_HD)
        hh = h[:, sl]
        parts.append(hh * lax.rsqrt(jnp.mean(hh * hh, axis=-1, keepdims=True) + EPS) * gain_ref[:, sl])
    hn = jnp.concatenate(parts, axis=-1)
    y_ref[0] = jax.nn.sigmoid(o_ref[0]) * (hn + skip_ref[...] * xc_ref[0])


def mlstm_output(h2, p_o, xc, gain, skip):
    _, bn, L, w = h2.shape
    tm = _row_tile(L)
    tok = pl.BlockSpec((1, tm, w), lambda b, i: (b, i, 0))
    row = pl.BlockSpec((1, w), lambda b, i: (0, 0))
    return pl.pallas_call(
        _mlout_kernel,
        grid=(bn, L // tm),
        in_specs=[pl.BlockSpec((1, 1, tm, w), lambda b, i: (0, b, i, 0)),
                  pl.BlockSpec((1, 1, tm, w), lambda b, i: (1, b, i, 0)), tok, tok, row, row],
        out_specs=tok,
        out_shape=jax.ShapeDtypeStruct((bn, L, w), F32),
        compiler_params=_cparams(("arbitrary", "arbitrary")),
        name="mlstm_output",
    )(h2, h2, p_o, xc, gain.reshape(1, w), skip.reshape(1, w))


def mlstm_mix(p_x, p_v, p_o, p_g, conv_w9, width, rows, wq, wk, gate_bias, skip, gain, state):
    bn, L, _ = p_x.shape
    xc = depthwise_conv(p_x, conv_w9, width=width, rows=rows, act=_silu)
    q, k = mlstm_qk(xc, wq, wk)
    gates_t = jnp.swapaxes(p_g[..., :ML_GATES], 1, 2).reshape(bn, ML_GATES, L // ML_CHUNK, ML_CHUNK)
    h2, cf, nf, mf = mlstm_scan(q, k, p_v, gates_t, gate_bias.reshape(ML_GATES), state)
    return mlstm_output(h2, p_o, xc, gain, skip), (cf, nf, mf)


PROJ_SPLITS = ((0, 3 * HY_W), (3 * HY_W, 3 * HY_W + S5_W),
               (3 * HY_W + S5_W, 3 * HY_W + S5_W + ML_W),
               (3 * HY_W + S5_W + ML_W, 3 * HY_W + S5_W + 2 * ML_W),
               (3 * HY_W + S5_W + 2 * ML_W, 3 * HY_W + S5_W + 3 * ML_W),
               (3 * HY_W + S5_W + 3 * ML_W, 3 * HY_W + S5_W + 3 * ML_W + LANES))
PROJ_DTYPES = (F32, F32, F32, BF16, F32, F32)


def kernel(x, c, ctx, c_ctx, w_mod, b_mod, g_pre_mix, g_post_mix, g_pre_mlp, g_post_mlp, w_in, w_out, hy_conv, hy_w1, hy_b1, hy_w2, hy_b2, hy_w3, hy_freq, hy_decay, hy_bias, s5_a_re, s5_a_im, s5_log_dt, s5_b_re, s5_b_im, s5_c_re, s5_c_im, s5_d, s5_w_glu, s5_b_glu, ml_conv, ml_wq, ml_wk, ml_gate_bias, ml_skip, ml_norm_gain, w_mlp1, w_mlp2):
    bsz, seq_len, d = x.shape
    ctx_len = ctx.shape[1]
    depth = w_mod.shape[0]

    cc = jnp.zeros((SUBLANES, d), F32).at[:bsz].set(c).at[bsz].set(c_ctx)
    mods = mod_vectors(cc, w_mod, b_mod)

    proj_w = w_in.shape[-1]
    w_in_b = jnp.pad(w_in, ((0, 0), (0, 0), (0, PROJ_SPLITS[-1][1] - proj_w))).astype(BF16)
    w_out_b, w1_b, w2_b = w_out.astype(BF16), w_mlp1.astype(BF16), w_mlp2.astype(BF16)
    wq_b, wk_b, wglu_b = ml_wq.astype(BF16), ml_wk.astype(BF16), s5_w_glu.astype(BF16)

    s5_zero = jnp.zeros((bsz, 1, 4 * S5_GROUPS * S5_N), F32)
    ml_zero = (jnp.zeros((bsz, 2, ML_HEADS, ML_HD, ML_HD), F32),
               jnp.zeros((bsz, 2, ML_HEADS, 1, ML_HD), F32),
               jnp.zeros((bsz, 2, ML_HEADS, 1, LANES), F32))

    for l in range(depth):
        mx = [mods[l, :bsz, j * d:(j + 1) * d][:, None, :] for j in range(6)]
        mc = [jnp.broadcast_to(mods[l, bsz, j * d:(j + 1) * d][None, None, :], (bsz, 1, d)) for j in range(6)]
        filt_x = hyena_filters(seq_len, hy_w1[l], hy_b1[l], hy_w2[l], hy_b2[l], hy_w3[l], hy_freq[l], hy_decay[l])
        filt_c = hyena_filters(ctx_len, hy_w1[l], hy_b1[l], hy_w2[l], hy_b2[l], hy_w3[l], hy_freq[l], hy_decay[l])
        s5_mats = s5_matrices(s5_a_re[l], s5_a_im[l], s5_log_dt[l], s5_b_re[l], s5_b_im[l], s5_c_re[l], s5_c_im[l])
        conv2d_w = ml_conv[l].reshape(9, ML_W)

        def mix(h_in, mod, L, fft_len, filt, width, rows, s5_h0, ml_state):
            p_hy, p_s5, p_mx, p_mv, p_mo, p_mg = in_projection(h_in, g_pre_mix[l], mod[1], mod[0], w_in_b[l],
                                                               PROJ_SPLITS, PROJ_DTYPES)
            y_hy = hyena_mix(p_hy, hy_conv[l], filt, hy_bias[l], fft_len)
            y_s5, s5_fin = s5_mix(p_s5, s5_mats, s5_d[l], wglu_b[l], s5_b_glu[l], s5_h0)
            y_ml, ml_fin = mlstm_mix(p_mx, p_mv, p_mo, p_mg, conv2d_w, width, rows, wq_b[l], wk_b[l],
                                     ml_gate_bias[l], ml_skip[l], ml_norm_gain[l], ml_state)
            return (y_hy, y_s5, y_ml), s5_fin, ml_fin

        y_c, s5_state, ml_state = mix(ctx, mc, ctx_len, CTX_FFT_LEN, filt_c, ctx_len, (0,), s5_zero, ml_zero)
        y_x, _, _ = mix(x, mx, seq_len, seq_len, filt_x, GRID_W, (-1, 0, 1), s5_state, ml_state)
        x = out_projection(*y_x, w_out_b[l], g_post_mix[l], mx[2], x)
        x = mlp_block(x, g_pre_mlp[l], mx[4], mx[3], w1_b[l], w2_b[l], g_post_mlp[l], mx[5])
        if l < depth - 1:
            ctx = out_projection(*y_c, w_out_b[l], g_post_mix[l], mc[2], ctx)
            ctx = mlp_block(ctx, g_pre_mlp[l], mc[4], mc[3], w1_b[l], w2_b[l], g_post_mlp[l], mc[5])
    return x
```

```python
import functools
import math

import numpy as np
import jax
import jax.numpy as jnp
from jax import lax
from jax.experimental import pallas as pl
from jax.experimental.pallas import tpu as pltpu

F32 = jnp.float32
BF16 = jnp.bfloat16
EPS = 1e-6
HIGHEST = lax.Precision.HIGHEST

V7X_VMEM_BYTES = 64 * 1024 * 1024
VMEM_LIMIT = V7X_VMEM_BYTES - 8 * 1024 * 1024
LANES = 128
SUBLANES = 8
FFT_N2 = 64
FFT_UNROLL_A = 4
FFT_UNROLL_B = 8

HY_W = 256
HY_BANDS = 8
HY_EMB = 1 + 2 * HY_BANDS
HY_FF = 64
S5_W = 256
S5_CH = 16
S5_GROUPS = S5_W // S5_CH
S5_N = 64
S5_T = 8
ML_HEADS = 4
ML_HD = 128
ML_W = ML_HEADS * ML_HD
ML_CHUNK = 64
ML_GATES = 4 * ML_HEADS
GRID_W = 64
CTX_FFT_LEN = 512


def _cparams(sem):
    return pltpu.CompilerParams(dimension_semantics=sem, vmem_limit_bytes=VMEM_LIMIT)


def _dot(a, b):
    return jnp.dot(a, b, preferred_element_type=F32)


def _dot_nt(a, b):
    return lax.dot_general(a, b, (((1,), (1,)), ((), ())), preferred_element_type=F32)


def _rms(x, g):
    return x * lax.rsqrt(jnp.mean(x * x, axis=-1, keepdims=True) + EPS) * g


def _silu(x):
    return x * jax.nn.sigmoid(x)


def _mod_kernel(c_ref, w_ref, b_ref, o_ref):
    s = _silu(c_ref[...]).astype(BF16)
    o_ref[0] = _dot(s, w_ref[0].astype(BF16)) + b_ref[0]


def mod_vectors(cc, w_mod, b_mod):
    depth, d, n = w_mod.shape
    r = cc.shape[0]
    tn = 1536
    return pl.pallas_call(
        _mod_kernel,
        grid=(depth, n // tn),
        in_specs=[pl.BlockSpec((r, d), lambda l, j: (0, 0)),
                  pl.BlockSpec((1, d, tn), lambda l, j: (l, 0, j)),
                  pl.BlockSpec((1, 1, tn), lambda l, j: (l, 0, j))],
        out_specs=pl.BlockSpec((1, r, tn), lambda l, j: (l, 0, j)),
        out_shape=jax.ShapeDtypeStruct((depth, r, n), F32),
        compiler_params=_cparams(("arbitrary", "arbitrary")),
        name="mod_vectors",
    )(cc, w_mod, b_mod.reshape(depth, 1, n))


def _row_tile(L):
    return min(L, 512)


def _inproj_kernel(x_ref, g_ref, sc_ref, sh_ref, w_ref, *o_refs, splits):
    h = _rms(x_ref[0], g_ref[...]) * (1.0 + sc_ref[0]) + sh_ref[0]
    hb = h.astype(BF16)
    for o_ref, (a, b) in zip(o_refs, splits):
        o_ref[0] = _dot(hb, w_ref[:, a:b]).astype(o_ref.dtype)


def in_projection(x, g, sc, sh, w, splits, dtypes):
    bn, L, d = x.shape
    tm = _row_tile(L)
    vec = pl.BlockSpec((1, 1, d), lambda b, i: (b, 0, 0))
    return pl.pallas_call(
        functools.partial(_inproj_kernel, splits=splits),
        grid=(bn, L // tm),
        in_specs=[pl.BlockSpec((1, tm, d), lambda b, i: (b, i, 0)),
                  pl.BlockSpec((1, d), lambda b, i: (0, 0)), vec, vec,
                  pl.BlockSpec(w.shape, lambda b, i: (0, 0))],
        out_specs=[pl.BlockSpec((1, tm, b_ - a_), lambda b, i: (b, i, 0)) for a_, b_ in splits],
        out_shape=[jax.ShapeDtypeStruct((bn, L, b_ - a_), dt) for (a_, b_), dt in zip(splits, dtypes)],
        compiler_params=_cparams(("arbitrary", "arbitrary")),
        name="in_projection",
    )(x, g.reshape(1, d), sc, sh, w)


def _outproj_kernel(yh_ref, ys0_ref, ys1_ref, ym_ref, w_ref, g_ref, gt_ref, x_ref, o_ref):
    a = yh_ref.shape[-1]
    acc = _dot(yh_ref[0].astype(BF16), w_ref[0:a])
    acc += _dot(ys0_ref[0, 0].astype(BF16), w_ref[a:a + LANES])
    acc += _dot(ys1_ref[0, 0].astype(BF16), w_ref[a + LANES:a + 2 * LANES])
    acc += _dot(ym_ref[0].astype(BF16), w_ref[a + 2 * LANES:])
    o_ref[0] = x_ref[0] + gt_ref[0] * _rms(acc, g_ref[...])


def out_projection(y_hy, y_s5, y_ml, w, g, gate, x):
    bn, L, d = x.shape
    tm = _row_tile(L)
    tok = lambda wd: pl.BlockSpec((1, tm, wd), lambda b, i: (b, i, 0))
    half = lambda h: pl.BlockSpec((1, 1, tm, LANES), lambda b, i: (h, b, i, 0))
    return pl.pallas_call(
        _outproj_kernel,
        grid=(bn, L // tm),
        in_specs=[tok(y_hy.shape[-1]), half(0), half(1), tok(y_ml.shape[-1]),
                  pl.BlockSpec(w.shape, lambda b, i: (0, 0)),
                  pl.BlockSpec((1, d), lambda b, i: (0, 0)),
                  pl.BlockSpec((1, 1, d), lambda b, i: (b, 0, 0)), tok(d)],
        out_specs=tok(d),
        out_shape=jax.ShapeDtypeStruct((bn, L, d), F32),
        compiler_params=_cparams(("arbitrary", "arbitrary")),
        name="out_projection",
    )(y_hy, y_s5, y_s5, y_ml, w, g.reshape(1, d), gate, x)


def _mlp_kernel(x_ref, g1_ref, sc_ref, sh_ref, w1_ref, w2_ref, g2_ref, gt_ref, o_ref, h_ref, acc_ref):
    j = pl.program_id(2)

    @pl.when(j == 0)
    def _():
        h = _rms(x_ref[0], g1_ref[...]) * (1.0 + sc_ref[0]) + sh_ref[0]
        h_ref[...] = h.astype(BF16)
        acc_ref[...] = jnp.zeros_like(acc_ref)

    a = jnp.maximum(_dot(h_ref[...], w1_ref[...]), 0.0)
    acc_ref[...] += _dot((a * a).astype(BF16), w2_ref[...])

    @pl.when(j == pl.num_programs(2) - 1)
    def _():
        o_ref[0] = x_ref[0] + gt_ref[0] * _rms(acc_ref[...], g2_ref[...])


def mlp_block(x, g_pre, sc, sh, w1, w2, g_post, gate):
    bn, L, d = x.shape
    hid = w1.shape[1]
    tm = _row_tile(L)
    th = 1024
    tok = pl.BlockSpec((1, tm, d), lambda b, i, j: (b, i, 0))
    vec = pl.BlockSpec((1, 1, d), lambda b, i, j: (b, 0, 0))
    gain = pl.BlockSpec((1, d), lambda b, i, j: (0, 0))
    return pl.pallas_call(
        _mlp_kernel,
        grid=(bn, L // tm, hid // th),
        in_specs=[tok, gain, vec, vec,
                  pl.BlockSpec((d, th), lambda b, i, j: (0, j)),
                  pl.BlockSpec((th, d), lambda b, i, j: (j, 0)),
                  gain, vec],
        out_specs=tok,
        out_shape=jax.ShapeDtypeStruct((bn, L, d), F32),
        scratch_shapes=[pltpu.VMEM((tm, d), BF16), pltpu.VMEM((tm, d), F32)],
        compiler_params=_cparams(("arbitrary", "arbitrary", "arbitrary")),
        name="mlp_block",
    )(x, g_pre.reshape(1, d), sc, sh, w1, w2, g_post.reshape(1, d), gate)


def _dwconv_kernel(x_ref, w_ref, o_ref, pad_ref, *, width, taps, act, pad):
    L = x_ref.shape[1]
    ch = x_ref.shape[2]
    pad_ref[0:pad, :] = jnp.zeros((pad, ch), F32)
    pad_ref[pad + L:pad + L + pad, :] = jnp.zeros((pad, ch), F32)
    pad_ref[pad:pad + L, :] = x_ref[0]
    tr = min(L, 256)
    for r0 in range(0, L, tr):
        col = (lax.broadcasted_iota(jnp.int32, (tr, ch), 0) + r0) & (width - 1)
        acc = jnp.zeros((tr, ch), F32)
        for dr, dc in taps:
            start = pad + r0 + dr * width + dc
            v = pad_ref[start:start + tr, :]
            if dc == -1:
                v = jnp.where(col >= 1, v, 0.0)
            elif dc == 1:
                v = jnp.where(col <= width - 2, v, 0.0)
            acc = acc + w_ref[(dr + 1) * 3 + (dc + 1):(dr + 1) * 3 + (dc + 2), :] * v
        o_ref[0, r0:r0 + tr, :] = act(acc)


def depthwise_conv(x, w9, width, rows, act=None):
    bn, L, ch = x.shape
    assert width & (width - 1) == 0
    taps = tuple((dr, dc) for dr in rows for dc in (-1, 0, 1))
    pad = -(-(width + 1) // SUBLANES) * SUBLANES if len(rows) > 1 else SUBLANES
    act = act or (lambda a: a)
    cs = LANES
    return pl.pallas_call(
        functools.partial(_dwconv_kernel, width=width, taps=taps, act=act, pad=pad),
        grid=(bn, ch // cs),
        in_specs=[pl.BlockSpec((1, L, cs), lambda b, c: (b, 0, c)),
                  pl.BlockSpec((9, cs), lambda b, c: (0, c))],
        out_specs=pl.BlockSpec((1, L, cs), lambda b, c: (b, 0, c)),
        out_shape=jax.ShapeDtypeStruct((bn, L, ch), F32),
        scratch_shapes=[pltpu.VMEM((L + 2 * pad, cs), F32)],
        compiler_params=_cparams(("arbitrary", "arbitrary")),
        name="depthwise_conv",
    )(x, w9)


def _hyfilt_kernel(w1_ref, b1_ref, w2_ref, b2_ref, w3_ref, fr_ref, dec_ref, o_ref, *, L):
    tr = min(L, 512)
    wd = o_ref.shape[1]
    half = wd // 2
    fr = fr_ref[...]
    ssq = jnp.zeros((1, wd), F32)
    for r0 in range(0, L, tr):
        t = (lax.broadcasted_iota(jnp.int32, (tr, 32), 0) + r0).astype(F32) / L
        lane = lax.broadcasted_iota(jnp.int32, (tr, 32), 1)
        band = jnp.where(lane <= HY_BANDS, lane, lane - HY_BANDS).astype(F32)
        ang = 2.0 * math.pi * t * band
        feat = jnp.where(lane == 0, t, jnp.where(lane <= HY_BANDS, jnp.cos(ang),
                                                 jnp.where(lane <= 2 * HY_BANDS, jnp.sin(ang), 0.0)))
        hdn = jnp.sin(fr * (jnp.dot(feat, w1_ref[...], precision=HIGHEST, preferred_element_type=F32) + b1_ref[...]))
        hdn = jnp.sin(fr * (jnp.dot(hdn, w2_ref[...], precision=HIGHEST, preferred_element_type=F32) + b2_ref[...]))
        filt = _dot(hdn.astype(BF16), w3_ref[...].astype(BF16))
        filt = filt * jnp.exp(-t[:, 0:1] * dec_ref[...])
        ssq = ssq + jnp.sum(filt * filt, axis=0, keepdims=True)
        o_ref[r0:r0 + tr, :] = filt
    tot = ssq[:, :half] + ssq[:, half:]
    scale = lax.rsqrt(tot + EPS)
    scale = jnp.concatenate([scale, scale], axis=1)
    for r0 in range(0, L, tr):
        o_ref[r0:r0 + tr, :] = o_ref[r0:r0 + tr, :] * scale


def hyena_filters(L, w1, b1, w2, b2, w3, freq, decay):
    order = decay.shape[0]
    wd = 2 * HY_W
    w1p = jnp.zeros((32, HY_FF), F32).at[:HY_EMB].set(w1)
    full = lambda a: pl.BlockSpec(a.shape, lambda o: (0,) * a.ndim)
    args = (w1p, b1.reshape(1, HY_FF), w2, b2.reshape(1, HY_FF))
    return pl.pallas_call(
        functools.partial(_hyfilt_kernel, L=L),
        grid=(order,),
        in_specs=[full(a) for a in args] + [pl.BlockSpec((HY_FF, wd), lambda o: (0, o)),
                                            pl.BlockSpec((1, HY_FF), lambda o: (0, 0)),
                                            pl.BlockSpec((1, wd), lambda o: (0, o))],
        out_specs=pl.BlockSpec((L, wd), lambda o: (0, o)),
        out_shape=jax.ShapeDtypeStruct((L, order * wd), F32),
        compiler_params=_cparams(("arbitrary",)),
        name="hyena_filters",
    )(*args, w3, freq.reshape(1, HY_FF), decay.reshape(1, order * wd))


@functools.lru_cache(maxsize=None)
def _dft_consts(L):
    n = 2 * L
    n2c = FFT_N2
    n1c = n // n2c
    n1h = n1c // 2
    n1 = np.arange(n1h)[None, None, :]
    k1 = np.arange(n1c)[None, :, None]
    n2 = np.arange(n2c)[:, None, None]
    ang = -2.0 * np.pi * (k1 * n1 / n1c + n2 * k1 / n)
    fr, fi = np.cos(ang), np.sin(ang)
    fa = np.concatenate([np.concatenate([fr, -fi], axis=2),
                         np.concatenate([fi, fr], axis=2)], axis=1)
    frt = np.transpose(fr, (0, 2, 1)) / n
    fit = -np.transpose(fi, (0, 2, 1)) / n
    fai = np.concatenate([np.concatenate([frt, -fit], axis=2),
                          np.concatenate([fit, frt], axis=2)], axis=1)
    k2 = np.arange(n2c)[:, None]
    m2 = np.arange(n2c)[None, :]
    angb = -2.0 * np.pi * k2 * m2 / n2c
    gr, gi = np.cos(angb), np.sin(angb)
    gb = np.block([[gr, -gi], [gi, gr]])
    gbi = np.block([[gr, gi], [-gi, gr]])
    return (fa.astype(np.float32), gb.astype(np.float32), gbi.astype(np.float32), fai.astype(np.float32))


def _slab_rows(n1c):
    return 2 * n1c + SUBLANES


def _slab(n2, n1c):
    return pl.ds(pl.multiple_of(n2 * _slab_rows(n1c), SUBLANES), 2 * n1c)


def _fft_stage_a(load_rhs, fa_ref, s_ref, n1c):
    def body(n2, carry):
        rhs = load_rhs(n2).astype(BF16)
        s_ref[_slab(n2, n1c), :] = _dot(fa_ref[n2], rhs)
        return carry
    lax.fori_loop(0, FFT_N2, body, 0, unroll=FFT_UNROLL_A)


def _fft_load_k1(s_ref, k1, n1c):
    xr = s_ref[pl.ds(k1, FFT_N2, stride=_slab_rows(n1c)), :]
    xi = s_ref[pl.ds(n1c + k1, FFT_N2, stride=_slab_rows(n1c)), :]
    return jnp.concatenate([xr, xi], axis=0)


def _fftconv_kernel(z_ref, gate_ref, bias_ref, h_ref, fa_ref, gb_ref, gbi_ref, fai_ref, o_ref, s_ref, *, L):
    n1c = L // 32
    n1h = n1c // 2
    half = FFT_N2

    def load_rhs(n2):
        za = z_ref[0, 0, pl.ds(n2, n1h, stride=FFT_N2), :]
        zb = z_ref[0, 1, pl.ds(n2, n1h, stride=FFT_N2), :]
        return jnp.concatenate([za, zb], axis=0)
    _fft_stage_a(load_rhs, fa_ref, s_ref, n1c)

    def stage_b(k1, carry):
        x = _dot(gb_ref[...], _fft_load_k1(s_ref, k1, n1c).astype(BF16))
        xr, xi = x[:half], x[half:]
        h = h_ref[k1]
        hr, hi = h[:half], h[half:]
        y = jnp.concatenate([xr * hr - xi * hi, xr * hi + xi * hr], axis=0)
        bp = _dot(gbi_ref[...], y.astype(BF16))
        s_ref[pl.ds(k1, FFT_N2, stride=_slab_rows(n1c)), :] = bp[:half]
        s_ref[pl.ds(n1c + k1, FFT_N2, stride=_slab_rows(n1c)), :] = bp[half:]
        return carry
    lax.fori_loop(0, n1c, stage_b, 0, unroll=FFT_UNROLL_B)

    bias = bias_ref[...]

    def stage_a_inv(n2, carry):
        rhs = s_ref[_slab(n2, n1c), :].astype(BF16)
        res = _dot(fai_ref[n2], rhs)
        for p in range(2):
            rows = pl.ds(n2, n1h, stride=FFT_N2)
            zin = z_ref[0, p, rows, :]
            o_ref[0, p, rows, :] = gate_ref[0, p, rows, :] * (res[p * n1h:(p + 1) * n1h] + bias * zin)
        return carry
    lax.fori_loop(0, FFT_N2, stage_a_inv, 0, unroll=FFT_UNROLL_A)


def fft_gated_conv(z, zoff, gate, goff, bias, hspec):
    bsz, L, _ = z.shape
    ch = bias.shape[0]
    n1c = L // 32
    fa, gb, gbi, fai = (jnp.asarray(a, BF16) for a in _dft_consts(L))
    zp = z.reshape(bsz // 2, 2, L, z.shape[-1])
    gp = gate.reshape(bsz // 2, 2, L, gate.shape[-1])
    cs = LANES
    one = pl.Buffered(1)
    const3 = lambda a: pl.BlockSpec(a.shape, lambda c, p: (0, 0, 0), pipeline_mode=one)
    const2 = lambda a: pl.BlockSpec(a.shape, lambda c, p: (0, 0), pipeline_mode=one)
    out = pl.pallas_call(
        functools.partial(_fftconv_kernel, L=L),
        grid=(ch // cs, bsz // 2),
        in_specs=[pl.BlockSpec((1, 2, L, cs), lambda c, p: (p, 0, 0, c + zoff)),
                  pl.BlockSpec((1, 2, L, cs), lambda c, p: (p, 0, 0, c + goff)),
                  pl.BlockSpec((1, cs), lambda c, p: (0, c)),
                  pl.BlockSpec((n1c, 2 * FFT_N2, cs), lambda c, p: (0, 0, c), pipeline_mode=one),
                  const3(fa), const2(gb), const2(gbi), const3(fai)],
        out_specs=pl.BlockSpec((1, 2, L, cs), lambda c, p: (p, 0, 0, c)),
        out_shape=jax.ShapeDtypeStruct((bsz // 2, 2, L, ch), F32),
        scratch_shapes=[pltpu.VMEM((FFT_N2 * _slab_rows(n1c), cs), F32)],
        compiler_params=_cparams(("arbitrary", "arbitrary")),
        name="fft_gated_conv",
    )(zp, gp, bias.reshape(1, ch), hspec, fa, gb, gbi, fai)
    return out.reshape(bsz, L, ch)


def _fftspec_kernel(hf_ref, hb_ref, fa_ref, gb_ref, o_ref, s_ref, *, L):
    n1c = L // 32
    n1h = n1c // 2
    half = FFT_N2
    for d, src in enumerate((hf_ref, hb_ref)):
        def load_rhs(n2, src=src, d=d):
            h = src[pl.ds(n2, n1h, stride=FFT_N2), :]
            if d == 1:
                row = lax.broadcasted_iota(jnp.int32, h.shape, 0)
                h = jnp.where((row == 0) & (n2 == 0), 0.0, h)
            return jnp.concatenate([h, jnp.zeros_like(h)], axis=0)
        _fft_stage_a(load_rhs, fa_ref, s_ref, n1c)

        def stage_b(k1, carry, d=d):
            x = _dot(gb_ref[...], _fft_load_k1(s_ref, k1, n1c).astype(BF16))
            if d == 0:
                o_ref[k1] = x
            else:
                o_ref[k1] = o_ref[k1] + jnp.concatenate([x[:half], -x[half:]], axis=0)
            return carry
        lax.fori_loop(0, n1c, stage_b, 0, unroll=FFT_UNROLL_B)


def fft_filter_spectrum(filt, foff, boff, L):
    if filt.shape[0] < L:
        filt = jnp.pad(filt, ((0, L - filt.shape[0]), (0, 0)))
    n1c = L // 32
    fa, gb, _, _ = (jnp.asarray(a, BF16) for a in _dft_consts(L))
    cs = LANES
    return pl.pallas_call(
        functools.partial(_fftspec_kernel, L=L),
        grid=(HY_W // cs,),
        in_specs=[pl.BlockSpec((L, cs), lambda c: (0, c + foff)), pl.BlockSpec((L, cs), lambda c: (0, c + boff)),
                  pl.BlockSpec(fa.shape, lambda c: (0, 0, 0)), pl.BlockSpec(gb.shape, lambda c: (0, 0))],
        out_specs=pl.BlockSpec((n1c, 2 * FFT_N2, cs), lambda c: (0, 0, c)),
        out_shape=jax.ShapeDtypeStruct((n1c, 2 * FFT_N2, HY_W), F32),
        scratch_shapes=[pltpu.VMEM((FFT_N2 * _slab_rows(n1c), cs), F32)],
        compiler_params=_cparams(("arbitrary",)),
        name="fft_filter_spectrum",
    )(filt, filt, fa, gb)


def hyena_mix(p_hy, conv_w, filt, bias, fft_len):
    bsz, L, _ = p_hy.shape
    w9 = jnp.zeros((9, 3 * HY_W), F32).at[3:6].set(conv_w)
    u = depthwise_conv(p_hy, w9, width=L, rows=(0,))
    if fft_len > L:
        u = jnp.pad(u, ((0, 0), (0, fft_len - L), (0, 0)))
    nb = HY_W // LANES
    z = fft_gated_conv(u, 0, u, nb, bias[0], fft_filter_spectrum(filt, 0, nb, fft_len))
    z = fft_gated_conv(z, 0, u, 2 * nb, bias[1], fft_filter_spectrum(filt, 2 * nb, 3 * nb, fft_len))
    return z[:, :L]


def s5_matrices(a_re, a_im, log_dt, b_re, b_im, c_re, c_im):
    t = S5_T
    sw = S5_GROUPS * S5_N
    dt = jnp.exp(log_dt)[..., None]
    mag = jnp.exp(a_re * dt)
    ar, ai = mag * jnp.cos(a_im * dt), mag * jnp.sin(a_im * dt)
    den = a_re * a_re + a_im * a_im
    qr = ((ar - 1.0) * a_re + ai * a_im) / den
    qi = (ai * a_re - (ar - 1.0) * a_im) / den
    bbr = qr[..., None] * b_re - qi[..., None] * b_im
    bbi = qr[..., None] * b_im + qi[..., None] * b_re
    pr, pi = [jnp.ones_like(ar)], [jnp.zeros_like(ai)]
    for _ in range(t):
        pr, pi = pr + [pr[-1] * ar - pi[-1] * ai], pi + [pr[-1] * ai + pi[-1] * ar]
    pr = [p.reshape(2, sw) for p in pr]
    pi = [p.reshape(2, sw) for p in pi]
    gd_hn = (np.arange(S5_W)[:, None] // S5_CH == np.arange(sw)[None, :] // S5_N).astype(np.float32)
    gn_hc = gd_hn.T
    expand_b = lambda b: jnp.tile(jnp.swapaxes(b, -1, -2).reshape(2, S5_W, S5_N), (1, 1, S5_GROUPS)) * gd_hn
    expand_c = lambda c: jnp.tile(jnp.swapaxes(c, -1, -2).reshape(2, sw, S5_CH), (1, 1, S5_GROUPS)) * gn_hc
    b_r, b_i = expand_b(bbr), expand_b(bbi)
    c_r, c_i = expand_c(c_re), expand_c(c_im)
    col = lambda p, x: p[x][None, :]
    rowv = lambda p, x: p[x][:, None]
    r_in = jnp.concatenate([
        jnp.concatenate([b_r[0] * col(pr[t - 1 - s], 0) - b_i[0] * col(pi[t - 1 - s], 0),
                         b_r[0] * col(pi[t - 1 - s], 0) + b_i[0] * col(pr[t - 1 - s], 0),
                         b_r[1] * col(pr[s], 1) - b_i[1] * col(pi[s], 1),
                         b_r[1] * col(pi[s], 1) + b_i[1] * col(pr[s], 1)], axis=1)
        for s in range(t)], axis=0)

    def scaled_c(x, j):
        return (c_r[x] * rowv(pr[j], x) - c_i[x] * rowv(pi[j], x), c_r[x] * rowv(pi[j], x) + c_i[x] * rowv(pr[j], x))
    blocks = []
    for tt in range(t):
        fr, fi = scaled_c(0, tt + 1)
        br, bi = scaled_c(1, t - tt)
        blocks.append(jnp.concatenate([fr, -fi, br, -bi], axis=0))
    r_out = jnp.concatenate(blocks, axis=1)
    lags = (list(range(t)), list(range(t - 1, -1, -1)))
    taps = []
    for x in range(2):
        sc = [scaled_c(x, j) for j in lags[x]]
        taps.append(s5_taps(b_r[x], b_i[x], jnp.concatenate([p[0] for p in sc], axis=1),
                            jnp.concatenate([p[1] for p in sc], axis=1)))
    w = S5_W
    m_intra = jnp.concatenate([
        jnp.pad(taps[0][:, :(t - s) * w], ((0, 0), (s * w, 0)))
        + jnp.pad(taps[1][:, (t - 1 - s) * w:], ((0, 0), (0, (t - 1 - s) * w)))
        for s in range(t)], axis=0)
    a8v = jnp.concatenate([pr[t][0], pi[t][0], pr[t][1], pi[t][1]])[None, :]
    return m_intra.astype(BF16), r_in.astype(BF16), r_out.astype(BF16), a8v


def _s5taps_kernel(br_ref, bi_ref, cr_ref, ci_ref, o_ref):
    o_ref[...] = (_dot(br_ref[...].astype(BF16), cr_ref[...].astype(BF16))
                  - _dot(bi_ref[...].astype(BF16), ci_ref[...].astype(BF16)))


def s5_taps(b_r, b_i, c_r, c_i):
    m, n = b_r.shape[0], c_r.shape[1]
    tn = 512
    lhs = pl.BlockSpec(b_r.shape, lambda j: (0, 0))
    rhs = pl.BlockSpec((c_r.shape[0], tn), lambda j: (0, j))
    return pl.pallas_call(
        _s5taps_kernel,
        grid=(n // tn,),
        in_specs=[lhs, lhs, rhs, rhs],
        out_specs=pl.BlockSpec((m, tn), lambda j: (0, j)),
        out_shape=jax.ShapeDtypeStruct((m, n), F32),
        compiler_params=_cparams(("arbitrary",)),
        name="s5_taps",
    )(b_r, b_i, c_r, c_i)


def _row_tokens(x0_ref, x1_ref, t, rows):
    sl = pl.ds(t, rows, stride=S5_T)
    return jnp.concatenate([x0_ref[0, sl, :], x1_ref[0, sl, :]], axis=1)


def _s5in_kernel(x0_ref, x1_ref, w_ref, o_ref):
    rows = o_ref.shape[1]
    wd = 2 * LANES
    acc = _dot(_row_tokens(x0_ref, x1_ref, 0, rows).astype(BF16), w_ref[0:wd, :])
    for t in range(1, S5_T):
        acc += _dot(_row_tokens(x0_ref, x1_ref, t, rows).astype(BF16), w_ref[t * wd:(t + 1) * wd, :])
    o_ref[0] = acc


def s5_state_inputs(x, w, tn):
    bn, L, wd = x.shape
    assert wd == 2 * LANES
    r = L // S5_T
    n = w.shape[1]
    tm = min(r, 256)
    half = lambda h: pl.BlockSpec((1, tm * S5_T, LANES), lambda j, b, i: (b, i, h))
    return pl.pallas_call(
        _s5in_kernel,
        grid=(n // tn, bn, r // tm),
        in_specs=[half(0), half(1), pl.BlockSpec((S5_T * wd, tn), lambda j, b, i: (0, j))],
        out_specs=pl.BlockSpec((1, tm, tn), lambda j, b, i: (b, i, j)),
        out_shape=jax.ShapeDtypeStruct((bn, r, n), F32),
        compiler_params=_cparams(("arbitrary", "arbitrary", "arbitrary")),
        name="s5_state_inputs",
    )(x, x, w)


def _s5out_kernel(x0_ref, x1_ref, m_ref, s_ref, r_ref, d_ref, wg_ref, bg_ref, o_ref):
    rows = s_ref.shape[1]
    wd = 2 * LANES
    xt = [_row_tokens(x0_ref, x1_ref, t, rows) for t in range(S5_T)]
    acc = _dot(s_ref[0].astype(BF16), r_ref[...])
    for t in range(S5_T):
        acc += _dot(xt[t].astype(BF16), m_ref[t * wd:(t + 1) * wd, :])
    for t in range(S5_T):
        g = jax.nn.gelu(acc[:, t * wd:(t + 1) * wd] + d_ref[...] * xt[t], approximate=True)
        y = g * jax.nn.sigmoid(_dot(g.astype(BF16), wg_ref[...]) + bg_ref[...])
        o_ref[0, 0, pl.ds(t, rows, stride=S5_T), :] = y[:, :LANES]
        o_ref[1, 0, pl.ds(t, rows, stride=S5_T), :] = y[:, LANES:]


def s5_readout_glu(x, m_intra, xs, r_out, d_skip, w_glu, b_glu):
    bn, L, wd = x.shape
    assert wd == 2 * LANES
    r = L // S5_T
    tm = min(r, 256)
    one = pl.Buffered(1)
    half = lambda h: pl.BlockSpec((1, tm * S5_T, LANES), lambda b, i: (b, i, h))
    row = pl.BlockSpec((1, wd), lambda b, i: (0, 0))
    return pl.pallas_call(
        _s5out_kernel,
        grid=(bn, r // tm),
        in_specs=[half(0), half(1), pl.BlockSpec(m_intra.shape, lambda b, i: (0, 0), pipeline_mode=one),
                  pl.BlockSpec((1, tm, xs.shape[-1]), lambda b, i: (b, i, 0)),
                  pl.BlockSpec(r_out.shape, lambda b, i: (0, 0), pipeline_mode=one),
                  row, pl.BlockSpec((wd, wd), lambda b, i: (0, 0)), row],
        out_specs=pl.BlockSpec((2, 1, tm * S5_T, LANES), lambda b, i: (0, b, i, 0)),
        out_shape=jax.ShapeDtypeStruct((2, bn, L, LANES), F32),
        compiler_params=_cparams(("arbitrary", "arbitrary")),
        name="s5_readout_glu",
    )(x, x, m_intra, xs, r_out, d_skip.reshape(1, wd), w_glu, b_glu.reshape(1, wd))


def _s5scan_kernel(r_ref, a_ref, h0_ref, x_ref, fin_ref):
    nk = r_ref.shape[1]
    sw = S5_GROUPS * S5_N
    ar_f, ai_f = a_ref[:, 0:sw], a_ref[:, sw:2 * sw]
    ar_b, ai_b = a_ref[:, 2 * sw:3 * sw], a_ref[:, 3 * sw:4 * sw]
    init = tuple(h0_ref[0, :, j * sw:(j + 1) * sw] for j in range(4))

    def body(i, carry):
        fr, fi, br, bi = carry
        k = nk - 1 - i
        x_ref[0, pl.ds(i, 1), 0:sw] = fr
        x_ref[0, pl.ds(i, 1), sw:2 * sw] = fi
        x_ref[0, pl.ds(k, 1), 2 * sw:3 * sw] = br
        x_ref[0, pl.ds(k, 1), 3 * sw:4 * sw] = bi
        rfr = r_ref[0, pl.ds(i, 1), 0:sw]
        rfi = r_ref[0, pl.ds(i, 1), sw:2 * sw]
        rbr = r_ref[0, pl.ds(k, 1), 2 * sw:3 * sw]
        rbi = r_ref[0, pl.ds(k, 1), 3 * sw:4 * sw]
        return (ar_f * fr - ai_f * fi + rfr, ar_f * fi + ai_f * fr + rfi,
                ar_b * br - ai_b * bi + rbr, ar_b * bi + ai_b * br + rbi)
    fin = lax.fori_loop(0, nk, body, init)
    for j in range(4):
        fin_ref[0, :, j * sw:(j + 1) * sw] = fin[j]


def s5_row_scan(r, a8v, h0):
    bn, nk, w = r.shape
    return pl.pallas_call(
        _s5scan_kernel,
        grid=(bn,),
        in_specs=[pl.BlockSpec((1, nk, w), lambda b: (b, 0, 0)),
                  pl.BlockSpec((1, w), lambda b: (0, 0)),
                  pl.BlockSpec((1, 1, w), lambda b: (b, 0, 0))],
        out_specs=[pl.BlockSpec((1, nk, w), lambda b: (b, 0, 0)),
                   pl.BlockSpec((1, 1, w), lambda b: (b, 0, 0))],
        out_shape=[jax.ShapeDtypeStruct((bn, nk, w), F32), jax.ShapeDtypeStruct((bn, 1, w), F32)],
        compiler_params=_cparams(("arbitrary",)),
        name="s5_row_scan",
    )(r, a8v, h0)


def s5_mix(p_s5, mats, d_skip, w_glu, b_glu, h0):
    m_intra, r_in, r_out, a8v = mats
    r = s5_state_inputs(p_s5, r_in, tn=1024)
    xs, fin = s5_row_scan(r, a8v, h0)
    return s5_readout_glu(p_s5, m_intra, xs, r_out, d_skip, w_glu, b_glu), fin


def _qk_kernel(x_ref, wq_ref, wk_ref, q_ref, k_ref):
    for h in range(ML_HEADS):
        sl = slice(h * ML_HD, (h + 1) * ML_HD)
        xb = x_ref[0, :, sl].astype(BF16)
        q_ref[0, :, sl] = _dot(xb, wq_ref[h]).astype(BF16)
        k_ref[0, :, sl] = (_dot(xb, wk_ref[h]) * (ML_HD ** -0.5)).astype(BF16)


def mlstm_qk(xc, wq, wk):
    bn, L, w = xc.shape
    tm = _row_tile(L)
    tok = pl.BlockSpec((1, tm, w), lambda b, i: (b, i, 0))
    wsp = pl.BlockSpec(wq.shape, lambda b, i: (0, 0, 0))
    return pl.pallas_call(
        _qk_kernel,
        grid=(bn, L // tm),
        in_specs=[tok, wsp, wsp],
        out_specs=[tok, tok],
        out_shape=[jax.ShapeDtypeStruct((bn, L, w), BF16)] * 2,
        compiler_params=_cparams(("arbitrary", "arbitrary")),
        name="mlstm_qk",
    )(xc, wq, wk)


def _mlstm_kernel(bias_ref, q_ref, k_ref, v_ref, g_ref, c0_ref, n0_ref, m0_ref,
                  h_ref, cf_ref, nf_ref, mf_ref, st_s, m_s, lf_s, b_s):
    hd = pl.program_id(0)
    bn = q_ref.shape[0]
    nc = g_ref.shape[2]
    t = ML_CHUNK
    chains = [(b, d) for b in range(bn) for d in range(2)]
    for i, (b, d) in enumerate(chains):
        st_s[i, :, 0:ML_HD] = c0_ref[b, d, 0].T
        st_s[i, :, ML_HD:] = jnp.broadcast_to(n0_ref[b, d, 0], (ML_HD, ML_HD)).T
        m_s[i] = m0_ref[b, d, 0]
    ones_blk = jnp.ones((t, ML_HD), BF16)
    row = lax.broadcasted_iota(jnp.int32, (t, t), 0)
    col = lax.broadcasted_iota(jnp.int32, (t, t), 1)
    tri = (col <= row, col >= row)
    cum = ((row <= col).astype(F32), (row >= col).astype(F32))
    bias_i = [bias_ref[d * 2 * ML_HEADS + hd] for d in range(2)]
    bias_f = [bias_ref[d * 2 * ML_HEADS + ML_HEADS + hd] for d in range(2)]
    for i, (b, d) in enumerate(chains):
        lf_all = jax.nn.log_sigmoid(g_ref[b, d * 2 * ML_HEADS + ML_HEADS + hd] + bias_f[d])
        lf_s[i] = lf_all
        b_s[i] = jnp.dot(lf_all, cum[d], precision=HIGHEST, preferred_element_type=F32)

    def body(j, carry):
        ids = range(len(chains))
        cidx = [j if d == 0 else nc - 1 - j for _, d in chains]
        r0 = [pl.multiple_of(c * t, t) for c in cidx]
        q = [q_ref[b, pl.ds(r0[i], t), :] for i, (b, d) in enumerate(chains)]
        k = [k_ref[b, pl.ds(r0[i], t), :] for i, (b, d) in enumerate(chains)]
        v = [v_ref[b, pl.ds(r0[i], t), :] for i, (b, d) in enumerate(chains)]
        li = [g_ref[b, d * 2 * ML_HEADS + hd, pl.ds(cidx[i], 1), :] + bias_i[d] for i, (b, d) in enumerate(chains)]
        lf = [lf_s[i, pl.ds(cidx[i], 1), :] for i in ids]
        b_row = [b_s[i, pl.ds(cidx[i], 1), :] for i in ids]
        b_col = [jnp.sum(jnp.where(tri[d], lf[i], 0.0), axis=-1, keepdims=True) for i, (b, d) in enumerate(chains)]
        g = [jnp.sum(lf[i], axis=-1, keepdims=True) for i in ids]
        a_row = [g[i] - b_row[i] + li[i] for i in ids]
        m_loc = [jnp.max(a_row[i], axis=-1, keepdims=True) for i in ids]
        w_row = [jnp.exp(a_row[i] - m_loc[i]) for i in ids]
        st_prev = [st_s[i] for i in ids]
        m_prev = [m_s[i][:, 0:1] for i in ids]
        kt = [k[i].astype(F32).T for i in ids]
        qr = [_dot(q[i], jnp.concatenate([st_prev[i].astype(BF16), kt[i].astype(BF16)], axis=1)) for i in ids]
        upd = [_dot((kt[i] * w_row[i]).astype(BF16), jnp.concatenate([v[i], ones_blk], axis=1)) for i in ids]
        dmat = [jnp.where(tri[d], b_col[i] - b_row[i] + li[i], -jnp.inf) for i, (b, d) in enumerate(chains)]
        inter = [b_col[i] + m_prev[i] for i in ids]
        m_t = [jnp.maximum(jnp.max(dmat[i], axis=-1, keepdims=True), inter[i]) for i in ids]
        s = [qr[i][:, 2 * ML_HD:] * jnp.exp(dmat[i] - m_t[i]) for i in ids]
        w_inter = [jnp.exp(inter[i] - m_t[i]) for i in ids]
        sv = [_dot(s[i].astype(BF16), v[i]) for i in ids]
        nq = [jnp.sum(s[i], axis=-1, keepdims=True) + w_inter[i] * qr[i][:, ML_HD:ML_HD + 1] for i in ids]
        for i, (b, d) in enumerate(chains):
            num = sv[i] + w_inter[i] * qr[i][:, :ML_HD]
            h_ref[d, b, pl.ds(r0[i], t), :] = num / jnp.maximum(jnp.abs(nq[i]), jnp.exp(-m_t[i]))
        for i in ids:
            m_new = jnp.maximum(g[i] + m_prev[i], m_loc[i])
            dec = jnp.exp(g[i] + m_prev[i] - m_new)
            grow = jnp.exp(m_loc[i] - m_new)
            st_s[i] = dec * st_prev[i] + grow * upd[i]
            m_s[i] = jnp.broadcast_to(m_new, (1, LANES))
        return carry
    lax.fori_loop(0, nc, body, 0)
    for i, (b, d) in enumerate(chains):
        cf_ref[b, d, 0] = st_s[i, :, 0:ML_HD].T
        nf_ref[b, d, 0] = st_s[i, :, ML_HD:].T[0:1]
        mf_ref[b, d, 0] = m_s[i]


def mlstm_scan(q, k, v, gates_t, gate_bias, state):
    bn, L, w = q.shape
    c0, n0, m0 = state
    hsp = pl.BlockSpec((bn, L, ML_HD), lambda h: (0, 0, h), pipeline_mode=pl.Buffered(1))
    st = lambda a: pl.BlockSpec((bn, 2, 1) + a.shape[3:], lambda h: (0, 0, h, 0, 0))
    return pl.pallas_call(
        _mlstm_kernel,
        grid=(ML_HEADS,),
        in_specs=[pl.BlockSpec(memory_space=pltpu.SMEM), hsp, hsp, hsp,
                  pl.BlockSpec(gates_t.shape, lambda h: (0, 0, 0, 0)),
                  st(c0), st(n0), st(m0)],
        out_specs=[pl.BlockSpec((2, bn, L, ML_HD), lambda h: (0, 0, 0, h)), st(c0), st(n0), st(m0)],
        out_shape=[jax.ShapeDtypeStruct((2, bn, L, w), F32),
                   jax.ShapeDtypeStruct(c0.shape, F32), jax.ShapeDtypeStruct(n0.shape, F32),
                   jax.ShapeDtypeStruct(m0.shape, F32)],
        scratch_shapes=[pltpu.VMEM((2 * bn, ML_HD, 2 * ML_HD), F32),
                        pltpu.VMEM((2 * bn, 1, LANES), F32),
                        pltpu.VMEM((2 * bn,) + gates_t.shape[2:], F32),
                        pltpu.VMEM((2 * bn,) + gates_t.shape[2:], F32)],
        compiler_params=_cparams(("arbitrary",)),
        name="mlstm_scan",
    )(gate_bias, q, k, v, gates_t, c0, n0, m0)


def _mlout_kernel(hf_ref, hb_ref, o_ref, xc_ref, gain_ref, skip_ref, y_ref):
    h = hf_ref[0, 0] + hb_ref[0, 0]
    parts = []
    for hd in range(ML_HEADS):
        sl = slice(hd * ML_HD, (hd + 1) * ML_HD)
        hh = h[:, sl]
        parts.append(hh * lax.rsqrt(jnp.mean(hh * hh, axis=-1, keepdims=True) + EPS) * gain_ref[:, sl])
    hn = jnp.concatenate(parts, axis=-1)
    y_ref[0] = jax.nn.sigmoid(o_ref[0]) * (hn + skip_ref[...] * xc_ref[0])


def mlstm_output(h2, p_o, xc, gain, skip):
    _, bn, L, w = h2.shape
    tm = _row_tile(L)
    tok = pl.BlockSpec((1, tm, w), lambda b, i: (b, i, 0))
    row = pl.BlockSpec((1, w), lambda b, i: (0, 0))
    return pl.pallas_call(
        _mlout_kernel,
        grid=(bn, L // tm),
        in_specs=[pl.BlockSpec((1, 1, tm, w), lambda b, i: (0, b, i, 0)),
                  pl.BlockSpec((1, 1, tm, w), lambda b, i: (1, b, i, 0)), tok, tok, row, row],
        out_specs=tok,
        out_shape=jax.ShapeDtypeStruct((bn, L, w), F32),
        compiler_params=_cparams(("arbitrary", "arbitrary")),
        name="mlstm_output",
    )(h2, h2, p_o, xc, gain.reshape(1, w), skip.reshape(1, w))


def mlstm_mix(p_x, p_v, p_o, p_g, conv_w9, width, rows, wq, wk, gate_bias, skip, gain, state):
    bn, L, _ = p_x.shape
    xc = depthwise_conv(p_x, conv_w9, width=width, rows=rows, act=_silu)
    q, k = mlstm_qk(xc, wq, wk)
    gates_t = jnp.swapaxes(p_g[..., :ML_GATES], 1, 2).reshape(bn, ML_GATES, L // ML_CHUNK, ML_CHUNK)
    h2, cf, nf, mf = mlstm_scan(q, k, p_v, gates_t, gate_bias.reshape(ML_GATES), state)
    return mlstm_output(h2, p_o, xc, gain, skip), (cf, nf, mf)


PROJ_SPLITS = ((0, 3 * HY_W), (3 * HY_W, 3 * HY_W + S5_W),
               (3 * HY_W + S5_W, 3 * HY_W + S5_W + ML_W),
               (3 * HY_W + S5_W + ML_W, 3 * HY_W + S5_W + 2 * ML_W),
               (3 * HY_W + S5_W + 2 * ML_W, 3 * HY_W + S5_W + 3 * ML_W),
               (3 * HY_W + S5_W + 3 * ML_W, 3 * HY_W + S5_W + 3 * ML_W + LANES))
PROJ_DTYPES = (F32, F32, F32, BF16, F32, F32)


def kernel(x, c, ctx, c_ctx, w_mod, b_mod, g_pre_mix, g_post_mix, g_pre_mlp, g_post_mlp, w_in, w_out, hy_conv, hy_w1, hy_b1, hy_w2, hy_b2, hy_w3, hy_freq, hy_decay, hy_bias, s5_a_re, s5_a_im, s5_log_dt, s5_b_re, s5_b_im, s5_c_re, s5_c_im, s5_d, s5_w_glu, s5_b_glu, ml_conv, ml_wq, ml_wk, ml_gate_bias, ml_skip, ml_norm_gain, w_mlp1, w_mlp2):
    bsz, seq_len, d = x.shape
    ctx_len = ctx.shape[1]
    depth = w_mod.shape[0]

    cc = jnp.zeros((SUBLANES, d), F32).at[:bsz].set(c).at[bsz].set(c_ctx)
    mods = mod_vectors(cc, w_mod, b_mod)

    proj_w = w_in.shape[-1]
    w_in_b = jnp.pad(w_in, ((0, 0), (0, 0), (0, PROJ_SPLITS[-1][1] - proj_w))).astype(BF16)
    w_out_b, w1_b, w2_b = w_out.astype(BF16), w_mlp1.astype(BF16), w_mlp2.astype(BF16)
    wq_b, wk_b, wglu_b = ml_wq.astype(BF16), ml_wk.astype(BF16), s5_w_glu.astype(BF16)

    s5_zero = jnp.zeros((bsz, 1, 4 * S5_GROUPS * S5_N), F32)
    ml_zero = (jnp.zeros((bsz, 2, ML_HEADS, ML_HD, ML_HD), F32),
               jnp.zeros((bsz, 2, ML_HEADS, 1, ML_HD), F32),
               jnp.zeros((bsz, 2, ML_HEADS, 1, LANES), F32))

    for l in range(depth):
        mx = [mods[l, :bsz, j * d:(j + 1) * d][:, None, :] for j in range(6)]
        mc = [jnp.broadcast_to(mods[l, bsz, j * d:(j + 1) * d][None, None, :], (bsz, 1, d)) for j in range(6)]
        filt_x = hyena_filters(seq_len, hy_w1[l], hy_b1[l], hy_w2[l], hy_b2[l], hy_w3[l], hy_freq[l], hy_decay[l])
        filt_c = hyena_filters(ctx_len, hy_w1[l], hy_b1[l], hy_w2[l], hy_b2[l], hy_w3[l], hy_freq[l], hy_decay[l])
        s5_mats = s5_matrices(s5_a_re[l], s5_a_im[l], s5_log_dt[l], s5_b_re[l], s5_b_im[l], s5_c_re[l], s5_c_im[l])
        conv2d_w = ml_conv[l].reshape(9, ML_W)

        def mix(h_in, mod, L, fft_len, filt, width, rows, s5_h0, ml_state):
            p_hy, p_s5, p_mx, p_mv, p_mo, p_mg = in_projection(h_in, g_pre_mix[l], mod[1], mod[0], w_in_b[l],
                                                               PROJ_SPLITS, PROJ_DTYPES)
            y_hy = hyena_mix(p_hy, hy_conv[l], filt, hy_bias[l], fft_len)
            y_s5, s5_fin = s5_mix(p_s5, s5_mats, s5_d[l], wglu_b[l], s5_b_glu[l], s5_h0)
            y_ml, ml_fin = mlstm_mix(p_mx, p_mv, p_mo, p_mg, conv2d_w, width, rows, wq_b[l], wk_b[l],
                                     ml_gate_bias[l], ml_skip[l], ml_norm_gain[l], ml_state)
            return (y_hy, y_s5, y_ml), s5_fin, ml_fin

        y_c, s5_state, ml_state = mix(ctx, mc, ctx_len, CTX_FFT_LEN, filt_c, ctx_len, (0,), s5_zero, ml_zero)
        y_x, _, _ = mix(x, mx, seq_len, seq_len, filt_x, GRID_W, (-1, 0, 1), s5_state, ml_state)
        x = out_projection(*y_x, w_out_b[l], g_post_mix[l], mx[2], x)
        x = mlp_block(x, g_pre_mlp[l], mx[4], mx[3], w1_b[l], w2_b[l], g_post_mlp[l], mx[5])
        if l < depth - 1:
            ctx = out_projection(*y_c, w_out_b[l], g_post_mix[l], mc[2], ctx)
            ctx = mlp_block(ctx, g_pre_mlp[l], mc[4], mc[3], w1_b[l], w2_b[l], g_post_mlp[l], mc[5])
    return x
```

```python
import functools
import math

import numpy as np
import jax
import jax.numpy as jnp
from jax import lax
from jax.experimental import pallas as pl
from jax.experimental.pallas import tpu as pltpu

F32 = jnp.float32
BF16 = jnp.bfloat16
EPS = 1e-6
HIGHEST = lax.Precision.HIGHEST

V7X_VMEM_BYTES = 64 * 1024 * 1024
VMEM_LIMIT = V7X_VMEM_BYTES - 8 * 1024 * 1024
LANES = 128
SUBLANES = 8
FFT_N2 = 64
FFT_UNROLL_A = 4
FFT_UNROLL_B = 8

HY_W = 256
HY_BANDS = 8
HY_EMB = 1 + 2 * HY_BANDS
HY_FF = 64
S5_W = 256
S5_CH = 16
S5_GROUPS = S5_W // S5_CH
S5_N = 64
S5_T = 8
ML_HEADS = 4
ML_HD = 128
ML_W = ML_HEADS * ML_HD
ML_CHUNK = 64
ML_GATES = 4 * ML_HEADS
GRID_W = 64
CTX_FFT_LEN = 512


def _cparams(sem):
    return pltpu.CompilerParams(dimension_semantics=sem, vmem_limit_bytes=VMEM_LIMIT)


def _dot(a, b):
    return jnp.dot(a, b, preferred_element_type=F32)


def _dot_nt(a, b):
    return lax.dot_general(a, b, (((1,), (1,)), ((), ())), preferred_element_type=F32)


def _rms(x, g):
    return x * lax.rsqrt(jnp.mean(x * x, axis=-1, keepdims=True) + EPS) * g


def _silu(x):
    return x * jax.nn.sigmoid(x)


def _mod_kernel(c_ref, w_ref, b_ref, o_ref):
    s = _silu(c_ref[...]).astype(BF16)
    o_ref[0] = _dot(s, w_ref[0].astype(BF16)) + b_ref[0]


def mod_vectors(cc, w_mod, b_mod):
    depth, d, n = w_mod.shape
    r = cc.shape[0]
    tn = 1536
    return pl.pallas_call(
        _mod_kernel,
        grid=(depth, n // tn),
        in_specs=[pl.BlockSpec((r, d), lambda l, j: (0, 0)),
                  pl.BlockSpec((1, d, tn), lambda l, j: (l, 0, j)),
                  pl.BlockSpec((1, 1, tn), lambda l, j: (l, 0, j))],
        out_specs=pl.BlockSpec((1, r, tn), lambda l, j: (l, 0, j)),
        out_shape=jax.ShapeDtypeStruct((depth, r, n), F32),
        compiler_params=_cparams(("arbitrary", "arbitrary")),
        name="mod_vectors",
    )(cc, w_mod, b_mod.reshape(depth, 1, n))


def _row_tile(L):
    return min(L, 512)


def _inproj_kernel(x_ref, g_ref, sc_ref, sh_ref, w_ref, *o_refs, splits):
    h = _rms(x_ref[0], g_ref[...]) * (1.0 + sc_ref[0]) + sh_ref[0]
    hb = h.astype(BF16)
    for o_ref, (a, b) in zip(o_refs, splits):
        o_ref[0] = _dot(hb, w_ref[:, a:b]).astype(o_ref.dtype)


def in_projection(x, g, sc, sh, w, splits, dtypes):
    bn, L, d = x.shape
    tm = _row_tile(L)
    vec = pl.BlockSpec((1, 1, d), lambda b, i: (b, 0, 0))
    return pl.pallas_call(
        functools.partial(_inproj_kernel, splits=splits),
        grid=(bn, L // tm),
        in_specs=[pl.BlockSpec((1, tm, d), lambda b, i: (b, i, 0)),
                  pl.BlockSpec((1, d), lambda b, i: (0, 0)), vec, vec,
                  pl.BlockSpec(w.shape, lambda b, i: (0, 0))],
        out_specs=[pl.BlockSpec((1, tm, b_ - a_), lambda b, i: (b, i, 0)) for a_, b_ in splits],
        out_shape=[jax.ShapeDtypeStruct((bn, L, b_ - a_), dt) for (a_, b_), dt in zip(splits, dtypes)],
        compiler_params=_cparams(("arbitrary", "arbitrary")),
        name="in_projection",
    )(x, g.reshape(1, d), sc, sh, w)


def _outproj_kernel(yh_ref, ys0_ref, ys1_ref, ym_ref, w_ref, g_ref, gt_ref, x_ref, o_ref):
    a = yh_ref.shape[-1]
    acc = _dot(yh_ref[0].astype(BF16), w_ref[0:a])
    acc += _dot(ys0_ref[0, 0].astype(BF16), w_ref[a:a + LANES])
    acc += _dot(ys1_ref[0, 0].astype(BF16), w_ref[a + LANES:a + 2 * LANES])
    acc += _dot(ym_ref[0].astype(BF16), w_ref[a + 2 * LANES:])
    o_ref[0] = x_ref[0] + gt_ref[0] * _rms(acc, g_ref[...])


def out_projection(y_hy, y_s5, y_ml, w, g, gate, x):
    bn, L, d = x.shape
    tm = _row_tile(L)
    tok = lambda wd: pl.BlockSpec((1, tm, wd), lambda b, i: (b, i, 0))
    half = lambda h: pl.BlockSpec((1, 1, tm, LANES), lambda b, i: (h, b, i, 0))
    return pl.pallas_call(
        _outproj_kernel,
        grid=(bn, L // tm),
        in_specs=[tok(y_hy.shape[-1]), half(0), half(1), tok(y_ml.shape[-1]),
                  pl.BlockSpec(w.shape, lambda b, i: (0, 0)),
                  pl.BlockSpec((1, d), lambda b, i: (0, 0)),
                  pl.BlockSpec((1, 1, d), lambda b, i: (b, 0, 0)), tok(d)],
        out_specs=tok(d),
        out_shape=jax.ShapeDtypeStruct((bn, L, d), F32),
        compiler_params=_cparams(("arbitrary", "arbitrary")),
        name="out_projection",
    )(y_hy, y_s5, y_s5, y_ml, w, g.reshape(1, d), gate, x)


def _mlp_kernel(x_ref, g1_ref, sc_ref, sh_ref, w1_ref, w2_ref, g2_ref, gt_ref, o_ref, h_ref, acc_ref):
    j = pl.program_id(2)

    @pl.when(j == 0)
    def _():
        h = _rms(x_ref[0], g1_ref[...]) * (1.0 + sc_ref[0]) + sh_ref[0]
        h_ref[...] = h.astype(BF16)
        acc_ref[...] = jnp.zeros_like(acc_ref)

    a = jnp.maximum(_dot(h_ref[...], w1_ref[...]), 0.0)
    acc_ref[...] += _dot((a * a).astype(BF16), w2_ref[...])

    @pl.when(j == pl.num_programs(2) - 1)
    def _():
        o_ref[0] = x_ref[0] + gt_ref[0] * _rms(acc_ref[...], g2_ref[...])


def mlp_block(x, g_pre, sc, sh, w1, w2, g_post, gate):
    bn, L, d = x.shape
    hid = w1.shape[1]
    tm = _row_tile(L)
    th = 2048
    tok = pl.BlockSpec((1, tm, d), lambda b, i, j: (b, i, 0))
    vec = pl.BlockSpec((1, 1, d), lambda b, i, j: (b, 0, 0))
    gain = pl.BlockSpec((1, d), lambda b, i, j: (0, 0))
    return pl.pallas_call(
        _mlp_kernel,
        grid=(bn, L // tm, hid // th),
        in_specs=[tok, gain, vec, vec,
                  pl.BlockSpec((d, th), lambda b, i, j: (0, j)),
                  pl.BlockSpec((th, d), lambda b, i, j: (j, 0)),
                  gain, vec],
        out_specs=tok,
        out_shape=jax.ShapeDtypeStruct((bn, L, d), F32),
        scratch_shapes=[pltpu.VMEM((tm, d), BF16), pltpu.VMEM((tm, d), F32)],
        compiler_params=_cparams(("arbitrary", "arbitrary", "arbitrary")),
        name="mlp_block",
    )(x, g_pre.reshape(1, d), sc, sh, w1, w2, g_post.reshape(1, d), gate)


def _dwconv_kernel(x_ref, w_ref, o_ref, pad_ref, *, width, taps, act, pad):
    L = x_ref.shape[1]
    ch = x_ref.shape[2]
    pad_ref[0:pad, :] = jnp.zeros((pad, ch), F32)
    pad_ref[pad + L:pad + L + pad, :] = jnp.zeros((pad, ch), F32)
    pad_ref[pad:pad + L, :] = x_ref[0].astype(F32)
    tr = min(L, 256)
    for r0 in range(0, L, tr):
        col = (lax.broadcasted_iota(jnp.int32, (tr, ch), 0) + r0) & (width - 1)
        acc = jnp.zeros((tr, ch), F32)
        for dr, dc in taps:
            start = pad + r0 + dr * width + dc
            v = pad_ref[start:start + tr, :]
            if dc == -1:
                v = jnp.where(col >= 1, v, 0.0)
            elif dc == 1:
                v = jnp.where(col <= width - 2, v, 0.0)
            acc = acc + w_ref[(dr + 1) * 3 + (dc + 1):(dr + 1) * 3 + (dc + 2), :] * v
        o_ref[0, r0:r0 + tr, :] = act(acc)


def depthwise_conv(x, w9, width, rows, act=None):
    bn, L, ch = x.shape
    assert width & (width - 1) == 0
    taps = tuple((dr, dc) for dr in rows for dc in (-1, 0, 1))
    pad = -(-(width + 1) // SUBLANES) * SUBLANES if len(rows) > 1 else SUBLANES
    act = act or (lambda a: a)
    cs = LANES
    return pl.pallas_call(
        functools.partial(_dwconv_kernel, width=width, taps=taps, act=act, pad=pad),
        grid=(bn, ch // cs),
        in_specs=[pl.BlockSpec((1, L, cs), lambda b, c: (b, 0, c)),
                  pl.BlockSpec((9, cs), lambda b, c: (0, c))],
        out_specs=pl.BlockSpec((1, L, cs), lambda b, c: (b, 0, c)),
        out_shape=jax.ShapeDtypeStruct((bn, L, ch), F32),
        scratch_shapes=[pltpu.VMEM((L + 2 * pad, cs), F32)],
        compiler_params=_cparams(("arbitrary", "arbitrary")),
        name="depthwise_conv",
    )(x, w9)


def _hyfilt_kernel(w1_ref, b1_ref, w2_ref, b2_ref, w3_ref, fr_ref, dec_ref, o_ref, *, L):
    tr = min(L, 512)
    wd = o_ref.shape[1]
    fr = fr_ref[...]
    ssq = jnp.zeros((1, wd), F32)
    for r0 in range(0, L, tr):
        t = (lax.broadcasted_iota(jnp.int32, (tr, 32), 0) + r0).astype(F32) / L
        lane = lax.broadcasted_iota(jnp.int32, (tr, 32), 1)
        band = jnp.where(lane <= HY_BANDS, lane, lane - HY_BANDS).astype(F32)
        ang = 2.0 * math.pi * t * band
        feat = jnp.where(lane == 0, t, jnp.where(lane <= HY_BANDS, jnp.cos(ang),
                                                 jnp.where(lane <= 2 * HY_BANDS, jnp.sin(ang), 0.0)))
        hdn = jnp.sin(fr * (jnp.dot(feat, w1_ref[...], precision=HIGHEST, preferred_element_type=F32) + b1_ref[...]))
        hdn = jnp.sin(fr * (jnp.dot(hdn, w2_ref[...], precision=HIGHEST, preferred_element_type=F32) + b2_ref[...]))
        filt = _dot(hdn.astype(BF16), w3_ref[...].astype(BF16))
        filt = filt * jnp.exp(-t[:, 0:1] * dec_ref[...])
        ssq = ssq + jnp.sum(filt * filt, axis=0, keepdims=True)
        o_ref[r0:r0 + tr, :] = filt
    scales = []
    for o in range(wd // (2 * HY_W)):
        tot = ssq[:, 2 * o * HY_W:(2 * o + 1) * HY_W] + ssq[:, (2 * o + 1) * HY_W:(2 * o + 2) * HY_W]
        scales += [lax.rsqrt(tot + EPS)] * 2
    scale = jnp.concatenate(scales, axis=1)
    for r0 in range(0, L, tr):
        o_ref[r0:r0 + tr, :] = o_ref[r0:r0 + tr, :] * scale


def hyena_filters(L, w1, b1, w2, b2, w3, freq, decay):
    wd = decay.shape[0] * 2 * HY_W
    w1p = jnp.zeros((32, HY_FF), F32).at[:HY_EMB].set(w1)
    full = lambda a: pl.BlockSpec(a.shape, lambda o: (0,) * a.ndim)
    args = (w1p, b1.reshape(1, HY_FF), w2, b2.reshape(1, HY_FF), w3, freq.reshape(1, HY_FF), decay.reshape(1, wd))
    return pl.pallas_call(
        functools.partial(_hyfilt_kernel, L=L),
        grid=(1,),
        in_specs=[full(a) for a in args],
        out_specs=pl.BlockSpec((L, wd), lambda o: (0, 0)),
        out_shape=jax.ShapeDtypeStruct((L, wd), F32),
        compiler_params=_cparams(("arbitrary",)),
        name="hyena_filters",
    )(*args)


@functools.lru_cache(maxsize=None)
def _dft_consts(L):
    n = 2 * L
    n2c = FFT_N2
    n1c = n // n2c
    n1h = n1c // 2
    n1 = np.arange(n1h)[None, None, :]
    k1 = np.arange(n1c)[None, :, None]
    n2 = np.arange(n2c)[:, None, None]
    ang = -2.0 * np.pi * (k1 * n1 / n1c + n2 * k1 / n)
    fr, fi = np.cos(ang), np.sin(ang)
    fa = np.concatenate([np.concatenate([fr, -fi], axis=2),
                         np.concatenate([fi, fr], axis=2)], axis=1)
    frt = np.transpose(fr, (0, 2, 1)) / n
    fit = -np.transpose(fi, (0, 2, 1)) / n
    fai = np.concatenate([np.concatenate([frt, -fit], axis=2),
                          np.concatenate([fit, frt], axis=2)], axis=1)
    k2 = np.arange(n2c)[:, None]
    m2 = np.arange(n2c)[None, :]
    angb = -2.0 * np.pi * k2 * m2 / n2c
    gr, gi = np.cos(angb), np.sin(angb)
    gb = np.block([[gr, -gi], [gi, gr]])
    gbi = np.block([[gr, gi], [-gi, gr]])
    return (fa.astype(np.float32), gb.astype(np.float32), gbi.astype(np.float32), fai.astype(np.float32))


def _slab_rows(n1c):
    return 2 * n1c + SUBLANES


def _slab(n2, n1c):
    return pl.ds(pl.multiple_of(n2 * _slab_rows(n1c), SUBLANES), 2 * n1c)


def _fft_stage_a(load_rhs, fa_ref, s_ref, n1c):
    def body(n2, carry):
        rhs = load_rhs(n2).astype(BF16)
        s_ref[_slab(n2, n1c), :] = _dot(fa_ref[n2], rhs)
        return carry
    lax.fori_loop(0, FFT_N2, body, 0, unroll=FFT_UNROLL_A)


def _fft_load_k1(s_ref, k1, n1c):
    xr = s_ref[pl.ds(k1, FFT_N2, stride=_slab_rows(n1c)), :]
    xi = s_ref[pl.ds(n1c + k1, FFT_N2, stride=_slab_rows(n1c)), :]
    return jnp.concatenate([xr, xi], axis=0)


def _fftconv_kernel(z_ref, gate_ref, bias_ref, h_ref, fa_ref, gb_ref, gbi_ref, fai_ref, o_ref, s_ref, *, L):
    n1c = L // 32
    n1h = n1c // 2
    half = FFT_N2

    def load_rhs(n2):
        za = z_ref[0, 0, pl.ds(n2, n1h, stride=FFT_N2), :]
        zb = z_ref[0, 1, pl.ds(n2, n1h, stride=FFT_N2), :]
        return jnp.concatenate([za, zb], axis=0)
    _fft_stage_a(load_rhs, fa_ref, s_ref, n1c)

    def stage_b(k1, carry):
        x = _dot(gb_ref[...], _fft_load_k1(s_ref, k1, n1c).astype(BF16))
        xr, xi = x[:half], x[half:]
        h = h_ref[k1]
        hr, hi = h[:half], h[half:]
        y = jnp.concatenate([xr * hr - xi * hi, xr * hi + xi * hr], axis=0)
        bp = _dot(gbi_ref[...], y.astype(BF16))
        s_ref[pl.ds(k1, FFT_N2, stride=_slab_rows(n1c)), :] = bp[:half]
        s_ref[pl.ds(n1c + k1, FFT_N2, stride=_slab_rows(n1c)), :] = bp[half:]
        return carry
    lax.fori_loop(0, n1c, stage_b, 0, unroll=FFT_UNROLL_B)

    bias = bias_ref[...]

    def stage_a_inv(n2, carry):
        rhs = s_ref[_slab(n2, n1c), :].astype(BF16)
        res = _dot(fai_ref[n2], rhs)
        for p in range(2):
            rows = pl.ds(n2, n1h, stride=FFT_N2)
            zin = z_ref[0, p, rows, :]
            o_ref[0, p, rows, :] = gate_ref[0, p, rows, :] * (res[p * n1h:(p + 1) * n1h] + bias * zin)
        return carry
    lax.fori_loop(0, FFT_N2, stage_a_inv, 0, unroll=FFT_UNROLL_A)


def fft_gated_conv(z, zoff, gate, goff, bias, hspec):
    bsz, L, _ = z.shape
    ch = bias.shape[0]
    n1c = L // 32
    fa, gb, gbi, fai = (jnp.asarray(a, BF16) for a in _dft_consts(L))
    zp = z.reshape(bsz // 2, 2, L, z.shape[-1])
    gp = gate.reshape(bsz // 2, 2, L, gate.shape[-1])
    cs = LANES
    one = pl.Buffered(1)
    const3 = lambda a: pl.BlockSpec(a.shape, lambda c, p: (0, 0, 0), pipeline_mode=one)
    const2 = lambda a: pl.BlockSpec(a.shape, lambda c, p: (0, 0), pipeline_mode=one)
    out = pl.pallas_call(
        functools.partial(_fftconv_kernel, L=L),
        grid=(ch // cs, bsz // 2),
        in_specs=[pl.BlockSpec((1, 2, L, cs), lambda c, p: (p, 0, 0, c + zoff)),
                  pl.BlockSpec((1, 2, L, cs), lambda c, p: (p, 0, 0, c + goff)),
                  pl.BlockSpec((1, cs), lambda c, p: (0, c)),
                  pl.BlockSpec((n1c, 2 * FFT_N2, cs), lambda c, p: (0, 0, c), pipeline_mode=one),
                  const3(fa), const2(gb), const2(gbi), const3(fai)],
        out_specs=pl.BlockSpec((1, 2, L, cs), lambda c, p: (p, 0, 0, c)),
        out_shape=jax.ShapeDtypeStruct((bsz // 2, 2, L, ch), F32),
        scratch_shapes=[pltpu.VMEM((FFT_N2 * _slab_rows(n1c), cs), F32)],
        compiler_params=_cparams(("arbitrary", "arbitrary")),
        name="fft_gated_conv",
    )(zp, gp, bias.reshape(1, ch), hspec, fa, gb, gbi, fai)
    return out.reshape(bsz, L, ch)


def _fftspec_kernel(hf_ref, hb_ref, fa_ref, gb_ref, o_ref, s_ref, *, L):
    n1c = L // 32
    n1h = n1c // 2
    half = FFT_N2
    for d, src in enumerate((hf_ref, hb_ref)):
        def load_rhs(n2, src=src, d=d):
            h = src[pl.ds(n2, n1h, stride=FFT_N2), :]
            if d == 1:
                row = lax.broadcasted_iota(jnp.int32, h.shape, 0)
                h = jnp.where((row == 0) & (n2 == 0), 0.0, h)
            return jnp.concatenate([h, jnp.zeros_like(h)], axis=0)
        _fft_stage_a(load_rhs, fa_ref, s_ref, n1c)

        def stage_b(k1, carry, d=d):
            x = _dot(gb_ref[...], _fft_load_k1(s_ref, k1, n1c).astype(BF16))
            if d == 0:
                o_ref[k1] = x
            else:
                o_ref[k1] = o_ref[k1] + jnp.concatenate([x[:half], -x[half:]], axis=0)
            return carry
        lax.fori_loop(0, n1c, stage_b, 0, unroll=FFT_UNROLL_B)


def fft_filter_spectrum(filt, foff, boff, L):
    if filt.shape[0] < L:
        filt = jnp.pad(filt, ((0, L - filt.shape[0]), (0, 0)))
    n1c = L // 32
    fa, gb, _, _ = (jnp.asarray(a, BF16) for a in _dft_consts(L))
    cs = LANES
    return pl.pallas_call(
        functools.partial(_fftspec_kernel, L=L),
        grid=(HY_W // cs,),
        in_specs=[pl.BlockSpec((L, cs), lambda c: (0, c + foff)), pl.BlockSpec((L, cs), lambda c: (0, c + boff)),
                  pl.BlockSpec(fa.shape, lambda c: (0, 0, 0)), pl.BlockSpec(gb.shape, lambda c: (0, 0))],
        out_specs=pl.BlockSpec((n1c, 2 * FFT_N2, cs), lambda c: (0, 0, c)),
        out_shape=jax.ShapeDtypeStruct((n1c, 2 * FFT_N2, HY_W), F32),
        scratch_shapes=[pltpu.VMEM((FFT_N2 * _slab_rows(n1c), cs), F32)],
        compiler_params=_cparams(("arbitrary",)),
        name="fft_filter_spectrum",
    )(filt, filt, fa, gb)


def hyena_mix(p_hy, conv_w, filt, bias, fft_len):
    bsz, L, _ = p_hy.shape
    w9 = jnp.zeros((9, 3 * HY_W), F32).at[3:6].set(conv_w)
    u = depthwise_conv(p_hy, w9, width=L, rows=(0,))
    if fft_len > L:
        u = jnp.pad(u, ((0, 0), (0, fft_len - L), (0, 0)))
    nb = HY_W // LANES
    z = fft_gated_conv(u, 0, u, nb, bias[0], fft_filter_spectrum(filt, 0, nb, fft_len))
    z = fft_gated_conv(z, 0, u, 2 * nb, bias[1], fft_filter_spectrum(filt, 2 * nb, 3 * nb, fft_len))
    return z[:, :L]


def s5_matrices(a_re, a_im, log_dt, b_re, b_im, c_re, c_im):
    t = S5_T
    sw = S5_GROUPS * S5_N
    dt = jnp.exp(log_dt)[..., None]
    mag = jnp.exp(a_re * dt)
    ar, ai = mag * jnp.cos(a_im * dt), mag * jnp.sin(a_im * dt)
    den = a_re * a_re + a_im * a_im
    qr = ((ar - 1.0) * a_re + ai * a_im) / den
    qi = (ai * a_re - (ar - 1.0) * a_im) / den
    bbr = qr[..., None] * b_re - qi[..., None] * b_im
    bbi = qr[..., None] * b_im + qi[..., None] * b_re
    pr, pi = [jnp.ones_like(ar)], [jnp.zeros_like(ai)]
    for _ in range(t):
        pr, pi = pr + [pr[-1] * ar - pi[-1] * ai], pi + [pr[-1] * ai + pi[-1] * ar]
    pr = [p.reshape(2, sw) for p in pr]
    pi = [p.reshape(2, sw) for p in pi]
    gd_hn = (np.arange(S5_W)[:, None] // S5_CH == np.arange(sw)[None, :] // S5_N).astype(np.float32)
    gn_hc = gd_hn.T
    expand_b = lambda b: jnp.tile(jnp.swapaxes(b, -1, -2).reshape(2, S5_W, S5_N), (1, 1, S5_GROUPS)) * gd_hn
    expand_c = lambda c: jnp.tile(jnp.swapaxes(c, -1, -2).reshape(2, sw, S5_CH), (1, 1, S5_GROUPS)) * gn_hc
    b_r, b_i = expand_b(bbr), expand_b(bbi)
    c_r, c_i = expand_c(c_re), expand_c(c_im)
    col = lambda p, x: p[x][None, :]
    rowv = lambda p, x: p[x][:, None]
    r_in = jnp.concatenate([
        jnp.concatenate([b_r[0] * col(pr[t - 1 - s], 0) - b_i[0] * col(pi[t - 1 - s], 0),
                         b_r[0] * col(pi[t - 1 - s], 0) + b_i[0] * col(pr[t - 1 - s], 0),
                         b_r[1] * col(pr[s], 1) - b_i[1] * col(pi[s], 1),
                         b_r[1] * col(pi[s], 1) + b_i[1] * col(pr[s], 1)], axis=1)
        for s in range(t)], axis=0)

    def scaled_c(x, j):
        return (c_r[x] * rowv(pr[j], x) - c_i[x] * rowv(pi[j], x), c_r[x] * rowv(pi[j], x) + c_i[x] * rowv(pr[j], x))
    blocks = []
    for tt in range(t):
        fr, fi = scaled_c(0, tt + 1)
        br, bi = scaled_c(1, t - tt)
        blocks.append(jnp.concatenate([fr, -fi, br, -bi], axis=0))
    r_out = jnp.concatenate(blocks, axis=1)
    lags = (list(range(t)), list(range(t - 1, -1, -1)))
    taps = []
    for x in range(2):
        sc = [scaled_c(x, j) for j in lags[x]]
        taps.append(s5_taps(b_r[x], b_i[x], jnp.concatenate([p[0] for p in sc], axis=1),
                            jnp.concatenate([p[1] for p in sc], axis=1)))
    w = S5_W
    m_intra = jnp.concatenate([
        jnp.pad(taps[0][:, :(t - s) * w], ((0, 0), (s * w, 0)))
        + jnp.pad(taps[1][:, (t - 1 - s) * w:], ((0, 0), (0, (t - 1 - s) * w)))
        for s in range(t)], axis=0)
    a8v = jnp.concatenate([pr[t][0], pi[t][0], pr[t][1], pi[t][1]])[None, :]
    return m_intra.astype(BF16), r_in.astype(BF16), r_out.astype(BF16), a8v


def _s5taps_kernel(br_ref, bi_ref, cr_ref, ci_ref, o_ref):
    o_ref[...] = (_dot(br_ref[...].astype(BF16), cr_ref[...].astype(BF16))
                  - _dot(bi_ref[...].astype(BF16), ci_ref[...].astype(BF16)))


def s5_taps(b_r, b_i, c_r, c_i):
    m, n = b_r.shape[0], c_r.shape[1]
    tn = 512
    lhs = pl.BlockSpec(b_r.shape, lambda j: (0, 0))
    rhs = pl.BlockSpec((c_r.shape[0], tn), lambda j: (0, j))
    return pl.pallas_call(
        _s5taps_kernel,
        grid=(n // tn,),
        in_specs=[lhs, lhs, rhs, rhs],
        out_specs=pl.BlockSpec((m, tn), lambda j: (0, j)),
        out_shape=jax.ShapeDtypeStruct((m, n), F32),
        compiler_params=_cparams(("arbitrary",)),
        name="s5_taps",
    )(b_r, b_i, c_r, c_i)


def _row_tokens(x0_ref, x1_ref, t, rows):
    sl = pl.ds(t, rows, stride=S5_T)
    return jnp.concatenate([x0_ref[0, sl, :], x1_ref[0, sl, :]], axis=1)


def _s5in_kernel(x0_ref, x1_ref, w_ref, o_ref):
    rows = o_ref.shape[1]
    wd = 2 * LANES
    acc = _dot(_row_tokens(x0_ref, x1_ref, 0, rows).astype(BF16), w_ref[0:wd, :])
    for t in range(1, S5_T):
        acc += _dot(_row_tokens(x0_ref, x1_ref, t, rows).astype(BF16), w_ref[t * wd:(t + 1) * wd, :])
    o_ref[0] = acc


def s5_state_inputs(x, w, tn):
    bn, L, wd = x.shape
    assert wd == 2 * LANES
    r = L // S5_T
    n = w.shape[1]
    tm = min(r, 256)
    half = lambda h: pl.BlockSpec((1, tm * S5_T, LANES), lambda j, b, i: (b, i, h))
    return pl.pallas_call(
        _s5in_kernel,
        grid=(n // tn, bn, r // tm),
        in_specs=[half(0), half(1), pl.BlockSpec((S5_T * wd, tn), lambda j, b, i: (0, j))],
        out_specs=pl.BlockSpec((1, tm, tn), lambda j, b, i: (b, i, j)),
        out_shape=jax.ShapeDtypeStruct((bn, r, n), F32),
        compiler_params=_cparams(("arbitrary", "arbitrary", "arbitrary")),
        name="s5_state_inputs",
    )(x, x, w)


def _s5out_kernel(x0_ref, x1_ref, m_ref, s_ref, r_ref, d_ref, wg_ref, bg_ref, o_ref):
    rows = s_ref.shape[1]
    wd = 2 * LANES
    xt = [_row_tokens(x0_ref, x1_ref, t, rows) for t in range(S5_T)]
    acc = _dot(s_ref[0].astype(BF16), r_ref[...])
    for t in range(S5_T):
        acc += _dot(xt[t].astype(BF16), m_ref[t * wd:(t + 1) * wd, :])
    for t in range(S5_T):
        g = jax.nn.gelu(acc[:, t * wd:(t + 1) * wd] + d_ref[...] * xt[t], approximate=True)
        y = g * jax.nn.sigmoid(_dot(g.astype(BF16), wg_ref[...]) + bg_ref[...])
        o_ref[0, 0, pl.ds(t, rows, stride=S5_T), :] = y[:, :LANES]
        o_ref[1, 0, pl.ds(t, rows, stride=S5_T), :] = y[:, LANES:]


def s5_readout_glu(x, m_intra, xs, r_out, d_skip, w_glu, b_glu):
    bn, L, wd = x.shape
    assert wd == 2 * LANES
    r = L // S5_T
    tm = min(r, 256)
    one = pl.Buffered(1)
    half = lambda h: pl.BlockSpec((1, tm * S5_T, LANES), lambda b, i: (b, i, h))
    row = pl.BlockSpec((1, wd), lambda b, i: (0, 0))
    return pl.pallas_call(
        _s5out_kernel,
        grid=(bn, r // tm),
        in_specs=[half(0), half(1), pl.BlockSpec(m_intra.shape, lambda b, i: (0, 0), pipeline_mode=one),
                  pl.BlockSpec((1, tm, xs.shape[-1]), lambda b, i: (b, i, 0)),
                  pl.BlockSpec(r_out.shape, lambda b, i: (0, 0), pipeline_mode=one),
                  row, pl.BlockSpec((wd, wd), lambda b, i: (0, 0)), row],
        out_specs=pl.BlockSpec((2, 1, tm * S5_T, LANES), lambda b, i: (0, b, i, 0)),
        out_shape=jax.ShapeDtypeStruct((2, bn, L, LANES), F32),
        compiler_params=_cparams(("arbitrary", "arbitrary")),
        name="s5_readout_glu",
    )(x, x, m_intra, xs, r_out, d_skip.reshape(1, wd), w_glu, b_glu.reshape(1, wd))


def _s5scan_kernel(r_ref, a_ref, h0_ref, x_ref, fin_ref):
    nk = r_ref.shape[1]
    sw = S5_GROUPS * S5_N
    ar_f, ai_f = a_ref[:, 0:sw], a_ref[:, sw:2 * sw]
    ar_b, ai_b = a_ref[:, 2 * sw:3 * sw], a_ref[:, 3 * sw:4 * sw]
    init = tuple(h0_ref[0, :, j * sw:(j + 1) * sw] for j in range(4))

    def body(i, carry):
        fr, fi, br, bi = carry
        k = nk - 1 - i
        x_ref[0, pl.ds(i, 1), 0:sw] = fr
        x_ref[0, pl.ds(i, 1), sw:2 * sw] = fi
        x_ref[0, pl.ds(k, 1), 2 * sw:3 * sw] = br
        x_ref[0, pl.ds(k, 1), 3 * sw:4 * sw] = bi
        rfr = r_ref[0, pl.ds(i, 1), 0:sw]
        rfi = r_ref[0, pl.ds(i, 1), sw:2 * sw]
        rbr = r_ref[0, pl.ds(k, 1), 2 * sw:3 * sw]
        rbi = r_ref[0, pl.ds(k, 1), 3 * sw:4 * sw]
        return (ar_f * fr - ai_f * fi + rfr, ar_f * fi + ai_f * fr + rfi,
                ar_b * br - ai_b * bi + rbr, ar_b * bi + ai_b * br + rbi)
    fin = lax.fori_loop(0, nk, body, init)
    for j in range(4):
        fin_ref[0, :, j * sw:(j + 1) * sw] = fin[j]


def s5_row_scan(r, a8v, h0):
    bn, nk, w = r.shape
    return pl.pallas_call(
        _s5scan_kernel,
        grid=(bn,),
        in_specs=[pl.BlockSpec((1, nk, w), lambda b: (b, 0, 0)),
                  pl.BlockSpec((1, w), lambda b: (0, 0)),
                  pl.BlockSpec((1, 1, w), lambda b: (b, 0, 0))],
        out_specs=[pl.BlockSpec((1, nk, w), lambda b: (b, 0, 0)),
                   pl.BlockSpec((1, 1, w), lambda b: (b, 0, 0))],
        out_shape=[jax.ShapeDtypeStruct((bn, nk, w), F32), jax.ShapeDtypeStruct((bn, 1, w), F32)],
        compiler_params=_cparams(("arbitrary",)),
        name="s5_row_scan",
    )(r, a8v, h0)


def s5_mix(p_s5, mats, d_skip, w_glu, b_glu, h0):
    m_intra, r_in, r_out, a8v = mats
    r = s5_state_inputs(p_s5, r_in, tn=1024)
    xs, fin = s5_row_scan(r, a8v, h0)
    return s5_readout_glu(p_s5, m_intra, xs, r_out, d_skip, w_glu, b_glu), fin


def _qk_kernel(x_ref, wq_ref, wk_ref, q_ref, k_ref):
    for h in range(ML_HEADS):
        sl = slice(h * ML_HD, (h + 1) * ML_HD)
        xb = x_ref[0, :, sl].astype(BF16)
        q_ref[0, :, sl] = _dot(xb, wq_ref[h]).astype(BF16)
        k_ref[0, :, sl] = (_dot(xb, wk_ref[h]) * (ML_HD ** -0.5)).astype(BF16)


def mlstm_qk(xc, wq, wk):
    bn, L, w = xc.shape
    tm = _row_tile(L)
    tok = pl.BlockSpec((1, tm, w), lambda b, i: (b, i, 0))
    wsp = pl.BlockSpec(wq.shape, lambda b, i: (0, 0, 0))
    return pl.pallas_call(
        _qk_kernel,
        grid=(bn, L // tm),
        in_specs=[tok, wsp, wsp],
        out_specs=[tok, tok],
        out_shape=[jax.ShapeDtypeStruct((bn, L, w), BF16)] * 2,
        compiler_params=_cparams(("arbitrary", "arbitrary")),
        name="mlstm_qk",
    )(xc, wq, wk)


def _mlstm_kernel(bias_ref, q_ref, k_ref, v_ref, g_ref, c0_ref, n0_ref, m0_ref,
                  h_ref, cf_ref, nf_ref, mf_ref, st_s, m_s, lf_s, b_s):
    hd = pl.program_id(0)
    bn = q_ref.shape[0]
    nc = g_ref.shape[2]
    t = ML_CHUNK
    chains = [(b, d) for b in range(bn) for d in range(2)]
    for i, (b, d) in enumerate(chains):
        st_s[i, :, 0:ML_HD] = c0_ref[b, d, 0].T
        st_s[i, :, ML_HD:] = jnp.broadcast_to(n0_ref[b, d, 0], (ML_HD, ML_HD)).T
        m_s[i] = m0_ref[b, d, 0]
    ones_blk = jnp.ones((t, ML_HD), BF16)
    row = lax.broadcasted_iota(jnp.int32, (t, t), 0)
    col = lax.broadcasted_iota(jnp.int32, (t, t), 1)
    tri = (col <= row, col >= row)
    cum = ((row <= col).astype(F32), (row >= col).astype(F32))
    bias_i = [bias_ref[d * 2 * ML_HEADS + hd] for d in range(2)]
    bias_f = [bias_ref[d * 2 * ML_HEADS + ML_HEADS + hd] for d in range(2)]
    for i, (b, d) in enumerate(chains):
        lf_all = jax.nn.log_sigmoid(g_ref[b, d * 2 * ML_HEADS + ML_HEADS + hd] + bias_f[d])
        lf_s[i] = lf_all
        b_s[i] = jnp.dot(lf_all, cum[d], precision=HIGHEST, preferred_element_type=F32)

    def body(j, carry):
        ids = range(len(chains))
        cidx = [j if d == 0 else nc - 1 - j for _, d in chains]
        r0 = [pl.multiple_of(c * t, t) for c in cidx]
        q = [q_ref[b, pl.ds(r0[i], t), :] for i, (b, d) in enumerate(chains)]
        k = [k_ref[b, pl.ds(r0[i], t), :] for i, (b, d) in enumerate(chains)]
        v = [v_ref[b, pl.ds(r0[i], t), :] for i, (b, d) in enumerate(chains)]
        li = [g_ref[b, d * 2 * ML_HEADS + hd, pl.ds(cidx[i], 1), :] + bias_i[d] for i, (b, d) in enumerate(chains)]
        lf = [lf_s[i, pl.ds(cidx[i], 1), :] for i in ids]
        b_row = [b_s[i, pl.ds(cidx[i], 1), :] for i in ids]
        b_col = [jnp.sum(jnp.where(tri[d], lf[i], 0.0), axis=-1, keepdims=True) for i, (b, d) in enumerate(chains)]
        g = [jnp.sum(lf[i], axis=-1, keepdims=True) for i in ids]
        a_row = [g[i] - b_row[i] + li[i] for i in ids]
        m_loc = [jnp.max(a_row[i], axis=-1, keepdims=True) for i in ids]
        w_row = [jnp.exp(a_row[i] - m_loc[i]) for i in ids]
        st_prev = [st_s[i] for i in ids]
        m_prev = [m_s[i][:, 0:1] for i in ids]
        kt = [k[i].astype(F32).T for i in ids]
        qr = [_dot(q[i], jnp.concatenate([st_prev[i].astype(BF16), kt[i].astype(BF16)], axis=1)) for i in ids]
        v1 = [jnp.concatenate([v[i], ones_blk], axis=1) for i in ids]
        upd = [_dot((kt[i] * w_row[i]).astype(BF16), v1[i]) for i in ids]
        dmat = [jnp.where(tri[d], b_col[i] - b_row[i] + li[i], -jnp.inf) for i, (b, d) in enumerate(chains)]
        inter = [b_col[i] + m_prev[i] for i in ids]
        m_t = [jnp.maximum(jnp.max(dmat[i], axis=-1, keepdims=True), inter[i]) for i in ids]
        s = [qr[i][:, 2 * ML_HD:] * jnp.exp(dmat[i] - m_t[i]) for i in ids]
        w_inter = [jnp.exp(inter[i] - m_t[i]) for i in ids]
        sv = [_dot(s[i].astype(BF16), v1[i]) for i in ids]
        nq = [sv[i][:, ML_HD:ML_HD + 1] + w_inter[i] * qr[i][:, ML_HD:ML_HD + 1] for i in ids]
        for i, (b, d) in enumerate(chains):
            num = sv[i][:, :ML_HD] + w_inter[i] * qr[i][:, :ML_HD]
            h_ref[d, b, pl.ds(r0[i], t), :] = num / jnp.maximum(jnp.abs(nq[i]), jnp.exp(-m_t[i]))
        for i in ids:
            m_new = jnp.maximum(g[i] + m_prev[i], m_loc[i])
            dec = jnp.exp(g[i] + m_prev[i] - m_new)
            grow = jnp.exp(m_loc[i] - m_new)
            st_s[i] = dec * st_prev[i] + grow * upd[i]
            m_s[i] = jnp.broadcast_to(m_new, (1, LANES))
        return carry
    lax.fori_loop(0, nc, body, 0)
    for i, (b, d) in enumerate(chains):
        cf_ref[b, d, 0] = st_s[i, :, 0:ML_HD].T
        nf_ref[b, d, 0] = st_s[i, :, ML_HD:].T[0:1]
        mf_ref[b, d, 0] = m_s[i]


def mlstm_scan(q, k, v, gates_t, gate_bias, state):
    bn, L, w = q.shape
    c0, n0, m0 = state
    hsp = pl.BlockSpec((bn, L, ML_HD), lambda h: (0, 0, h), pipeline_mode=pl.Buffered(1))
    st = lambda a: pl.BlockSpec((bn, 2, 1) + a.shape[3:], lambda h: (0, 0, h, 0, 0))
    return pl.pallas_call(
        _mlstm_kernel,
        grid=(ML_HEADS,),
        in_specs=[pl.BlockSpec(memory_space=pltpu.SMEM), hsp, hsp, hsp,
                  pl.BlockSpec(gates_t.shape, lambda h: (0, 0, 0, 0)),
                  st(c0), st(n0), st(m0)],
        out_specs=[pl.BlockSpec((2, bn, L, ML_HD), lambda h: (0, 0, 0, h)), st(c0), st(n0), st(m0)],
        out_shape=[jax.ShapeDtypeStruct((2, bn, L, w), F32),
                   jax.ShapeDtypeStruct(c0.shape, F32), jax.ShapeDtypeStruct(n0.shape, F32),
                   jax.ShapeDtypeStruct(m0.shape, F32)],
        scratch_shapes=[pltpu.VMEM((2 * bn, ML_HD, 2 * ML_HD), F32),
                        pltpu.VMEM((2 * bn, 1, LANES), F32),
                        pltpu.VMEM((2 * bn,) + gates_t.shape[2:], F32),
                        pltpu.VMEM((2 * bn,) + gates_t.shape[2:], F32)],
        compiler_params=_cparams(("arbitrary",)),
        name="mlstm_scan",
    )(gate_bias, q, k, v, gates_t, c0, n0, m0)


def _mlout_kernel(hf_ref, hb_ref, o_ref, xc_ref, gain_ref, skip_ref, y_ref):
    h = hf_ref[0, 0] + hb_ref[0, 0]
    parts = []
    for hd in range(ML_HEADS):
        sl = slice(hd * ML_HD, (hd + 1) * ML_HD)
        hh = h[:, sl]
        parts.append(hh * lax.rsqrt(jnp.mean(hh * hh, axis=-1, keepdims=True) + EPS) * gain_ref[:, sl])
    hn = jnp.concatenate(parts, axis=-1)
    y_ref[0] = jax.nn.sigmoid(o_ref[0].astype(F32)) * (hn + skip_ref[...] * xc_ref[0])


def mlstm_output(h2, p_o, xc, gain, skip):
    _, bn, L, w = h2.shape
    tm = _row_tile(L)
    tok = pl.BlockSpec((1, tm, w), lambda b, i: (b, i, 0))
    row = pl.BlockSpec((1, w), lambda b, i: (0, 0))
    return pl.pallas_call(
        _mlout_kernel,
        grid=(bn, L // tm),
        in_specs=[pl.BlockSpec((1, 1, tm, w), lambda b, i: (0, b, i, 0)),
                  pl.BlockSpec((1, 1, tm, w), lambda b, i: (1, b, i, 0)), tok, tok, row, row],
        out_specs=tok,
        out_shape=jax.ShapeDtypeStruct((bn, L, w), F32),
        compiler_params=_cparams(("arbitrary", "arbitrary")),
        name="mlstm_output",
    )(h2, h2, p_o, xc, gain.reshape(1, w), skip.reshape(1, w))


def mlstm_mix(p_x, p_v, p_o, p_g, conv_w9, width, rows, wq, wk, gate_bias, skip, gain, state):
    bn, L, _ = p_x.shape
    xc = depthwise_conv(p_x, conv_w9, width=width, rows=rows, act=_silu)
    q, k = mlstm_qk(xc, wq, wk)
    gates_t = jnp.swapaxes(p_g[..., :ML_GATES], 1, 2).reshape(bn, ML_GATES, L // ML_CHUNK, ML_CHUNK)
    h2, cf, nf, mf = mlstm_scan(q, k, p_v, gates_t, gate_bias.reshape(ML_GATES), state)
    return mlstm_output(h2, p_o, xc, gain, skip), (cf, nf, mf)


PROJ_SPLITS = ((0, 3 * HY_W), (3 * HY_W, 3 * HY_W + S5_W),
               (3 * HY_W + S5_W, 3 * HY_W + S5_W + ML_W),
               (3 * HY_W + S5_W + ML_W, 3 * HY_W + S5_W + 2 * ML_W),
               (3 * HY_W + S5_W + 2 * ML_W, 3 * HY_W + S5_W + 3 * ML_W),
               (3 * HY_W + S5_W + 3 * ML_W, 3 * HY_W + S5_W + 3 * ML_W + LANES))
PROJ_DTYPES = (BF16, F32, BF16, BF16, BF16, F32)


def kernel(x, c, ctx, c_ctx, w_mod, b_mod, g_pre_mix, g_post_mix, g_pre_mlp, g_post_mlp, w_in, w_out, hy_conv, hy_w1, hy_b1, hy_w2, hy_b2, hy_w3, hy_freq, hy_decay, hy_bias, s5_a_re, s5_a_im, s5_log_dt, s5_b_re, s5_b_im, s5_c_re, s5_c_im, s5_d, s5_w_glu, s5_b_glu, ml_conv, ml_wq, ml_wk, ml_gate_bias, ml_skip, ml_norm_gain, w_mlp1, w_mlp2):
    bsz, seq_len, d = x.shape
    ctx_len = ctx.shape[1]
    depth = w_mod.shape[0]

    cc = jnp.zeros((SUBLANES, d), F32).at[:bsz].set(c).at[bsz].set(c_ctx)
    mods = mod_vectors(cc, w_mod, b_mod)

    proj_w = w_in.shape[-1]
    w_in_b = jnp.pad(w_in, ((0, 0), (0, 0), (0, PROJ_SPLITS[-1][1] - proj_w))).astype(BF16)
    w_out_b, w1_b, w2_b = w_out.astype(BF16), w_mlp1.astype(BF16), w_mlp2.astype(BF16)
    wq_b, wk_b, wglu_b = ml_wq.astype(BF16), ml_wk.astype(BF16), s5_w_glu.astype(BF16)

    s5_zero = jnp.zeros((bsz, 1, 4 * S5_GROUPS * S5_N), F32)
    ml_zero = (jnp.zeros((bsz, 2, ML_HEADS, ML_HD, ML_HD), F32),
               jnp.zeros((bsz, 2, ML_HEADS, 1, ML_HD), F32),
               jnp.zeros((bsz, 2, ML_HEADS, 1, LANES), F32))

    for l in range(depth):
        mx = [mods[l, :bsz, j * d:(j + 1) * d][:, None, :] for j in range(6)]
        mc = [jnp.broadcast_to(mods[l, bsz, j * d:(j + 1) * d][None, None, :], (bsz, 1, d)) for j in range(6)]
        filt_x = hyena_filters(seq_len, hy_w1[l], hy_b1[l], hy_w2[l], hy_b2[l], hy_w3[l], hy_freq[l], hy_decay[l])
        filt_c = hyena_filters(ctx_len, hy_w1[l], hy_b1[l], hy_w2[l], hy_b2[l], hy_w3[l], hy_freq[l], hy_decay[l])
        s5_mats = s5_matrices(s5_a_re[l], s5_a_im[l], s5_log_dt[l], s5_b_re[l], s5_b_im[l], s5_c_re[l], s5_c_im[l])
        conv2d_w = ml_conv[l].reshape(9, ML_W)

        def mix(h_in, mod, L, fft_len, filt, width, rows, s5_h0, ml_state):
            p_hy, p_s5, p_mx, p_mv, p_mo, p_mg = in_projection(h_in, g_pre_mix[l], mod[1], mod[0], w_in_b[l],
                                                               PROJ_SPLITS, PROJ_DTYPES)
            y_hy = hyena_mix(p_hy, hy_conv[l], filt, hy_bias[l], fft_len)
            y_s5, s5_fin = s5_mix(p_s5, s5_mats, s5_d[l], wglu_b[l], s5_b_glu[l], s5_h0)
            y_ml, ml_fin = mlstm_mix(p_mx, p_mv, p_mo, p_mg, conv2d_w, width, rows, wq_b[l], wk_b[l],
                                     ml_gate_bias[l], ml_skip[l], ml_norm_gain[l], ml_state)
            return (y_hy, y_s5, y_ml), s5_fin, ml_fin

        y_c, s5_state, ml_state = mix(ctx, mc, ctx_len, CTX_FFT_LEN, filt_c, ctx_len, (0,), s5_zero, ml_zero)
        y_x, _, _ = mix(x, mx, seq_len, seq_len, filt_x, GRID_W, (-1, 0, 1), s5_state, ml_state)
        x = out_projection(*y_x, w_out_b[l], g_post_mix[l], mx[2], x)
        x = mlp_block(x, g_pre_mlp[l], mx[4], mx[3], w1_b[l], w2_b[l], g_post_mlp[l], mx[5])
        if l < depth - 1:
            ctx = out_projection(*y_c, w_out_b[l], g_post_mix[l], mc[2], ctx)
            ctx = mlp_block(ctx, g_pre_mlp[l], mc[4], mc[3], w1_b[l], w2_b[l], g_post_mlp[l], mc[5])
    return x
```

```python
import functools
import math

import numpy as np
import jax
import jax.numpy as jnp
from jax import lax
from jax.experimental import pallas as pl
from jax.experimental.pallas import tpu as pltpu

F32 = jnp.float32
BF16 = jnp.bfloat16
EPS = 1e-6
HIGHEST = lax.Precision.HIGHEST

V7X_VMEM_BYTES = 64 * 1024 * 1024
VMEM_LIMIT = V7X_VMEM_BYTES - 8 * 1024 * 1024
LANES = 128
SUBLANES = 8
FFT_N2 = 64
FFT_UNROLL_A = 4
FFT_UNROLL_B = 8

HY_W = 256
HY_BANDS = 8
HY_EMB = 1 + 2 * HY_BANDS
HY_FF = 64
S5_W = 256
S5_CH = 16
S5_GROUPS = S5_W // S5_CH
S5_N = 64
S5_T = 8
ML_HEADS = 4
ML_HD = 128
ML_W = ML_HEADS * ML_HD
ML_CHUNK = 64
ML_GATES = 4 * ML_HEADS
GRID_W = 64
CTX_FFT_LEN = 512


def _cparams(sem):
    return pltpu.CompilerParams(dimension_semantics=sem, vmem_limit_bytes=VMEM_LIMIT)


def _dot(a, b):
    return jnp.dot(a, b, preferred_element_type=F32)


def _dot_nt(a, b):
    return lax.dot_general(a, b, (((1,), (1,)), ((), ())), preferred_element_type=F32)


def _rms(x, g):
    return x * lax.rsqrt(jnp.mean(x * x, axis=-1, keepdims=True) + EPS) * g


def _silu(x):
    return x * jax.nn.sigmoid(x)


def _mod_kernel(c_ref, w_ref, b_ref, o_ref):
    s = _silu(c_ref[...]).astype(BF16)
    o_ref[0] = _dot(s, w_ref[0].astype(BF16)) + b_ref[0]


def mod_vectors(cc, w_mod, b_mod):
    depth, d, n = w_mod.shape
    r = cc.shape[0]
    tn = 1536
    return pl.pallas_call(
        _mod_kernel,
        grid=(depth, n // tn),
        in_specs=[pl.BlockSpec((r, d), lambda l, j: (0, 0)),
                  pl.BlockSpec((1, d, tn), lambda l, j: (l, 0, j)),
                  pl.BlockSpec((1, 1, tn), lambda l, j: (l, 0, j))],
        out_specs=pl.BlockSpec((1, r, tn), lambda l, j: (l, 0, j)),
        out_shape=jax.ShapeDtypeStruct((depth, r, n), F32),
        compiler_params=_cparams(("arbitrary", "arbitrary")),
        name="mod_vectors",
    )(cc, w_mod, b_mod.reshape(depth, 1, n))


def _row_tile(L):
    return min(L, 512)


def _inproj_kernel(x_ref, g_ref, sc_ref, sh_ref, w_ref, *o_refs, splits):
    h = _rms(x_ref[0], g_ref[...]) * (1.0 + sc_ref[0]) + sh_ref[0]
    hb = h.astype(BF16)
    for o_ref, (a, b) in zip(o_refs, splits):
        o_ref[0] = _dot(hb, w_ref[:, a:b]).astype(o_ref.dtype)


def in_projection(x, g, sc, sh, w, splits, dtypes):
    bn, L, d = x.shape
    tm = _row_tile(L)
    vec = pl.BlockSpec((1, 1, d), lambda b, i: (b, 0, 0))
    return pl.pallas_call(
        functools.partial(_inproj_kernel, splits=splits),
        grid=(bn, L // tm),
        in_specs=[pl.BlockSpec((1, tm, d), lambda b, i: (b, i, 0)),
                  pl.BlockSpec((1, d), lambda b, i: (0, 0)), vec, vec,
                  pl.BlockSpec(w.shape, lambda b, i: (0, 0))],
        out_specs=[pl.BlockSpec((1, tm, b_ - a_), lambda b, i: (b, i, 0)) for a_, b_ in splits],
        out_shape=[jax.ShapeDtypeStruct((bn, L, b_ - a_), dt) for (a_, b_), dt in zip(splits, dtypes)],
        compiler_params=_cparams(("arbitrary", "arbitrary")),
        name="in_projection",
    )(x, g.reshape(1, d), sc, sh, w)


def _outproj_kernel(yh_ref, ys0_ref, ys1_ref, ym_ref, w_ref, g_ref, gt_ref, x_ref, o_ref):
    a = yh_ref.shape[-1]
    acc = _dot(yh_ref[0].astype(BF16), w_ref[0:a])
    acc += _dot(ys0_ref[0, 0].astype(BF16), w_ref[a:a + LANES])
    acc += _dot(ys1_ref[0, 0].astype(BF16), w_ref[a + LANES:a + 2 * LANES])
    acc += _dot(ym_ref[0].astype(BF16), w_ref[a + 2 * LANES:])
    o_ref[0] = x_ref[0] + gt_ref[0] * _rms(acc, g_ref[...])


def out_projection(y_hy, y_s5, y_ml, w, g, gate, x):
    bn, L, d = x.shape
    tm = _row_tile(L)
    tok = lambda wd: pl.BlockSpec((1, tm, wd), lambda b, i: (b, i, 0))
    half = lambda h: pl.BlockSpec((1, 1, tm, LANES), lambda b, i: (h, b, i, 0))
    return pl.pallas_call(
        _outproj_kernel,
        grid=(bn, L // tm),
        in_specs=[tok(y_hy.shape[-1]), half(0), half(1), tok(y_ml.shape[-1]),
                  pl.BlockSpec(w.shape, lambda b, i: (0, 0)),
                  pl.BlockSpec((1, d), lambda b, i: (0, 0)),
                  pl.BlockSpec((1, 1, d), lambda b, i: (b, 0, 0)), tok(d)],
        out_specs=tok(d),
        out_shape=jax.ShapeDtypeStruct((bn, L, d), F32),
        compiler_params=_cparams(("arbitrary", "arbitrary")),
        name="out_projection",
    )(y_hy, y_s5, y_s5, y_ml, w, g.reshape(1, d), gate, x)


def _mlp_kernel(x_ref, g1_ref, sc_ref, sh_ref, w1_ref, w2_ref, g2_ref, gt_ref, o_ref, h_ref, acc_ref):
    j = pl.program_id(2)

    @pl.when(j == 0)
    def _():
        h = _rms(x_ref[0], g1_ref[...]) * (1.0 + sc_ref[0]) + sh_ref[0]
        h_ref[...] = h.astype(BF16)
        acc_ref[...] = jnp.zeros_like(acc_ref)

    a = jnp.maximum(_dot(h_ref[...], w1_ref[...]), 0.0)
    acc_ref[...] += _dot((a * a).astype(BF16), w2_ref[...])

    @pl.when(j == pl.num_programs(2) - 1)
    def _():
        o_ref[0] = x_ref[0] + gt_ref[0] * _rms(acc_ref[...], g2_ref[...])


def mlp_block(x, g_pre, sc, sh, w1, w2, g_post, gate):
    bn, L, d = x.shape
    hid = w1.shape[1]
    tm = min(L, 1024)
    th = 2048
    tok = pl.BlockSpec((1, tm, d), lambda b, i, j: (b, i, 0))
    vec = pl.BlockSpec((1, 1, d), lambda b, i, j: (b, 0, 0))
    gain = pl.BlockSpec((1, d), lambda b, i, j: (0, 0))
    return pl.pallas_call(
        _mlp_kernel,
        grid=(bn, L // tm, hid // th),
        in_specs=[tok, gain, vec, vec,
                  pl.BlockSpec((d, th), lambda b, i, j: (0, j)),
                  pl.BlockSpec((th, d), lambda b, i, j: (j, 0)),
                  gain, vec],
        out_specs=tok,
        out_shape=jax.ShapeDtypeStruct((bn, L, d), F32),
        scratch_shapes=[pltpu.VMEM((tm, d), BF16), pltpu.VMEM((tm, d), F32)],
        compiler_params=_cparams(("arbitrary", "arbitrary", "arbitrary")),
        name="mlp_block",
    )(x, g_pre.reshape(1, d), sc, sh, w1, w2, g_post.reshape(1, d), gate)


def _dwconv_kernel(x_ref, w_ref, o_ref, pad_ref, *, width, taps, act, pad):
    L = x_ref.shape[1]
    ch = x_ref.shape[2]
    pad_ref[0:pad, :] = jnp.zeros((pad, ch), F32)
    pad_ref[pad + L:pad + L + pad, :] = jnp.zeros((pad, ch), F32)
    pad_ref[pad:pad + L, :] = x_ref[0].astype(F32)
    tr = min(L, 256)
    for r0 in range(0, L, tr):
        col = (lax.broadcasted_iota(jnp.int32, (tr, ch), 0) + r0) & (width - 1)
        acc = jnp.zeros((tr, ch), F32)
        for dr, dc in taps:
            start = pad + r0 + dr * width + dc
            v = pad_ref[start:start + tr, :]
            if dc == -1:
                v = jnp.where(col >= 1, v, 0.0)
            elif dc == 1:
                v = jnp.where(col <= width - 2, v, 0.0)
            acc = acc + w_ref[(dr + 1) * 3 + (dc + 1):(dr + 1) * 3 + (dc + 2), :] * v
        o_ref[0, r0:r0 + tr, :] = act(acc)


def depthwise_conv(x, w9, width, rows, act=None):
    bn, L, ch = x.shape
    assert width & (width - 1) == 0
    taps = tuple((dr, dc) for dr in rows for dc in (-1, 0, 1))
    pad = -(-(width + 1) // SUBLANES) * SUBLANES if len(rows) > 1 else SUBLANES
    act = act or (lambda a: a)
    cs = LANES
    return pl.pallas_call(
        functools.partial(_dwconv_kernel, width=width, taps=taps, act=act, pad=pad),
        grid=(bn, ch // cs),
        in_specs=[pl.BlockSpec((1, L, cs), lambda b, c: (b, 0, c)),
                  pl.BlockSpec((9, cs), lambda b, c: (0, c))],
        out_specs=pl.BlockSpec((1, L, cs), lambda b, c: (b, 0, c)),
        out_shape=jax.ShapeDtypeStruct((bn, L, ch), F32),
        scratch_shapes=[pltpu.VMEM((L + 2 * pad, cs), F32)],
        compiler_params=_cparams(("arbitrary", "arbitrary")),
        name="depthwise_conv",
    )(x, w9)


def _hyfilt_kernel(w1_ref, b1_ref, w2_ref, b2_ref, w3_ref, fr_ref, dec_ref, o_ref, *, L):
    tr = min(L, 512)
    wd = o_ref.shape[1]
    fr = fr_ref[...]
    ssq = jnp.zeros((1, wd), F32)
    for r0 in range(0, L, tr):
        t = (lax.broadcasted_iota(jnp.int32, (tr, 32), 0) + r0).astype(F32) / L
        lane = lax.broadcasted_iota(jnp.int32, (tr, 32), 1)
        band = jnp.where(lane <= HY_BANDS, lane, lane - HY_BANDS).astype(F32)
        ang = 2.0 * math.pi * t * band
        feat = jnp.where(lane == 0, t, jnp.where(lane <= HY_BANDS, jnp.cos(ang),
                                                 jnp.where(lane <= 2 * HY_BANDS, jnp.sin(ang), 0.0)))
        hdn = jnp.sin(fr * (jnp.dot(feat, w1_ref[...], precision=HIGHEST, preferred_element_type=F32) + b1_ref[...]))
        hdn = jnp.sin(fr * (jnp.dot(hdn, w2_ref[...], precision=HIGHEST, preferred_element_type=F32) + b2_ref[...]))
        filt = _dot(hdn.astype(BF16), w3_ref[...].astype(BF16))
        filt = filt * jnp.exp(-t[:, 0:1] * dec_ref[...])
        ssq = ssq + jnp.sum(filt * filt, axis=0, keepdims=True)
        o_ref[r0:r0 + tr, :] = filt
    scales = []
    for o in range(wd // (2 * HY_W)):
        tot = ssq[:, 2 * o * HY_W:(2 * o + 1) * HY_W] + ssq[:, (2 * o + 1) * HY_W:(2 * o + 2) * HY_W]
        scales += [lax.rsqrt(tot + EPS)] * 2
    scale = jnp.concatenate(scales, axis=1)
    for r0 in range(0, L, tr):
        o_ref[r0:r0 + tr, :] = o_ref[r0:r0 + tr, :] * scale


def hyena_filters(L, w1, b1, w2, b2, w3, freq, decay):
    wd = decay.shape[0] * 2 * HY_W
    w1p = jnp.zeros((32, HY_FF), F32).at[:HY_EMB].set(w1)
    full = lambda a: pl.BlockSpec(a.shape, lambda o: (0,) * a.ndim)
    args = (w1p, b1.reshape(1, HY_FF), w2, b2.reshape(1, HY_FF), w3, freq.reshape(1, HY_FF), decay.reshape(1, wd))
    return pl.pallas_call(
        functools.partial(_hyfilt_kernel, L=L),
        grid=(1,),
        in_specs=[full(a) for a in args],
        out_specs=pl.BlockSpec((L, wd), lambda o: (0, 0)),
        out_shape=jax.ShapeDtypeStruct((L, wd), F32),
        compiler_params=_cparams(("arbitrary",)),
        name="hyena_filters",
    )(*args)


@functools.lru_cache(maxsize=None)
def _dft_consts(L):
    n = 2 * L
    n2c = FFT_N2
    n1c = n // n2c
    n1h = n1c // 2
    n1 = np.arange(n1h)[None, None, :]
    k1 = np.arange(n1c)[None, :, None]
    n2 = np.arange(n2c)[:, None, None]
    ang = -2.0 * np.pi * (k1 * n1 / n1c + n2 * k1 / n)
    fr, fi = np.cos(ang), np.sin(ang)
    fa = np.concatenate([np.concatenate([fr, -fi], axis=2),
                         np.concatenate([fi, fr], axis=2)], axis=1)
    frt = np.transpose(fr, (0, 2, 1)) / n
    fit = -np.transpose(fi, (0, 2, 1)) / n
    fai = np.concatenate([np.concatenate([frt, -fit], axis=2),
                          np.concatenate([fit, frt], axis=2)], axis=1)
    k2 = np.arange(n2c)[:, None]
    m2 = np.arange(n2c)[None, :]
    angb = -2.0 * np.pi * k2 * m2 / n2c
    gr, gi = np.cos(angb), np.sin(angb)
    gb = np.block([[gr, -gi], [gi, gr]])
    gbi = np.block([[gr, gi], [-gi, gr]])
    return (fa.astype(np.float32), gb.astype(np.float32), gbi.astype(np.float32), fai.astype(np.float32))


def _slab_rows(n1c):
    return 2 * n1c + SUBLANES


def _slab(n2, n1c):
    return pl.ds(pl.multiple_of(n2 * _slab_rows(n1c), SUBLANES), 2 * n1c)


def _fft_stage_a(load_rhs, fa_ref, s_ref, n1c):
    def body(n2, carry):
        rhs = load_rhs(n2).astype(BF16)
        s_ref[_slab(n2, n1c), :] = _dot(fa_ref[n2], rhs)
        return carry
    lax.fori_loop(0, FFT_N2, body, 0, unroll=FFT_UNROLL_A)


def _fft_load_k1(s_ref, k1, n1c):
    xr = s_ref[pl.ds(k1, FFT_N2, stride=_slab_rows(n1c)), :]
    xi = s_ref[pl.ds(n1c + k1, FFT_N2, stride=_slab_rows(n1c)), :]
    return jnp.concatenate([xr, xi], axis=0)


def _fftconv_kernel(z_ref, gate_ref, bias_ref, h_ref, fa_ref, gb_ref, gbi_ref, fai_ref, o_ref, s_ref, *, L):
    n1c = L // 32
    n1h = n1c // 2
    half = FFT_N2

    def load_rhs(n2):
        za = z_ref[0, 0, pl.ds(n2, n1h, stride=FFT_N2), :]
        zb = z_ref[0, 1, pl.ds(n2, n1h, stride=FFT_N2), :]
        return jnp.concatenate([za, zb], axis=0)
    _fft_stage_a(load_rhs, fa_ref, s_ref, n1c)

    def stage_b(k1, carry):
        x = _dot(gb_ref[...], _fft_load_k1(s_ref, k1, n1c).astype(BF16))
        xr, xi = x[:half], x[half:]
        h = h_ref[k1]
        hr, hi = h[:half], h[half:]
        y = jnp.concatenate([xr * hr - xi * hi, xr * hi + xi * hr], axis=0)
        bp = _dot(gbi_ref[...], y.astype(BF16))
        s_ref[pl.ds(k1, FFT_N2, stride=_slab_rows(n1c)), :] = bp[:half]
        s_ref[pl.ds(n1c + k1, FFT_N2, stride=_slab_rows(n1c)), :] = bp[half:]
        return carry
    lax.fori_loop(0, n1c, stage_b, 0, unroll=FFT_UNROLL_B)

    bias = bias_ref[...]

    def stage_a_inv(n2, carry):
        rhs = s_ref[_slab(n2, n1c), :].astype(BF16)
        res = _dot(fai_ref[n2], rhs)
        for p in range(2):
            rows = pl.ds(n2, n1h, stride=FFT_N2)
            zin = z_ref[0, p, rows, :]
            o_ref[0, p, rows, :] = gate_ref[0, p, rows, :] * (res[p * n1h:(p + 1) * n1h] + bias * zin)
        return carry
    lax.fori_loop(0, FFT_N2, stage_a_inv, 0, unroll=FFT_UNROLL_A)


def fft_gated_conv(z, zoff, gate, goff, bias, hspec):
    bsz, L, _ = z.shape
    ch = bias.shape[0]
    n1c = L // 32
    fa, gb, gbi, fai = (jnp.asarray(a).astype(BF16) for a in _dft_consts(L))
    zp = z.reshape(bsz // 2, 2, L, z.shape[-1])
    gp = gate.reshape(bsz // 2, 2, L, gate.shape[-1])
    cs = LANES
    one = pl.Buffered(1)
    const3 = lambda a: pl.BlockSpec(a.shape, lambda c, p: (0, 0, 0), pipeline_mode=one)
    const2 = lambda a: pl.BlockSpec(a.shape, lambda c, p: (0, 0), pipeline_mode=one)
    out = pl.pallas_call(
        functools.partial(_fftconv_kernel, L=L),
        grid=(ch // cs, bsz // 2),
        in_specs=[pl.BlockSpec((1, 2, L, cs), lambda c, p: (p, 0, 0, c + zoff)),
                  pl.BlockSpec((1, 2, L, cs), lambda c, p: (p, 0, 0, c + goff)),
                  pl.BlockSpec((1, cs), lambda c, p: (0, c)),
                  pl.BlockSpec((n1c, 2 * FFT_N2, cs), lambda c, p: (0, 0, c), pipeline_mode=one),
                  const3(fa), const2(gb), const2(gbi), const3(fai)],
        out_specs=pl.BlockSpec((1, 2, L, cs), lambda c, p: (p, 0, 0, c)),
        out_shape=jax.ShapeDtypeStruct((bsz // 2, 2, L, ch), F32),
        scratch_shapes=[pltpu.VMEM((FFT_N2 * _slab_rows(n1c), cs), F32)],
        compiler_params=_cparams(("arbitrary", "arbitrary")),
        name="fft_gated_conv",
    )(zp, gp, bias.reshape(1, ch), hspec, fa, gb, gbi, fai)
    return out.reshape(bsz, L, ch)


def _fftspec_kernel(hf_ref, hb_ref, fa_ref, gb_ref, o_ref, s_ref, *, L):
    n1c = L // 32
    n1h = n1c // 2
    half = FFT_N2
    for d, src in enumerate((hf_ref, hb_ref)):
        def load_rhs(n2, src=src, d=d):
            h = src[pl.ds(n2, n1h, stride=FFT_N2), :]
            if d == 1:
                row = lax.broadcasted_iota(jnp.int32, h.shape, 0)
                h = jnp.where((row == 0) & (n2 == 0), 0.0, h)
            return jnp.concatenate([h, jnp.zeros_like(h)], axis=0)
        _fft_stage_a(load_rhs, fa_ref, s_ref, n1c)

        def stage_b(k1, carry, d=d):
            x = _dot(gb_ref[...], _fft_load_k1(s_ref, k1, n1c).astype(BF16))
            if d == 0:
                o_ref[k1] = x
            else:
                o_ref[k1] = o_ref[k1] + jnp.concatenate([x[:half], -x[half:]], axis=0)
            return carry
        lax.fori_loop(0, n1c, stage_b, 0, unroll=FFT_UNROLL_B)


def fft_filter_spectrum(filt, foff, boff, L):
    if filt.shape[0] < L:
        filt = jnp.pad(filt, ((0, L - filt.shape[0]), (0, 0)))
    n1c = L // 32
    fa, gb, _, _ = (jnp.asarray(a).astype(BF16) for a in _dft_consts(L))
    cs = LANES
    return pl.pallas_call(
        functools.partial(_fftspec_kernel, L=L),
        grid=(HY_W // cs,),
        in_specs=[pl.BlockSpec((L, cs), lambda c: (0, c + foff)), pl.BlockSpec((L, cs), lambda c: (0, c + boff)),
                  pl.BlockSpec(fa.shape, lambda c: (0, 0, 0)), pl.BlockSpec(gb.shape, lambda c: (0, 0))],
        out_specs=pl.BlockSpec((n1c, 2 * FFT_N2, cs), lambda c: (0, 0, c)),
        out_shape=jax.ShapeDtypeStruct((n1c, 2 * FFT_N2, HY_W), F32),
        scratch_shapes=[pltpu.VMEM((FFT_N2 * _slab_rows(n1c), cs), F32)],
        compiler_params=_cparams(("arbitrary",)),
        name="fft_filter_spectrum",
    )(filt, filt, fa, gb)


def hyena_mix(p_hy, conv_w, filt, bias, fft_len):
    bsz, L, _ = p_hy.shape
    w9 = jnp.zeros((9, 3 * HY_W), F32).at[3:6].set(conv_w)
    u = depthwise_conv(p_hy, w9, width=L, rows=(0,))
    if fft_len > L:
        u = jnp.pad(u, ((0, 0), (0, fft_len - L), (0, 0)))
    nb = HY_W // LANES
    z = fft_gated_conv(u, 0, u, nb, bias[0], fft_filter_spectrum(filt, 0, nb, fft_len))
    z = fft_gated_conv(z, 0, u, 2 * nb, bias[1], fft_filter_spectrum(filt, 2 * nb, 3 * nb, fft_len))
    return z[:, :L]


def s5_matrices(a_re, a_im, log_dt, b_re, b_im, c_re, c_im):
    t = S5_T
    sw = S5_GROUPS * S5_N
    dt = jnp.exp(log_dt)[..., None]
    mag = jnp.exp(a_re * dt)
    ar, ai = mag * jnp.cos(a_im * dt), mag * jnp.sin(a_im * dt)
    den = a_re * a_re + a_im * a_im
    qr = ((ar - 1.0) * a_re + ai * a_im) / den
    qi = (ai * a_re - (ar - 1.0) * a_im) / den
    bbr = qr[..., None] * b_re - qi[..., None] * b_im
    bbi = qr[..., None] * b_im + qi[..., None] * b_re
    pr, pi = [jnp.ones_like(ar)], [jnp.zeros_like(ai)]
    for _ in range(t):
        pr, pi = pr + [pr[-1] * ar - pi[-1] * ai], pi + [pr[-1] * ai + pi[-1] * ar]
    pr = [p.reshape(2, sw) for p in pr]
    pi = [p.reshape(2, sw) for p in pi]
    gd_hn = (np.arange(S5_W)[:, None] // S5_CH == np.arange(sw)[None, :] // S5_N).astype(np.float32)
    gn_hc = gd_hn.T
    expand_b = lambda b: jnp.tile(jnp.swapaxes(b, -1, -2).reshape(2, S5_W, S5_N), (1, 1, S5_GROUPS)) * gd_hn
    expand_c = lambda c: jnp.tile(jnp.swapaxes(c, -1, -2).reshape(2, sw, S5_CH), (1, 1, S5_GROUPS)) * gn_hc
    b_r, b_i = expand_b(bbr), expand_b(bbi)
    c_r, c_i = expand_c(c_re), expand_c(c_im)
    col = lambda p, x: p[x][None, :]
    rowv = lambda p, x: p[x][:, None]
    r_in = jnp.concatenate([
        jnp.concatenate([b_r[0] * col(pr[t - 1 - s], 0) - b_i[0] * col(pi[t - 1 - s], 0),
                         b_r[0] * col(pi[t - 1 - s], 0) + b_i[0] * col(pr[t - 1 - s], 0),
                         b_r[1] * col(pr[s], 1) - b_i[1] * col(pi[s], 1),
                         b_r[1] * col(pi[s], 1) + b_i[1] * col(pr[s], 1)], axis=1)
        for s in range(t)], axis=0)

    def scaled_c(x, j):
        return (c_r[x] * rowv(pr[j], x) - c_i[x] * rowv(pi[j], x), c_r[x] * rowv(pi[j], x) + c_i[x] * rowv(pr[j], x))
    blocks = []
    for tt in range(t):
        fr, fi = scaled_c(0, tt + 1)
        br, bi = scaled_c(1, t - tt)
        blocks.append(jnp.concatenate([fr, -fi, br, -bi], axis=0))
    r_out = jnp.concatenate(blocks, axis=1)
    lags = (list(range(t)), list(range(t - 1, -1, -1)))
    taps = []
    for x in range(2):
        sc = [scaled_c(x, j) for j in lags[x]]
        taps.append(s5_taps(b_r[x], b_i[x], jnp.concatenate([p[0] for p in sc], axis=1),
                            jnp.concatenate([p[1] for p in sc], axis=1)))
    w = S5_W
    m_intra = jnp.concatenate([
        jnp.pad(taps[0][:, :(t - s) * w], ((0, 0), (s * w, 0)))
        + jnp.pad(taps[1][:, (t - 1 - s) * w:], ((0, 0), (0, (t - 1 - s) * w)))
        for s in range(t)], axis=0)
    a8v = jnp.concatenate([pr[t][0], pi[t][0], pr[t][1], pi[t][1]])[None, :]
    return m_intra.astype(BF16), r_in.astype(BF16), r_out.astype(BF16), a8v


def _s5taps_kernel(br_ref, bi_ref, cr_ref, ci_ref, o_ref):
    o_ref[...] = (_dot(br_ref[...].astype(BF16), cr_ref[...].astype(BF16))
                  - _dot(bi_ref[...].astype(BF16), ci_ref[...].astype(BF16)))


def s5_taps(b_r, b_i, c_r, c_i):
    m, n = b_r.shape[0], c_r.shape[1]
    tn = 512
    lhs = pl.BlockSpec(b_r.shape, lambda j: (0, 0))
    rhs = pl.BlockSpec((c_r.shape[0], tn), lambda j: (0, j))
    return pl.pallas_call(
        _s5taps_kernel,
        grid=(n // tn,),
        in_specs=[lhs, lhs, rhs, rhs],
        out_specs=pl.BlockSpec((m, tn), lambda j: (0, j)),
        out_shape=jax.ShapeDtypeStruct((m, n), F32),
        compiler_params=_cparams(("arbitrary",)),
        name="s5_taps",
    )(b_r, b_i, c_r, c_i)


def _row_tokens(x0_ref, x1_ref, t, rows):
    sl = pl.ds(t, rows, stride=S5_T)
    return jnp.concatenate([x0_ref[0, sl, :], x1_ref[0, sl, :]], axis=1)


def _s5in_kernel(x0_ref, x1_ref, w_ref, o_ref):
    rows = o_ref.shape[1]
    wd = 2 * LANES
    acc = _dot(_row_tokens(x0_ref, x1_ref, 0, rows).astype(BF16), w_ref[0:wd, :])
    for t in range(1, S5_T):
        acc += _dot(_row_tokens(x0_ref, x1_ref, t, rows).astype(BF16), w_ref[t * wd:(t + 1) * wd, :])
    o_ref[0] = acc


def s5_state_inputs(x, w, tn):
    bn, L, wd = x.shape
    assert wd == 2 * LANES
    r = L // S5_T
    n = w.shape[1]
    tm = min(r, 256)
    half = lambda h: pl.BlockSpec((1, tm * S5_T, LANES), lambda j, b, i: (b, i, h))
    return pl.pallas_call(
        _s5in_kernel,
        grid=(n // tn, bn, r // tm),
        in_specs=[half(0), half(1), pl.BlockSpec((S5_T * wd, tn), lambda j, b, i: (0, j))],
        out_specs=pl.BlockSpec((1, tm, tn), lambda j, b, i: (b, i, j)),
        out_shape=jax.ShapeDtypeStruct((bn, r, n), F32),
        compiler_params=_cparams(("arbitrary", "arbitrary", "arbitrary")),
        name="s5_state_inputs",
    )(x, x, w)


def _s5out_kernel(x0_ref, x1_ref, m_ref, s_ref, r_ref, d_ref, wg_ref, bg_ref, o_ref):
    rows = s_ref.shape[1]
    wd = 2 * LANES
    xt = [_row_tokens(x0_ref, x1_ref, t, rows) for t in range(S5_T)]
    acc = _dot(s_ref[0].astype(BF16), r_ref[...])
    for t in range(S5_T):
        acc += _dot(xt[t].astype(BF16), m_ref[t * wd:(t + 1) * wd, :])
    for t in range(S5_T):
        g = jax.nn.gelu(acc[:, t * wd:(t + 1) * wd] + d_ref[...] * xt[t], approximate=True)
        y = g * jax.nn.sigmoid(_dot(g.astype(BF16), wg_ref[...]) + bg_ref[...])
        o_ref[0, 0, pl.ds(t, rows, stride=S5_T), :] = y[:, :LANES]
        o_ref[1, 0, pl.ds(t, rows, stride=S5_T), :] = y[:, LANES:]


def s5_readout_glu(x, m_intra, xs, r_out, d_skip, w_glu, b_glu):
    bn, L, wd = x.shape
    assert wd == 2 * LANES
    r = L // S5_T
    tm = min(r, 256)
    one = pl.Buffered(1)
    half = lambda h: pl.BlockSpec((1, tm * S5_T, LANES), lambda b, i: (b, i, h))
    row = pl.BlockSpec((1, wd), lambda b, i: (0, 0))
    return pl.pallas_call(
        _s5out_kernel,
        grid=(bn, r // tm),
        in_specs=[half(0), half(1), pl.BlockSpec(m_intra.shape, lambda b, i: (0, 0), pipeline_mode=one),
                  pl.BlockSpec((1, tm, xs.shape[-1]), lambda b, i: (b, i, 0)),
                  pl.BlockSpec(r_out.shape, lambda b, i: (0, 0), pipeline_mode=one),
                  row, pl.BlockSpec((wd, wd), lambda b, i: (0, 0)), row],
        out_specs=pl.BlockSpec((2, 1, tm * S5_T, LANES), lambda b, i: (0, b, i, 0)),
        out_shape=jax.ShapeDtypeStruct((2, bn, L, LANES), F32),
        compiler_params=_cparams(("arbitrary", "arbitrary")),
        name="s5_readout_glu",
    )(x, x, m_intra, xs, r_out, d_skip.reshape(1, wd), w_glu, b_glu.reshape(1, wd))


def _s5scan_kernel(r_ref, a_ref, h0_ref, x_ref, fin_ref):
    nk = r_ref.shape[1]
    sw = S5_GROUPS * S5_N
    ar_f, ai_f = a_ref[:, 0:sw], a_ref[:, sw:2 * sw]
    ar_b, ai_b = a_ref[:, 2 * sw:3 * sw], a_ref[:, 3 * sw:4 * sw]
    init = tuple(h0_ref[0, :, j * sw:(j + 1) * sw] for j in range(4))

    def body(i, carry):
        fr, fi, br, bi = carry
        k = nk - 1 - i
        x_ref[0, pl.ds(i, 1), 0:sw] = fr
        x_ref[0, pl.ds(i, 1), sw:2 * sw] = fi
        x_ref[0, pl.ds(k, 1), 2 * sw:3 * sw] = br
        x_ref[0, pl.ds(k, 1), 3 * sw:4 * sw] = bi
        rfr = r_ref[0, pl.ds(i, 1), 0:sw]
        rfi = r_ref[0, pl.ds(i, 1), sw:2 * sw]
        rbr = r_ref[0, pl.ds(k, 1), 2 * sw:3 * sw]
        rbi = r_ref[0, pl.ds(k, 1), 3 * sw:4 * sw]
        return (ar_f * fr - ai_f * fi + rfr, ar_f * fi + ai_f * fr + rfi,
                ar_b * br - ai_b * bi + rbr, ar_b * bi + ai_b * br + rbi)
    fin = lax.fori_loop(0, nk, body, init)
    for j in range(4):
        fin_ref[0, :, j * sw:(j + 1) * sw] = fin[j]


def s5_row_scan(r, a8v, h0):
    bn, nk, w = r.shape
    return pl.pallas_call(
        _s5scan_kernel,
        grid=(bn,),
        in_specs=[pl.BlockSpec((1, nk, w), lambda b: (b, 0, 0)),
                  pl.BlockSpec((1, w), lambda b: (0, 0)),
                  pl.BlockSpec((1, 1, w), lambda b: (b, 0, 0))],
        out_specs=[pl.BlockSpec((1, nk, w), lambda b: (b, 0, 0)),
                   pl.BlockSpec((1, 1, w), lambda b: (b, 0, 0))],
        out_shape=[jax.ShapeDtypeStruct((bn, nk, w), F32), jax.ShapeDtypeStruct((bn, 1, w), F32)],
        compiler_params=_cparams(("arbitrary",)),
        name="s5_row_scan",
    )(r, a8v, h0)


def s5_mix(p_s5, mats, d_skip, w_glu, b_glu, h0):
    m_intra, r_in, r_out, a8v = mats
    r = s5_state_inputs(p_s5, r_in, tn=1024)
    xs, fin = s5_row_scan(r, a8v, h0)
    return s5_readout_glu(p_s5, m_intra, xs, r_out, d_skip, w_glu, b_glu), fin


def _qk_kernel(x_ref, wq_ref, wk_ref, q_ref, k_ref):
    for h in range(ML_HEADS):
        sl = slice(h * ML_HD, (h + 1) * ML_HD)
        xb = x_ref[0, :, sl].astype(BF16)
        q_ref[0, :, sl] = _dot(xb, wq_ref[h]).astype(BF16)
        k_ref[0, :, sl] = (_dot(xb, wk_ref[h]) * (ML_HD ** -0.5)).astype(BF16)


def mlstm_qk(xc, wq, wk):
    bn, L, w = xc.shape
    tm = _row_tile(L)
    tok = pl.BlockSpec((1, tm, w), lambda b, i: (b, i, 0))
    wsp = pl.BlockSpec(wq.shape, lambda b, i: (0, 0, 0))
    return pl.pallas_call(
        _qk_kernel,
        grid=(bn, L // tm),
        in_specs=[tok, wsp, wsp],
        out_specs=[tok, tok],
        out_shape=[jax.ShapeDtypeStruct((bn, L, w), BF16)] * 2,
        compiler_params=_cparams(("arbitrary", "arbitrary")),
        name="mlstm_qk",
    )(xc, wq, wk)


def _mlstm_kernel(bias_ref, q_ref, k_ref, v_ref, g_ref, c0_ref, n0_ref, m0_ref,
                  h_ref, cf_ref, nf_ref, mf_ref, st_s, m_s, lf_s, b_s):
    hd = pl.program_id(0)
    bn = q_ref.shape[0]
    nc = g_ref.shape[2]
    t = ML_CHUNK
    chains = [(b, d) for b in range(bn) for d in range(2)]
    for i, (b, d) in enumerate(chains):
        st_s[i, :, 0:ML_HD] = c0_ref[b, d, 0].T
        st_s[i, :, ML_HD:] = jnp.broadcast_to(n0_ref[b, d, 0], (ML_HD, ML_HD)).T
        m_s[i] = m0_ref[b, d, 0]
    ones_blk = jnp.ones((t, ML_HD), BF16)
    row = lax.broadcasted_iota(jnp.int32, (t, t), 0)
    col = lax.broadcasted_iota(jnp.int32, (t, t), 1)
    tri = (col <= row, col >= row)
    cum = ((row <= col).astype(F32), (row >= col).astype(F32))
    bias_i = [bias_ref[d * 2 * ML_HEADS + hd] for d in range(2)]
    bias_f = [bias_ref[d * 2 * ML_HEADS + ML_HEADS + hd] for d in range(2)]
    for i, (b, d) in enumerate(chains):
        lf_all = jax.nn.log_sigmoid(g_ref[b, d * 2 * ML_HEADS + ML_HEADS + hd] + bias_f[d])
        lf_s[i] = lf_all
        b_s[i] = jnp.dot(lf_all, cum[d], precision=HIGHEST, preferred_element_type=F32)

    def body(j, carry):
        ids = range(len(chains))
        cidx = [j if d == 0 else nc - 1 - j for _, d in chains]
        r0 = [pl.multiple_of(c * t, t) for c in cidx]
        q = [q_ref[b, pl.ds(r0[i], t), :] for i, (b, d) in enumerate(chains)]
        k = [k_ref[b, pl.ds(r0[i], t), :] for i, (b, d) in enumerate(chains)]
        v = [v_ref[b, pl.ds(r0[i], t), :] for i, (b, d) in enumerate(chains)]
        li = [g_ref[b, d * 2 * ML_HEADS + hd, pl.ds(cidx[i], 1), :] + bias_i[d] for i, (b, d) in enumerate(chains)]
        lf = [lf_s[i, pl.ds(cidx[i], 1), :] for i in ids]
        b_row = [b_s[i, pl.ds(cidx[i], 1), :] for i in ids]
        b_col = [jnp.sum(jnp.where(tri[d], lf[i], 0.0), axis=-1, keepdims=True) for i, (b, d) in enumerate(chains)]
        g = [jnp.sum(lf[i], axis=-1, keepdims=True) for i in ids]
        a_row = [g[i] - b_row[i] + li[i] for i in ids]
        m_loc = [jnp.max(a_row[i], axis=-1, keepdims=True) for i in ids]
        w_row = [jnp.exp(a_row[i] - m_loc[i]) for i in ids]
        st_prev = [st_s[i] for i in ids]
        m_prev = [m_s[i][:, 0:1] for i in ids]
        kt = [k[i].astype(F32).T for i in ids]
        qr = [_dot(q[i], jnp.concatenate([st_prev[i].astype(BF16), kt[i].astype(BF16)], axis=1)) for i in ids]
        v1 = [jnp.concatenate([v[i], ones_blk], axis=1) for i in ids]
        upd = [_dot((kt[i] * w_row[i]).astype(BF16), v1[i]) for i in ids]
        dmat = [jnp.where(tri[d], b_col[i] - b_row[i] + li[i], -jnp.inf) for i, (b, d) in enumerate(chains)]
        inter = [b_col[i] + m_prev[i] for i in ids]
        m_t = [jnp.maximum(jnp.max(dmat[i], axis=-1, keepdims=True), inter[i]) for i in ids]
        s = [qr[i][:, 2 * ML_HD:] * jnp.exp(dmat[i] - m_t[i]) for i in ids]
        w_inter = [jnp.exp(inter[i] - m_t[i]) for i in ids]
        sv = [_dot(s[i].astype(BF16), v1[i]) for i in ids]
        nq = [sv[i][:, ML_HD:ML_HD + 1] + w_inter[i] * qr[i][:, ML_HD:ML_HD + 1] for i in ids]
        for i, (b, d) in enumerate(chains):
            num = sv[i][:, :ML_HD] + w_inter[i] * qr[i][:, :ML_HD]
            h_ref[d, b, pl.ds(r0[i], t), :] = num / jnp.maximum(jnp.abs(nq[i]), jnp.exp(-m_t[i]))
        for i in ids:
            m_new = jnp.maximum(g[i] + m_prev[i], m_loc[i])
            dec = jnp.exp(g[i] + m_prev[i] - m_new)
            grow = jnp.exp(m_loc[i] - m_new)
            st_s[i] = dec * st_prev[i] + grow * upd[i]
            m_s[i] = jnp.broadcast_to(m_new, (1, LANES))
        return carry
    lax.fori_loop(0, nc, body, 0)
    for i, (b, d) in enumerate(chains):
        cf_ref[b, d, 0] = st_s[i, :, 0:ML_HD].T
        nf_ref[b, d, 0] = st_s[i, :, ML_HD:].T[0:1]
        mf_ref[b, d, 0] = m_s[i]


def mlstm_scan(q, k, v, gates_t, gate_bias, state):
    bn, L, w = q.shape
    c0, n0, m0 = state
    hsp = pl.BlockSpec((bn, L, ML_HD), lambda h: (0, 0, h), pipeline_mode=pl.Buffered(1))
    st = lambda a: pl.BlockSpec((bn, 2, 1) + a.shape[3:], lambda h: (0, 0, h, 0, 0))
    return pl.pallas_call(
        _mlstm_kernel,
        grid=(ML_HEADS,),
        in_specs=[pl.BlockSpec(memory_space=pltpu.SMEM), hsp, hsp, hsp,
                  pl.BlockSpec(gates_t.shape, lambda h: (0, 0, 0, 0)),
                  st(c0), st(n0), st(m0)],
        out_specs=[pl.BlockSpec((2, bn, L, ML_HD), lambda h: (0, 0, 0, h)), st(c0), st(n0), st(m0)],
        out_shape=[jax.ShapeDtypeStruct((2, bn, L, w), F32),
                   jax.ShapeDtypeStruct(c0.shape, F32), jax.ShapeDtypeStruct(n0.shape, F32),
                   jax.ShapeDtypeStruct(m0.shape, F32)],
        scratch_shapes=[pltpu.VMEM((2 * bn, ML_HD, 2 * ML_HD), F32),
                        pltpu.VMEM((2 * bn, 1, LANES), F32),
                        pltpu.VMEM((2 * bn,) + gates_t.shape[2:], F32),
                        pltpu.VMEM((2 * bn,) + gates_t.shape[2:], F32)],
        compiler_params=_cparams(("arbitrary",)),
        name="mlstm_scan",
    )(gate_bias, q, k, v, gates_t, c0, n0, m0)


def _mlout_kernel(hf_ref, hb_ref, o_ref, xc_ref, gain_ref, skip_ref, y_ref):
    h = hf_ref[0, 0] + hb_ref[0, 0]
    parts = []
    for hd in range(ML_HEADS):
        sl = slice(hd * ML_HD, (hd + 1) * ML_HD)
        hh = h[:, sl]
        parts.append(hh * lax.rsqrt(jnp.mean(hh * hh, axis=-1, keepdims=True) + EPS) * gain_ref[:, sl])
    hn = jnp.concatenate(parts, axis=-1)
    y_ref[0] = jax.nn.sigmoid(o_ref[0].astype(F32)) * (hn + skip_ref[...] * xc_ref[0])


def mlstm_output(h2, p_o, xc, gain, skip):
    _, bn, L, w = h2.shape
    tm = _row_tile(L)
    tok = pl.BlockSpec((1, tm, w), lambda b, i: (b, i, 0))
    row = pl.BlockSpec((1, w), lambda b, i: (0, 0))
    return pl.pallas_call(
        _mlout_kernel,
        grid=(bn, L // tm),
        in_specs=[pl.BlockSpec((1, 1, tm, w), lambda b, i: (0, b, i, 0)),
                  pl.BlockSpec((1, 1, tm, w), lambda b, i: (1, b, i, 0)), tok, tok, row, row],
        out_specs=tok,
        out_shape=jax.ShapeDtypeStruct((bn, L, w), F32),
        compiler_params=_cparams(("arbitrary", "arbitrary")),
        name="mlstm_output",
    )(h2, h2, p_o, xc, gain.reshape(1, w), skip.reshape(1, w))


def mlstm_mix(p_x, p_v, p_o, p_g, conv_w9, width, rows, wq, wk, gate_bias, skip, gain, state):
    bn, L, _ = p_x.shape
    xc = depthwise_conv(p_x, conv_w9, width=width, rows=rows, act=_silu)
    q, k = mlstm_qk(xc, wq, wk)
    gates_t = jnp.swapaxes(p_g[..., :ML_GATES], 1, 2).reshape(bn, ML_GATES, L // ML_CHUNK, ML_CHUNK)
    h2, cf, nf, mf = mlstm_scan(q, k, p_v, gates_t, gate_bias.reshape(ML_GATES), state)
    return mlstm_output(h2, p_o, xc, gain, skip), (cf, nf, mf)


PROJ_SPLITS = ((0, 3 * HY_W), (3 * HY_W, 3 * HY_W + S5_W),
               (3 * HY_W + S5_W, 3 * HY_W + S5_W + ML_W),
               (3 * HY_W + S5_W + ML_W, 3 * HY_W + S5_W + 2 * ML_W),
               (3 * HY_W + S5_W + 2 * ML_W, 3 * HY_W + S5_W + 3 * ML_W),
               (3 * HY_W + S5_W + 3 * ML_W, 3 * HY_W + S5_W + 3 * ML_W + LANES))
PROJ_DTYPES = (BF16, F32, BF16, BF16, BF16, F32)


def kernel(x, c, ctx, c_ctx, w_mod, b_mod, g_pre_mix, g_post_mix, g_pre_mlp, g_post_mlp, w_in, w_out, hy_conv, hy_w1, hy_b1, hy_w2, hy_b2, hy_w3, hy_freq, hy_decay, hy_bias, s5_a_re, s5_a_im, s5_log_dt, s5_b_re, s5_b_im, s5_c_re, s5_c_im, s5_d, s5_w_glu, s5_b_glu, ml_conv, ml_wq, ml_wk, ml_gate_bias, ml_skip, ml_norm_gain, w_mlp1, w_mlp2):
    bsz, seq_len, d = x.shape
    ctx_len = ctx.shape[1]
    depth = w_mod.shape[0]

    cc = jnp.zeros((SUBLANES, d), F32).at[:bsz].set(c).at[bsz].set(c_ctx)
    mods = mod_vectors(cc, w_mod, b_mod)

    proj_w = w_in.shape[-1]
    w_in_b = jnp.pad(w_in, ((0, 0), (0, 0), (0, PROJ_SPLITS[-1][1] - proj_w))).astype(BF16)
    w_out_b, w1_b, w2_b = w_out.astype(BF16), w_mlp1.astype(BF16), w_mlp2.astype(BF16)
    wq_b, wk_b, wglu_b = ml_wq.astype(BF16), ml_wk.astype(BF16), s5_w_glu.astype(BF16)

    s5_zero = jnp.zeros((bsz, 1, 4 * S5_GROUPS * S5_N), F32)
    ml_zero = (jnp.zeros((bsz, 2, ML_HEADS, ML_HD, ML_HD), F32),
               jnp.zeros((bsz, 2, ML_HEADS, 1, ML_HD), F32),
               jnp.zeros((bsz, 2, ML_HEADS, 1, LANES), F32))

    for l in range(depth):
        mx = [mods[l, :bsz, j * d:(j + 1) * d][:, None, :] for j in range(6)]
        mc = [jnp.broadcast_to(mods[l, bsz, j * d:(j + 1) * d][None, None, :], (bsz, 1, d)) for j in range(6)]
        filt_x = hyena_filters(seq_len, hy_w1[l], hy_b1[l], hy_w2[l], hy_b2[l], hy_w3[l], hy_freq[l], hy_decay[l])
        filt_c = hyena_filters(ctx_len, hy_w1[l], hy_b1[l], hy_w2[l], hy_b2[l], hy_w3[l], hy_freq[l], hy_decay[l])
        s5_mats = s5_matrices(s5_a_re[l], s5_a_im[l], s5_log_dt[l], s5_b_re[l], s5_b_im[l], s5_c_re[l], s5_c_im[l])
        conv2d_w = ml_conv[l].reshape(9, ML_W)

        def mix(h_in, mod, L, fft_len, filt, width, rows, s5_h0, ml_state):
            p_hy, p_s5, p_mx, p_mv, p_mo, p_mg = in_projection(h_in, g_pre_mix[l], mod[1], mod[0], w_in_b[l],
                                                               PROJ_SPLITS, PROJ_DTYPES)
            y_hy = hyena_mix(p_hy, hy_conv[l], filt, hy_bias[l], fft_len)
            y_s5, s5_fin = s5_mix(p_s5, s5_mats, s5_d[l], wglu_b[l], s5_b_glu[l], s5_h0)
            y_ml, ml_fin = mlstm_mix(p_mx, p_mv, p_mo, p_mg, conv2d_w, width, rows, wq_b[l], wk_b[l],
                                     ml_gate_bias[l], ml_skip[l], ml_norm_gain[l], ml_state)
            return (y_hy, y_s5, y_ml), s5_fin, ml_fin

        y_c, s5_state, ml_state = mix(ctx, mc, ctx_len, CTX_FFT_LEN, filt_c, ctx_len, (0,), s5_zero, ml_zero)
        y_x, _, _ = mix(x, mx, seq_len, seq_len, filt_x, GRID_W, (-1, 0, 1), s5_state, ml_state)
        x = out_projection(*y_x, w_out_b[l], g_post_mix[l], mx[2], x)
        x = mlp_block(x, g_pre_mlp[l], mx[4], mx[3], w1_b[l], w2_b[l], g_post_mlp[l], mx[5])
        if l < depth - 1:
            ctx = out_projection(*y_c, w_out_b[l], g_post_mix[l], mc[2], ctx)
            ctx = mlp_block(ctx, g_pre_mlp[l], mc[4], mc[3], w1_b[l], w2_b[l], g_post_mlp[l], mc[5])
    return x
```

```python
import functools
import math

import numpy as np
import jax
import jax.numpy as jnp
from jax import lax
from jax.experimental import pallas as pl
from jax.experimental.pallas import tpu as pltpu

F32 = jnp.float32
BF16 = jnp.bfloat16
EPS = 1e-6
HIGHEST = lax.Precision.HIGHEST

V7X_VMEM_BYTES = 64 * 1024 * 1024
VMEM_LIMIT = V7X_VMEM_BYTES - 8 * 1024 * 1024
LANES = 128
SUBLANES = 8
FFT_N2 = 64
FFT_UNROLL_A = 4
FFT_UNROLL_B = 8

HY_W = 256
HY_BANDS = 8
HY_EMB = 1 + 2 * HY_BANDS
HY_FF = 64
S5_W = 256
S5_CH = 16
S5_GROUPS = S5_W // S5_CH
S5_N = 64
S5_T = 8
ML_HEADS = 4
ML_HD = 128
ML_W = ML_HEADS * ML_HD
ML_CHUNK = 64
ML_GATES = 4 * ML_HEADS
GRID_W = 64
CTX_FFT_LEN = 512


def _cparams(sem):
    return pltpu.CompilerParams(dimension_semantics=sem, vmem_limit_bytes=VMEM_LIMIT)


def _dot(a, b):
    return jnp.dot(a, b, preferred_element_type=F32)


def _dot_nt(a, b):
    return lax.dot_general(a, b, (((1,), (1,)), ((), ())), preferred_element_type=F32)


def _rms(x, g):
    return x * lax.rsqrt(jnp.mean(x * x, axis=-1, keepdims=True) + EPS) * g


def _silu(x):
    return x * jax.nn.sigmoid(x)


def _mod_kernel(c_ref, w_ref, b_ref, o_ref):
    s = _silu(c_ref[...]).astype(BF16)
    o_ref[0] = _dot(s, w_ref[0].astype(BF16)) + b_ref[0]


def mod_vectors(cc, w_mod, b_mod):
    depth, d, n = w_mod.shape
    r = cc.shape[0]
    tn = 1536
    return pl.pallas_call(
        _mod_kernel,
        grid=(depth, n // tn),
        in_specs=[pl.BlockSpec((r, d), lambda l, j: (0, 0)),
                  pl.BlockSpec((1, d, tn), lambda l, j: (l, 0, j)),
                  pl.BlockSpec((1, 1, tn), lambda l, j: (l, 0, j))],
        out_specs=pl.BlockSpec((1, r, tn), lambda l, j: (l, 0, j)),
        out_shape=jax.ShapeDtypeStruct((depth, r, n), F32),
        compiler_params=_cparams(("arbitrary", "arbitrary")),
        name="mod_vectors",
    )(cc, w_mod, b_mod.reshape(depth, 1, n))


def _row_tile(L):
    return min(L, 512)


def _inproj_kernel(x_ref, g_ref, sc_ref, sh_ref, w_ref, *o_refs, splits):
    h = _rms(x_ref[0], g_ref[...]) * (1.0 + sc_ref[0]) + sh_ref[0]
    hb = h.astype(BF16)
    for o_ref, (a, b) in zip(o_refs, splits):
        o_ref[0] = _dot(hb, w_ref[:, a:b]).astype(o_ref.dtype)


def in_projection(x, g, sc, sh, w, splits, dtypes):
    bn, L, d = x.shape
    tm = _row_tile(L)
    vec = pl.BlockSpec((1, 1, d), lambda b, i: (b, 0, 0))
    return pl.pallas_call(
        functools.partial(_inproj_kernel, splits=splits),
        grid=(bn, L // tm),
        in_specs=[pl.BlockSpec((1, tm, d), lambda b, i: (b, i, 0)),
                  pl.BlockSpec((1, d), lambda b, i: (0, 0)), vec, vec,
                  pl.BlockSpec(w.shape, lambda b, i: (0, 0))],
        out_specs=[pl.BlockSpec((1, tm, b_ - a_), lambda b, i: (b, i, 0)) for a_, b_ in splits],
        out_shape=[jax.ShapeDtypeStruct((bn, L, b_ - a_), dt) for (a_, b_), dt in zip(splits, dtypes)],
        compiler_params=_cparams(("arbitrary", "arbitrary")),
        name="in_projection",
    )(x, g.reshape(1, d), sc, sh, w)


def _mlstm_gated(h, p_o, xc, gain, skip):
    parts = []
    for hd in range(ML_HEADS):
        sl = slice(hd * ML_HD, (hd + 1) * ML_HD)
        hh = h[:, sl]
        parts.append(hh * lax.rsqrt(jnp.mean(hh * hh, axis=-1, keepdims=True) + EPS) * gain[:, sl])
    return jax.nn.sigmoid(p_o) * (jnp.concatenate(parts, axis=-1) + skip * xc)


def _outproj_kernel(yh_ref, ys0_ref, ys1_ref, hf_ref, hb_ref, po_ref, xc_ref, gain_ref, skip_ref,
                    w_ref, g_ref, gt_ref, x_ref, o_ref):
    a = yh_ref.shape[-1]
    ym = _mlstm_gated(hf_ref[0, 0] + hb_ref[0, 0], po_ref[0].astype(F32), xc_ref[0], gain_ref[...], skip_ref[...])
    acc = _dot(yh_ref[0].astype(BF16), w_ref[0:a])
    acc += _dot(ys0_ref[0, 0].astype(BF16), w_ref[a:a + LANES])
    acc += _dot(ys1_ref[0, 0].astype(BF16), w_ref[a + LANES:a + 2 * LANES])
    acc += _dot(ym.astype(BF16), w_ref[a + 2 * LANES:])
    o_ref[0] = x_ref[0] + gt_ref[0] * _rms(acc, g_ref[...])


def out_projection(y_hy, y_s5, ml_parts, w, g, gate, x):
    h2, p_o, xc, gain, skip = ml_parts
    bn, L, d = x.shape
    tm = _row_tile(L)
    tok = lambda wd: pl.BlockSpec((1, tm, wd), lambda b, i: (b, i, 0))
    half = lambda h: pl.BlockSpec((1, 1, tm, LANES), lambda b, i: (h, b, i, 0))
    hdir = lambda dr: pl.BlockSpec((1, 1, tm, ML_W), lambda b, i: (dr, b, i, 0))
    row = pl.BlockSpec((1, ML_W), lambda b, i: (0, 0))
    return pl.pallas_call(
        _outproj_kernel,
        grid=(bn, L // tm),
        in_specs=[tok(y_hy.shape[-1]), half(0), half(1), hdir(0), hdir(1), tok(ML_W), tok(ML_W), row, row,
                  pl.BlockSpec(w.shape, lambda b, i: (0, 0)),
                  pl.BlockSpec((1, d), lambda b, i: (0, 0)),
                  pl.BlockSpec((1, 1, d), lambda b, i: (b, 0, 0)), tok(d)],
        out_specs=tok(d),
        out_shape=jax.ShapeDtypeStruct((bn, L, d), F32),
        compiler_params=_cparams(("arbitrary", "arbitrary")),
        name="out_projection",
    )(y_hy, y_s5, y_s5, h2, h2, p_o, xc, gain.reshape(1, ML_W), skip.reshape(1, ML_W), w, g.reshape(1, d), gate, x)


def _mlp_kernel(x_ref, g1_ref, sc_ref, sh_ref, w1_ref, w2_ref, g2_ref, gt_ref, o_ref, h_ref, acc_ref):
    j = pl.program_id(2)

    @pl.when(j == 0)
    def _():
        h = _rms(x_ref[0], g1_ref[...]) * (1.0 + sc_ref[0]) + sh_ref[0]
        h_ref[...] = h.astype(BF16)
        acc_ref[...] = jnp.zeros_like(acc_ref)

    a = jnp.maximum(_dot(h_ref[...], w1_ref[...]), 0.0)
    acc_ref[...] += _dot((a * a).astype(BF16), w2_ref[...])

    @pl.when(j == pl.num_programs(2) - 1)
    def _():
        o_ref[0] = x_ref[0] + gt_ref[0] * _rms(acc_ref[...], g2_ref[...])


def mlp_block(x, g_pre, sc, sh, w1, w2, g_post, gate):
    bn, L, d = x.shape
    hid = w1.shape[1]
    tm = min(L, 1024)
    th = 2048
    tok = pl.BlockSpec((1, tm, d), lambda b, i, j: (b, i, 0))
    vec = pl.BlockSpec((1, 1, d), lambda b, i, j: (b, 0, 0))
    gain = pl.BlockSpec((1, d), lambda b, i, j: (0, 0))
    return pl.pallas_call(
        _mlp_kernel,
        grid=(bn, L // tm, hid // th),
        in_specs=[tok, gain, vec, vec,
                  pl.BlockSpec((d, th), lambda b, i, j: (0, j)),
                  pl.BlockSpec((th, d), lambda b, i, j: (j, 0)),
                  gain, vec],
        out_specs=tok,
        out_shape=jax.ShapeDtypeStruct((bn, L, d), F32),
        scratch_shapes=[pltpu.VMEM((tm, d), BF16), pltpu.VMEM((tm, d), F32)],
        compiler_params=_cparams(("arbitrary", "arbitrary", "arbitrary")),
        name="mlp_block",
    )(x, g_pre.reshape(1, d), sc, sh, w1, w2, g_post.reshape(1, d), gate)


def _dwconv_kernel(x_ref, w_ref, o_ref, pad_ref, *, width, taps, act, pad):
    L = x_ref.shape[1]
    ch = x_ref.shape[2]
    pad_ref[0:pad, :] = jnp.zeros((pad, ch), F32)
    pad_ref[pad + L:pad + L + pad, :] = jnp.zeros((pad, ch), F32)
    pad_ref[pad:pad + L, :] = x_ref[0].astype(F32)
    tr = min(L, 256)
    for r0 in range(0, L, tr):
        col = (lax.broadcasted_iota(jnp.int32, (tr, ch), 0) + r0) & (width - 1)
        acc = jnp.zeros((tr, ch), F32)
        for dr, dc in taps:
            start = pad + r0 + dr * width + dc
            v = pad_ref[start:start + tr, :]
            if dc == -1:
                v = jnp.where(col >= 1, v, 0.0)
            elif dc == 1:
                v = jnp.where(col <= width - 2, v, 0.0)
            acc = acc + w_ref[(dr + 1) * 3 + (dc + 1):(dr + 1) * 3 + (dc + 2), :] * v
        o_ref[0, r0:r0 + tr, :] = act(acc)


def depthwise_conv(x, w9, width, rows, act=None):
    bn, L, ch = x.shape
    assert width & (width - 1) == 0
    taps = tuple((dr, dc) for dr in rows for dc in (-1, 0, 1))
    pad = -(-(width + 1) // SUBLANES) * SUBLANES if len(rows) > 1 else SUBLANES
    act = act or (lambda a: a)
    cs = LANES
    return pl.pallas_call(
        functools.partial(_dwconv_kernel, width=width, taps=taps, act=act, pad=pad),
        grid=(bn, ch // cs),
        in_specs=[pl.BlockSpec((1, L, cs), lambda b, c: (b, 0, c)),
                  pl.BlockSpec((9, cs), lambda b, c: (0, c))],
        out_specs=pl.BlockSpec((1, L, cs), lambda b, c: (b, 0, c)),
        out_shape=jax.ShapeDtypeStruct((bn, L, ch), F32),
        scratch_shapes=[pltpu.VMEM((L + 2 * pad, cs), F32)],
        compiler_params=_cparams(("arbitrary", "arbitrary")),
        name="depthwise_conv",
    )(x, w9)


def _hyfilt_kernel(w1_ref, b1_ref, w2_ref, b2_ref, w3_ref, fr_ref, dec_ref, o_ref, *, L):
    tr = min(L, 512)
    wd = o_ref.shape[1]
    fr = fr_ref[...]
    ssq = jnp.zeros((1, wd), F32)
    for r0 in range(0, L, tr):
        t = (lax.broadcasted_iota(jnp.int32, (tr, 32), 0) + r0).astype(F32) / L
        lane = lax.broadcasted_iota(jnp.int32, (tr, 32), 1)
        band = jnp.where(lane <= HY_BANDS, lane, lane - HY_BANDS).astype(F32)
        ang = 2.0 * math.pi * t * band
        feat = jnp.where(lane == 0, t, jnp.where(lane <= HY_BANDS, jnp.cos(ang),
                                                 jnp.where(lane <= 2 * HY_BANDS, jnp.sin(ang), 0.0)))
        hdn = jnp.sin(fr * (jnp.dot(feat, w1_ref[...], precision=HIGHEST, preferred_element_type=F32) + b1_ref[...]))
        hdn = jnp.sin(fr * (jnp.dot(hdn, w2_ref[...], precision=HIGHEST, preferred_element_type=F32) + b2_ref[...]))
        filt = _dot(hdn.astype(BF16), w3_ref[...].astype(BF16))
        filt = filt * jnp.exp(-t[:, 0:1] * dec_ref[...])
        ssq = ssq + jnp.sum(filt * filt, axis=0, keepdims=True)
        o_ref[r0:r0 + tr, :] = filt
    scales = []
    for o in range(wd // (2 * HY_W)):
        tot = ssq[:, 2 * o * HY_W:(2 * o + 1) * HY_W] + ssq[:, (2 * o + 1) * HY_W:(2 * o + 2) * HY_W]
        scales += [lax.rsqrt(tot + EPS)] * 2
    scale = jnp.concatenate(scales, axis=1)
    for r0 in range(0, L, tr):
        o_ref[r0:r0 + tr, :] = o_ref[r0:r0 + tr, :] * scale


def hyena_filters(L, w1, b1, w2, b2, w3, freq, decay):
    wd = decay.shape[0] * 2 * HY_W
    w1p = jnp.zeros((32, HY_FF), F32).at[:HY_EMB].set(w1)
    full = lambda a: pl.BlockSpec(a.shape, lambda o: (0,) * a.ndim)
    args = (w1p, b1.reshape(1, HY_FF), w2, b2.reshape(1, HY_FF), w3, freq.reshape(1, HY_FF), decay.reshape(1, wd))
    return pl.pallas_call(
        functools.partial(_hyfilt_kernel, L=L),
        grid=(1,),
        in_specs=[full(a) for a in args],
        out_specs=pl.BlockSpec((L, wd), lambda o: (0, 0)),
        out_shape=jax.ShapeDtypeStruct((L, wd), F32),
        compiler_params=_cparams(("arbitrary",)),
        name="hyena_filters",
    )(*args)


@functools.lru_cache(maxsize=None)
def _dft_consts(L):
    n = 2 * L
    n2c = FFT_N2
    n1c = n // n2c
    n1h = n1c // 2
    n1 = np.arange(n1h)[None, None, :]
    k1 = np.arange(n1c)[None, :, None]
    n2 = np.arange(n2c)[:, None, None]
    ang = -2.0 * np.pi * (k1 * n1 / n1c + n2 * k1 / n)
    fr, fi = np.cos(ang), np.sin(ang)
    fa = np.concatenate([np.concatenate([fr, -fi], axis=2),
                         np.concatenate([fi, fr], axis=2)], axis=1)
    frt = np.transpose(fr, (0, 2, 1)) / n
    fit = -np.transpose(fi, (0, 2, 1)) / n
    fai = np.concatenate([np.concatenate([frt, -fit], axis=2),
                          np.concatenate([fit, frt], axis=2)], axis=1)
    k2 = np.arange(n2c)[:, None]
    m2 = np.arange(n2c)[None, :]
    angb = -2.0 * np.pi * k2 * m2 / n2c
    gr, gi = np.cos(angb), np.sin(angb)
    gb = np.block([[gr, -gi], [gi, gr]])
    gbi = np.block([[gr, gi], [-gi, gr]])
    return (fa.astype(np.float32), gb.astype(np.float32), gbi.astype(np.float32), fai.astype(np.float32))


def _slab_rows(n1c):
    return 2 * n1c + SUBLANES


def _slab(n2, n1c):
    return pl.ds(pl.multiple_of(n2 * _slab_rows(n1c), SUBLANES), 2 * n1c)


def _fft_stage_a(load_rhs, fa_ref, s_ref, n1c):
    def body(n2, carry):
        rhs = load_rhs(n2).astype(BF16)
        s_ref[_slab(n2, n1c), :] = _dot(fa_ref[n2], rhs)
        return carry
    lax.fori_loop(0, FFT_N2, body, 0, unroll=FFT_UNROLL_A)


def _fft_load_k1(s_ref, k1, n1c):
    xr = s_ref[pl.ds(k1, FFT_N2, stride=_slab_rows(n1c)), :]
    xi = s_ref[pl.ds(n1c + k1, FFT_N2, stride=_slab_rows(n1c)), :]
    return jnp.concatenate([xr, xi], axis=0)


def _fftconv_kernel(z_ref, gate_ref, bias_ref, h_ref, fa_ref, gb_ref, gbi_ref, fai_ref, o_ref, s_ref, *, L):
    n1c = L // 32
    n1h = n1c // 2
    half = FFT_N2

    def load_rhs(n2):
        za = z_ref[0, 0, pl.ds(n2, n1h, stride=FFT_N2), :]
        zb = z_ref[0, 1, pl.ds(n2, n1h, stride=FFT_N2), :]
        return jnp.concatenate([za, zb], axis=0)
    _fft_stage_a(load_rhs, fa_ref, s_ref, n1c)

    def stage_b(k1, carry):
        x = _dot(gb_ref[...], _fft_load_k1(s_ref, k1, n1c).astype(BF16))
        xr, xi = x[:half], x[half:]
        h = h_ref[k1]
        hr, hi = h[:half], h[half:]
        y = jnp.concatenate([xr * hr - xi * hi, xr * hi + xi * hr], axis=0)
        bp = _dot(gbi_ref[...], y.astype(BF16))
        s_ref[pl.ds(k1, FFT_N2, stride=_slab_rows(n1c)), :] = bp[:half]
        s_ref[pl.ds(n1c + k1, FFT_N2, stride=_slab_rows(n1c)), :] = bp[half:]
        return carry
    lax.fori_loop(0, n1c, stage_b, 0, unroll=FFT_UNROLL_B)

    bias = bias_ref[...]

    def stage_a_inv(n2, carry):
        rhs = s_ref[_slab(n2, n1c), :].astype(BF16)
        res = _dot(fai_ref[n2], rhs)
        for p in range(2):
            rows = pl.ds(n2, n1h, stride=FFT_N2)
            zin = z_ref[0, p, rows, :]
            o_ref[0, p, rows, :] = gate_ref[0, p, rows, :] * (res[p * n1h:(p + 1) * n1h] + bias * zin)
        return carry
    lax.fori_loop(0, FFT_N2, stage_a_inv, 0, unroll=FFT_UNROLL_A)


def fft_gated_conv(z, zoff, gate, goff, bias, hspec):
    bsz, L, _ = z.shape
    ch = bias.shape[0]
    n1c = L // 32
    fa, gb, gbi, fai = (jnp.asarray(a).astype(BF16) for a in _dft_consts(L))
    zp = z.reshape(bsz // 2, 2, L, z.shape[-1])
    gp = gate.reshape(bsz // 2, 2, L, gate.shape[-1])
    cs = LANES
    one = pl.Buffered(1)
    const3 = lambda a: pl.BlockSpec(a.shape, lambda c, p: (0, 0, 0), pipeline_mode=one)
    const2 = lambda a: pl.BlockSpec(a.shape, lambda c, p: (0, 0), pipeline_mode=one)
    out = pl.pallas_call(
        functools.partial(_fftconv_kernel, L=L),
        grid=(ch // cs, bsz // 2),
        in_specs=[pl.BlockSpec((1, 2, L, cs), lambda c, p: (p, 0, 0, c + zoff)),
                  pl.BlockSpec((1, 2, L, cs), lambda c, p: (p, 0, 0, c + goff)),
                  pl.BlockSpec((1, cs), lambda c, p: (0, c)),
                  pl.BlockSpec((n1c, 2 * FFT_N2, cs), lambda c, p: (0, 0, c), pipeline_mode=one),
                  const3(fa), const2(gb), const2(gbi), const3(fai)],
        out_specs=pl.BlockSpec((1, 2, L, cs), lambda c, p: (p, 0, 0, c)),
        out_shape=jax.ShapeDtypeStruct((bsz // 2, 2, L, ch), F32),
        scratch_shapes=[pltpu.VMEM((FFT_N2 * _slab_rows(n1c), cs), F32)],
        compiler_params=_cparams(("arbitrary", "arbitrary")),
        name="fft_gated_conv",
    )(zp, gp, bias.reshape(1, ch), hspec, fa, gb, gbi, fai)
    return out.reshape(bsz, L, ch)


def _fftspec_kernel(hf_ref, hb_ref, fa_ref, gb_ref, o_ref, s_ref, *, L):
    n1c = L // 32
    n1h = n1c // 2
    half = FFT_N2
    for d, src in enumerate((hf_ref, hb_ref)):
        def load_rhs(n2, src=src, d=d):
            h = src[pl.ds(n2, n1h, stride=FFT_N2), :]
            if d == 1:
                row = lax.broadcasted_iota(jnp.int32, h.shape, 0)
                h = jnp.where((row == 0) & (n2 == 0), 0.0, h)
            return jnp.concatenate([h, jnp.zeros_like(h)], axis=0)
        _fft_stage_a(load_rhs, fa_ref, s_ref, n1c)

        def stage_b(k1, carry, d=d):
            x = _dot(gb_ref[...], _fft_load_k1(s_ref, k1, n1c).astype(BF16))
            if d == 0:
                o_ref[k1] = x
            else:
                o_ref[k1] = o_ref[k1] + jnp.concatenate([x[:half], -x[half:]], axis=0)
            return carry
        lax.fori_loop(0, n1c, stage_b, 0, unroll=FFT_UNROLL_B)


def fft_filter_spectrum(filt, foff, boff, L):
    if filt.shape[0] < L:
        filt = jnp.pad(filt, ((0, L - filt.shape[0]), (0, 0)))
    n1c = L // 32
    fa, gb, _, _ = (jnp.asarray(a).astype(BF16) for a in _dft_consts(L))
    cs = LANES
    return pl.pallas_call(
        functools.partial(_fftspec_kernel, L=L),
        grid=(HY_W // cs,),
        in_specs=[pl.BlockSpec((L, cs), lambda c: (0, c + foff)), pl.BlockSpec((L, cs), lambda c: (0, c + boff)),
                  pl.BlockSpec(fa.shape, lambda c: (0, 0, 0)), pl.BlockSpec(gb.shape, lambda c: (0, 0))],
        out_specs=pl.BlockSpec((n1c, 2 * FFT_N2, cs), lambda c: (0, 0, c)),
        out_shape=jax.ShapeDtypeStruct((n1c, 2 * FFT_N2, HY_W), F32),
        scratch_shapes=[pltpu.VMEM((FFT_N2 * _slab_rows(n1c), cs), F32)],
        compiler_params=_cparams(("arbitrary",)),
        name="fft_filter_spectrum",
    )(filt, filt, fa, gb)


def hyena_mix(p_hy, conv_w, filt, bias, fft_len):
    bsz, L, _ = p_hy.shape
    w9 = jnp.zeros((9, 3 * HY_W), F32).at[3:6].set(conv_w)
    u = depthwise_conv(p_hy, w9, width=L, rows=(0,))
    if fft_len > L:
        u = jnp.pad(u, ((0, 0), (0, fft_len - L), (0, 0)))
    nb = HY_W // LANES
    z = fft_gated_conv(u, 0, u, nb, bias[0], fft_filter_spectrum(filt, 0, nb, fft_len))
    z = fft_gated_conv(z, 0, u, 2 * nb, bias[1], fft_filter_spectrum(filt, 2 * nb, 3 * nb, fft_len))
    return z[:, :L]


def s5_matrices(a_re, a_im, log_dt, b_re, b_im, c_re, c_im):
    t = S5_T
    sw = S5_GROUPS * S5_N
    dt = jnp.exp(log_dt)[..., None]
    mag = jnp.exp(a_re * dt)
    ar, ai = mag * jnp.cos(a_im * dt), mag * jnp.sin(a_im * dt)
    den = a_re * a_re + a_im * a_im
    qr = ((ar - 1.0) * a_re + ai * a_im) / den
    qi = (ai * a_re - (ar - 1.0) * a_im) / den
    bbr = qr[..., None] * b_re - qi[..., None] * b_im
    bbi = qr[..., None] * b_im + qi[..., None] * b_re
    pr, pi = [jnp.ones_like(ar)], [jnp.zeros_like(ai)]
    for _ in range(t):
        pr, pi = pr + [pr[-1] * ar - pi[-1] * ai], pi + [pr[-1] * ai + pi[-1] * ar]
    pr = [p.reshape(2, sw) for p in pr]
    pi = [p.reshape(2, sw) for p in pi]
    gd_hn = (np.arange(S5_W)[:, None] // S5_CH == np.arange(sw)[None, :] // S5_N).astype(np.float32)
    gn_hc = gd_hn.T
    expand_b = lambda b: jnp.tile(jnp.swapaxes(b, -1, -2).reshape(2, S5_W, S5_N), (1, 1, S5_GROUPS)) * gd_hn
    expand_c = lambda c: jnp.tile(jnp.swapaxes(c, -1, -2).reshape(2, sw, S5_CH), (1, 1, S5_GROUPS)) * gn_hc
    b_r, b_i = expand_b(bbr), expand_b(bbi)
    c_r, c_i = expand_c(c_re), expand_c(c_im)
    col = lambda p, x: p[x][None, :]
    rowv = lambda p, x: p[x][:, None]
    r_in = jnp.concatenate([
        jnp.concatenate([b_r[0] * col(pr[t - 1 - s], 0) - b_i[0] * col(pi[t - 1 - s], 0),
                         b_r[0] * col(pi[t - 1 - s], 0) + b_i[0] * col(pr[t - 1 - s], 0),
                         b_r[1] * col(pr[s], 1) - b_i[1] * col(pi[s], 1),
                         b_r[1] * col(pi[s], 1) + b_i[1] * col(pr[s], 1)], axis=1)
        for s in range(t)], axis=0)

    def scaled_c(x, j):
        return (c_r[x] * rowv(pr[j], x) - c_i[x] * rowv(pi[j], x), c_r[x] * rowv(pi[j], x) + c_i[x] * rowv(pr[j], x))
    blocks = []
    for tt in range(t):
        fr, fi = scaled_c(0, tt + 1)
        br, bi = scaled_c(1, t - tt)
        blocks.append(jnp.concatenate([fr, -fi, br, -bi], axis=0))
    r_out = jnp.concatenate(blocks, axis=1)
    lags = (list(range(t)), list(range(t - 1, -1, -1)))
    taps = []
    for x in range(2):
        sc = [scaled_c(x, j) for j in lags[x]]
        taps.append(s5_taps(b_r[x], b_i[x], jnp.concatenate([p[0] for p in sc], axis=1),
                            jnp.concatenate([p[1] for p in sc], axis=1)))
    w = S5_W
    m_intra = jnp.concatenate([
        jnp.pad(taps[0][:, :(t - s) * w], ((0, 0), (s * w, 0)))
        + jnp.pad(taps[1][:, (t - 1 - s) * w:], ((0, 0), (0, (t - 1 - s) * w)))
        for s in range(t)], axis=0)
    a8v = jnp.concatenate([pr[t][0], pi[t][0], pr[t][1], pi[t][1]])[None, :]
    return m_intra.astype(BF16), r_in.astype(BF16), r_out.astype(BF16), a8v


def _s5taps_kernel(br_ref, bi_ref, cr_ref, ci_ref, o_ref):
    o_ref[...] = (_dot(br_ref[...].astype(BF16), cr_ref[...].astype(BF16))
                  - _dot(bi_ref[...].astype(BF16), ci_ref[...].astype(BF16)))


def s5_taps(b_r, b_i, c_r, c_i):
    m, n = b_r.shape[0], c_r.shape[1]
    tn = 512
    lhs = pl.BlockSpec(b_r.shape, lambda j: (0, 0))
    rhs = pl.BlockSpec((c_r.shape[0], tn), lambda j: (0, j))
    return pl.pallas_call(
        _s5taps_kernel,
        grid=(n // tn,),
        in_specs=[lhs, lhs, rhs, rhs],
        out_specs=pl.BlockSpec((m, tn), lambda j: (0, j)),
        out_shape=jax.ShapeDtypeStruct((m, n), F32),
        compiler_params=_cparams(("arbitrary",)),
        name="s5_taps",
    )(b_r, b_i, c_r, c_i)


def _row_tokens(x0_ref, x1_ref, t, rows):
    sl = pl.ds(t, rows, stride=S5_T)
    return jnp.concatenate([x0_ref[0, sl, :], x1_ref[0, sl, :]], axis=1)


def _s5in_kernel(x0_ref, x1_ref, w_ref, o_ref):
    rows = o_ref.shape[1]
    wd = 2 * LANES
    acc = _dot(_row_tokens(x0_ref, x1_ref, 0, rows).astype(BF16), w_ref[0:wd, :])
    for t in range(1, S5_T):
        acc += _dot(_row_tokens(x0_ref, x1_ref, t, rows).astype(BF16), w_ref[t * wd:(t + 1) * wd, :])
    o_ref[0] = acc


def s5_state_inputs(x, w, tn):
    bn, L, wd = x.shape
    assert wd == 2 * LANES
    r = L // S5_T
    n = w.shape[1]
    tm = min(r, 256)
    half = lambda h: pl.BlockSpec((1, tm * S5_T, LANES), lambda j, b, i: (b, i, h))
    return pl.pallas_call(
        _s5in_kernel,
        grid=(n // tn, bn, r // tm),
        in_specs=[half(0), half(1), pl.BlockSpec((S5_T * wd, tn), lambda j, b, i: (0, j))],
        out_specs=pl.BlockSpec((1, tm, tn), lambda j, b, i: (b, i, j)),
        out_shape=jax.ShapeDtypeStruct((bn, r, n), F32),
        compiler_params=_cparams(("arbitrary", "arbitrary", "arbitrary")),
        name="s5_state_inputs",
    )(x, x, w)


def _s5out_kernel(x0_ref, x1_ref, m_ref, s_ref, r_ref, d_ref, wg_ref, bg_ref, o_ref):
    rows = s_ref.shape[1]
    wd = 2 * LANES
    xt = [_row_tokens(x0_ref, x1_ref, t, rows) for t in range(S5_T)]
    acc = _dot(s_ref[0].astype(BF16), r_ref[...])
    for t in range(S5_T):
        acc += _dot(xt[t].astype(BF16), m_ref[t * wd:(t + 1) * wd, :])
    for t in range(S5_T):
        g = jax.nn.gelu(acc[:, t * wd:(t + 1) * wd] + d_ref[...] * xt[t], approximate=True)
        y = g * jax.nn.sigmoid(_dot(g.astype(BF16), wg_ref[...]) + bg_ref[...])
        o_ref[0, 0, pl.ds(t, rows, stride=S5_T), :] = y[:, :LANES]
        o_ref[1, 0, pl.ds(t, rows, stride=S5_T), :] = y[:, LANES:]


def s5_readout_glu(x, m_intra, xs, r_out, d_skip, w_glu, b_glu):
    bn, L, wd = x.shape
    assert wd == 2 * LANES
    r = L // S5_T
    tm = min(r, 256)
    one = pl.Buffered(1)
    half = lambda h: pl.BlockSpec((1, tm * S5_T, LANES), lambda b, i: (b, i, h))
    row = pl.BlockSpec((1, wd), lambda b, i: (0, 0))
    return pl.pallas_call(
        _s5out_kernel,
        grid=(bn, r // tm),
        in_specs=[half(0), half(1), pl.BlockSpec(m_intra.shape, lambda b, i: (0, 0), pipeline_mode=one),
                  pl.BlockSpec((1, tm, xs.shape[-1]), lambda b, i: (b, i, 0)),
                  pl.BlockSpec(r_out.shape, lambda b, i: (0, 0), pipeline_mode=one),
                  row, pl.BlockSpec((wd, wd), lambda b, i: (0, 0)), row],
        out_specs=pl.BlockSpec((2, 1, tm * S5_T, LANES), lambda b, i: (0, b, i, 0)),
        out_shape=jax.ShapeDtypeStruct((2, bn, L, LANES), F32),
        compiler_params=_cparams(("arbitrary", "arbitrary")),
        name="s5_readout_glu",
    )(x, x, m_intra, xs, r_out, d_skip.reshape(1, wd), w_glu, b_glu.reshape(1, wd))


def _s5scan_kernel(r_ref, a_ref, h0_ref, x_ref, fin_ref):
    nk = r_ref.shape[1]
    sw = S5_GROUPS * S5_N
    ar_f, ai_f = a_ref[:, 0:sw], a_ref[:, sw:2 * sw]
    ar_b, ai_b = a_ref[:, 2 * sw:3 * sw], a_ref[:, 3 * sw:4 * sw]
    init = tuple(h0_ref[0, :, j * sw:(j + 1) * sw] for j in range(4))

    def body(i, carry):
        fr, fi, br, bi = carry
        k = nk - 1 - i
        x_ref[0, pl.ds(i, 1), 0:sw] = fr
        x_ref[0, pl.ds(i, 1), sw:2 * sw] = fi
        x_ref[0, pl.ds(k, 1), 2 * sw:3 * sw] = br
        x_ref[0, pl.ds(k, 1), 3 * sw:4 * sw] = bi
        rfr = r_ref[0, pl.ds(i, 1), 0:sw]
        rfi = r_ref[0, pl.ds(i, 1), sw:2 * sw]
        rbr = r_ref[0, pl.ds(k, 1), 2 * sw:3 * sw]
        rbi = r_ref[0, pl.ds(k, 1), 3 * sw:4 * sw]
        return (ar_f * fr - ai_f * fi + rfr, ar_f * fi + ai_f * fr + rfi,
                ar_b * br - ai_b * bi + rbr, ar_b * bi + ai_b * br + rbi)
    fin = lax.fori_loop(0, nk, body, init)
    for j in range(4):
        fin_ref[0, :, j * sw:(j + 1) * sw] = fin[j]


def s5_row_scan(r, a8v, h0):
    bn, nk, w = r.shape
    return pl.pallas_call(
        _s5scan_kernel,
        grid=(bn,),
        in_specs=[pl.BlockSpec((1, nk, w), lambda b: (b, 0, 0)),
                  pl.BlockSpec((1, w), lambda b: (0, 0)),
                  pl.BlockSpec((1, 1, w), lambda b: (b, 0, 0))],
        out_specs=[pl.BlockSpec((1, nk, w), lambda b: (b, 0, 0)),
                   pl.BlockSpec((1, 1, w), lambda b: (b, 0, 0))],
        out_shape=[jax.ShapeDtypeStruct((bn, nk, w), F32), jax.ShapeDtypeStruct((bn, 1, w), F32)],
        compiler_params=_cparams(("arbitrary",)),
        name="s5_row_scan",
    )(r, a8v, h0)


def s5_mix(p_s5, mats, d_skip, w_glu, b_glu, h0):
    m_intra, r_in, r_out, a8v = mats
    r = s5_state_inputs(p_s5, r_in, tn=1024)
    xs, fin = s5_row_scan(r, a8v, h0)
    return s5_readout_glu(p_s5, m_intra, xs, r_out, d_skip, w_glu, b_glu), fin


def _qk_kernel(x_ref, wq_ref, wk_ref, q_ref, k_ref):
    for h in range(ML_HEADS):
        sl = slice(h * ML_HD, (h + 1) * ML_HD)
        xb = x_ref[0, :, sl].astype(BF16)
        q_ref[0, :, sl] = _dot(xb, wq_ref[h]).astype(BF16)
        k_ref[0, :, sl] = (_dot(xb, wk_ref[h]) * (ML_HD ** -0.5)).astype(BF16)


def mlstm_qk(xc, wq, wk):
    bn, L, w = xc.shape
    tm = _row_tile(L)
    tok = pl.BlockSpec((1, tm, w), lambda b, i: (b, i, 0))
    wsp = pl.BlockSpec(wq.shape, lambda b, i: (0, 0, 0))
    return pl.pallas_call(
        _qk_kernel,
        grid=(bn, L // tm),
        in_specs=[tok, wsp, wsp],
        out_specs=[tok, tok],
        out_shape=[jax.ShapeDtypeStruct((bn, L, w), BF16)] * 2,
        compiler_params=_cparams(("arbitrary", "arbitrary")),
        name="mlstm_qk",
    )(xc, wq, wk)


def _mlstm_kernel(bias_ref, q_ref, k_ref, v_ref, g_ref, c0_ref, n0_ref, m0_ref,
                  h_ref, cf_ref, nf_ref, mf_ref, st_s, m_s, lf_s, b_s):
    hd = pl.program_id(0)
    bn = q_ref.shape[0]
    nc = g_ref.shape[2]
    t = ML_CHUNK
    chains = [(b, d) for b in range(bn) for d in range(2)]
    for i, (b, d) in enumerate(chains):
        st_s[i, :, 0:ML_HD] = c0_ref[b, d, 0].T
        st_s[i, :, ML_HD:] = jnp.broadcast_to(n0_ref[b, d, 0], (ML_HD, ML_HD)).T
        m_s[i] = m0_ref[b, d, 0]
    ones_blk = jnp.ones((t, ML_HD), BF16)
    row = lax.broadcasted_iota(jnp.int32, (t, t), 0)
    col = lax.broadcasted_iota(jnp.int32, (t, t), 1)
    tri = (col <= row, col >= row)
    cum = ((row <= col).astype(F32), (row >= col).astype(F32))
    bias_i = [bias_ref[d * 2 * ML_HEADS + hd] for d in range(2)]
    bias_f = [bias_ref[d * 2 * ML_HEADS + ML_HEADS + hd] for d in range(2)]
    for i, (b, d) in enumerate(chains):
        lf_all = jax.nn.log_sigmoid(g_ref[b, d * 2 * ML_HEADS + ML_HEADS + hd] + bias_f[d])
        lf_s[i] = lf_all
        b_s[i] = jnp.dot(lf_all, cum[d], precision=HIGHEST, preferred_element_type=F32)

    def body(j, carry):
        ids = range(len(chains))
        cidx = [j if d == 0 else nc - 1 - j for _, d in chains]
        r0 = [pl.multiple_of(c * t, t) for c in cidx]
        q = [q_ref[b, pl.ds(r0[i], t), :] for i, (b, d) in enumerate(chains)]
        k = [k_ref[b, pl.ds(r0[i], t), :] for i, (b, d) in enumerate(chains)]
        v = [v_ref[b, pl.ds(r0[i], t), :] for i, (b, d) in enumerate(chains)]
        li = [g_ref[b, d * 2 * ML_HEADS + hd, pl.ds(cidx[i], 1), :] + bias_i[d] for i, (b, d) in enumerate(chains)]
        lf = [lf_s[i, pl.ds(cidx[i], 1), :] for i in ids]
        b_row = [b_s[i, pl.ds(cidx[i], 1), :] for i in ids]
        b_col = [jnp.sum(jnp.where(tri[d], lf[i], 0.0), axis=-1, keepdims=True) for i, (b, d) in enumerate(chains)]
        g = [jnp.sum(lf[i], axis=-1, keepdims=True) for i in ids]
        a_row = [g[i] - b_row[i] + li[i] for i in ids]
        m_loc = [jnp.max(a_row[i], axis=-1, keepdims=True) for i in ids]
        w_row = [jnp.exp(a_row[i] - m_loc[i]) for i in ids]
        st_prev = [st_s[i] for i in ids]
        m_prev = [m_s[i][:, 0:1] for i in ids]
        kt = [k[i].astype(F32).T for i in ids]
        qr = [_dot(q[i], jnp.concatenate([st_prev[i].astype(BF16), kt[i].astype(BF16)], axis=1)) for i in ids]
        v1 = [jnp.concatenate([v[i], ones_blk], axis=1) for i in ids]
        upd = [_dot((kt[i] * w_row[i]).astype(BF16), v1[i]) for i in ids]
        dmat = [jnp.where(tri[d], b_col[i] - b_row[i] + li[i], -jnp.inf) for i, (b, d) in enumerate(chains)]
        inter = [b_col[i] + m_prev[i] for i in ids]
        m_t = [jnp.maximum(jnp.max(dmat[i], axis=-1, keepdims=True), inter[i]) for i in ids]
        s = [qr[i][:, 2 * ML_HD:] * jnp.exp(dmat[i] - m_t[i]) for i in ids]
        w_inter = [jnp.exp(inter[i] - m_t[i]) for i in ids]
        sv = [_dot(s[i].astype(BF16), v1[i]) for i in ids]
        nq = [sv[i][:, ML_HD:ML_HD + 1] + w_inter[i] * qr[i][:, ML_HD:ML_HD + 1] for i in ids]
        for i, (b, d) in enumerate(chains):
            num = sv[i][:, :ML_HD] + w_inter[i] * qr[i][:, :ML_HD]
            h_ref[d, b, pl.ds(r0[i], t), :] = num / jnp.maximum(jnp.abs(nq[i]), jnp.exp(-m_t[i]))
        for i in ids:
            m_new = jnp.maximum(g[i] + m_prev[i], m_loc[i])
            dec = jnp.exp(g[i] + m_prev[i] - m_new)
            grow = jnp.exp(m_loc[i] - m_new)
            st_s[i] = dec * st_prev[i] + grow * upd[i]
            m_s[i] = jnp.broadcast_to(m_new, (1, LANES))
        return carry
    lax.fori_loop(0, nc, body, 0)
    for i, (b, d) in enumerate(chains):
        cf_ref[b, d, 0] = st_s[i, :, 0:ML_HD].T
        nf_ref[b, d, 0] = st_s[i, :, ML_HD:].T[0:1]
        mf_ref[b, d, 0] = m_s[i]


def mlstm_scan(q, k, v, gates_t, gate_bias, state):
    bn, L, w = q.shape
    c0, n0, m0 = state
    hsp = pl.BlockSpec((bn, L, ML_HD), lambda h: (0, 0, h), pipeline_mode=pl.Buffered(1))
    st = lambda a: pl.BlockSpec((bn, 2, 1) + a.shape[3:], lambda h: (0, 0, h, 0, 0))
    return pl.pallas_call(
        _mlstm_kernel,
        grid=(ML_HEADS,),
        in_specs=[pl.BlockSpec(memory_space=pltpu.SMEM), hsp, hsp, hsp,
                  pl.BlockSpec(gates_t.shape, lambda h: (0, 0, 0, 0)),
                  st(c0), st(n0), st(m0)],
        out_specs=[pl.BlockSpec((2, bn, L, ML_HD), lambda h: (0, 0, 0, h)), st(c0), st(n0), st(m0)],
        out_shape=[jax.ShapeDtypeStruct((2, bn, L, w), F32),
                   jax.ShapeDtypeStruct(c0.shape, F32), jax.ShapeDtypeStruct(n0.shape, F32),
                   jax.ShapeDtypeStruct(m0.shape, F32)],
        scratch_shapes=[pltpu.VMEM((2 * bn, ML_HD, 2 * ML_HD), F32),
                        pltpu.VMEM((2 * bn, 1, LANES), F32),
                        pltpu.VMEM((2 * bn,) + gates_t.shape[2:], F32),
                        pltpu.VMEM((2 * bn,) + gates_t.shape[2:], F32)],
        compiler_params=_cparams(("arbitrary",)),
        name="mlstm_scan",
    )(gate_bias, q, k, v, gates_t, c0, n0, m0)


def _mlout_kernel(hf_ref, hb_ref, o_ref, xc_ref, gain_ref, skip_ref, y_ref):
    h = hf_ref[0, 0] + hb_ref[0, 0]
    parts = []
    for hd in range(ML_HEADS):
        sl = slice(hd * ML_HD, (hd + 1) * ML_HD)
        hh = h[:, sl]
        parts.append(hh * lax.rsqrt(jnp.mean(hh * hh, axis=-1, keepdims=True) + EPS) * gain_ref[:, sl])
    hn = jnp.concatenate(parts, axis=-1)
    y_ref[0] = jax.nn.sigmoid(o_ref[0].astype(F32)) * (hn + skip_ref[...] * xc_ref[0])


def mlstm_output(h2, p_o, xc, gain, skip):
    _, bn, L, w = h2.shape
    tm = _row_tile(L)
    tok = pl.BlockSpec((1, tm, w), lambda b, i: (b, i, 0))
    row = pl.BlockSpec((1, w), lambda b, i: (0, 0))
    return pl.pallas_call(
        _mlout_kernel,
        grid=(bn, L // tm),
        in_specs=[pl.BlockSpec((1, 1, tm, w), lambda b, i: (0, b, i, 0)),
                  pl.BlockSpec((1, 1, tm, w), lambda b, i: (1, b, i, 0)), tok, tok, row, row],
        out_specs=tok,
        out_shape=jax.ShapeDtypeStruct((bn, L, w), F32),
        compiler_params=_cparams(("arbitrary", "arbitrary")),
        name="mlstm_output",
    )(h2, h2, p_o, xc, gain.reshape(1, w), skip.reshape(1, w))


def mlstm_mix(p_x, p_v, p_o, p_g, conv_w9, width, rows, wq, wk, gate_bias, skip, gain, state):
    bn, L, _ = p_x.shape
    xc = depthwise_conv(p_x, conv_w9, width=width, rows=rows, act=_silu)
    q, k = mlstm_qk(xc, wq, wk)
    gates_t = jnp.swapaxes(p_g[..., :ML_GATES], 1, 2).reshape(bn, ML_GATES, L // ML_CHUNK, ML_CHUNK)
    h2, cf, nf, mf = mlstm_scan(q, k, p_v, gates_t, gate_bias.reshape(ML_GATES), state)
    return (h2, p_o, xc, gain, skip), (cf, nf, mf)


PROJ_SPLITS = ((0, 3 * HY_W), (3 * HY_W, 3 * HY_W + S5_W),
               (3 * HY_W + S5_W, 3 * HY_W + S5_W + ML_W),
               (3 * HY_W + S5_W + ML_W, 3 * HY_W + S5_W + 2 * ML_W),
               (3 * HY_W + S5_W + 2 * ML_W, 3 * HY_W + S5_W + 3 * ML_W),
               (3 * HY_W + S5_W + 3 * ML_W, 3 * HY_W + S5_W + 3 * ML_W + LANES))
PROJ_DTYPES = (BF16, F32, BF16, BF16, BF16, F32)


def kernel(x, c, ctx, c_ctx, w_mod, b_mod, g_pre_mix, g_post_mix, g_pre_mlp, g_post_mlp, w_in, w_out, hy_conv, hy_w1, hy_b1, hy_w2, hy_b2, hy_w3, hy_freq, hy_decay, hy_bias, s5_a_re, s5_a_im, s5_log_dt, s5_b_re, s5_b_im, s5_c_re, s5_c_im, s5_d, s5_w_glu, s5_b_glu, ml_conv, ml_wq, ml_wk, ml_gate_bias, ml_skip, ml_norm_gain, w_mlp1, w_mlp2):
    bsz, seq_len, d = x.shape
    ctx_len = ctx.shape[1]
    depth = w_mod.shape[0]

    cc = jnp.zeros((SUBLANES, d), F32).at[:bsz].set(c).at[bsz].set(c_ctx)
    mods = mod_vectors(cc, w_mod, b_mod)

    proj_w = w_in.shape[-1]
    w_in_b = jnp.pad(w_in, ((0, 0), (0, 0), (0, PROJ_SPLITS[-1][1] - proj_w))).astype(BF16)
    w_out_b, w1_b, w2_b = w_out.astype(BF16), w_mlp1.astype(BF16), w_mlp2.astype(BF16)
    wq_b, wk_b, wglu_b = ml_wq.astype(BF16), ml_wk.astype(BF16), s5_w_glu.astype(BF16)

    s5_zero = jnp.zeros((bsz, 1, 4 * S5_GROUPS * S5_N), F32)
    ml_zero = (jnp.zeros((bsz, 2, ML_HEADS, ML_HD, ML_HD), F32),
               jnp.zeros((bsz, 2, ML_HEADS, 1, ML_HD), F32),
               jnp.zeros((bsz, 2, ML_HEADS, 1, LANES), F32))

    for l in range(depth):
        mx = [mods[l, :bsz, j * d:(j + 1) * d][:, None, :] for j in range(6)]
        mc = [jnp.broadcast_to(mods[l, bsz, j * d:(j + 1) * d][None, None, :], (bsz, 1, d)) for j in range(6)]
        filt_x = hyena_filters(seq_len, hy_w1[l], hy_b1[l], hy_w2[l], hy_b2[l], hy_w3[l], hy_freq[l], hy_decay[l])
        filt_c = hyena_filters(ctx_len, hy_w1[l], hy_b1[l], hy_w2[l], hy_b2[l], hy_w3[l], hy_freq[l], hy_decay[l])
        s5_mats = s5_matrices(s5_a_re[l], s5_a_im[l], s5_log_dt[l], s5_b_re[l], s5_b_im[l], s5_c_re[l], s5_c_im[l])
        conv2d_w = ml_conv[l].reshape(9, ML_W)

        def mix(h_in, mod, L, fft_len, filt, width, rows, s5_h0, ml_state):
            p_hy, p_s5, p_mx, p_mv, p_mo, p_mg = in_projection(h_in, g_pre_mix[l], mod[1], mod[0], w_in_b[l],
                                                               PROJ_SPLITS, PROJ_DTYPES)
            y_hy = hyena_mix(p_hy, hy_conv[l], filt, hy_bias[l], fft_len)
            y_s5, s5_fin = s5_mix(p_s5, s5_mats, s5_d[l], wglu_b[l], s5_b_glu[l], s5_h0)
            y_ml, ml_fin = mlstm_mix(p_mx, p_mv, p_mo, p_mg, conv2d_w, width, rows, wq_b[l], wk_b[l],
                                     ml_gate_bias[l], ml_skip[l], ml_norm_gain[l], ml_state)
            return (y_hy, y_s5, y_ml), s5_fin, ml_fin

        y_c, s5_state, ml_state = mix(ctx, mc, ctx_len, CTX_FFT_LEN, filt_c, ctx_len, (0,), s5_zero, ml_zero)
        y_x, _, _ = mix(x, mx, seq_len, seq_len, filt_x, GRID_W, (-1, 0, 1), s5_state, ml_state)
        x = out_projection(*y_x, w_out_b[l], g_post_mix[l], mx[2], x)
        x = mlp_block(x, g_pre_mlp[l], mx[4], mx[3], w1_b[l], w2_b[l], g_post_mlp[l], mx[5])
        if l < depth - 1:
            ctx = out_projection(*y_c, w_out_b[l], g_post_mix[l], mc[2], ctx)
            ctx = mlp_block(ctx, g_pre_mlp[l], mc[4], mc[3], w1_b[l], w2_b[l], g_post_mlp[l], mc[5])
    return x
```
